```python
import math
import jax, jax.numpy as jnp
from jax import lax
import numpy as np

D_MODEL = 2048
BATCH = 4
SEQ = 2048
DEPTH = 4

GRID_W = 64
CTX_LEN = 256
N_MIXERS = 3
EPS = 1e-6

NA_HEADS = 16
NA_HEAD_DIM = D_MODEL // NA_HEADS
NA_WIN_ROWS = 8
NA_WIN_COLS = 16

SSD_D_INNER = 2 * D_MODEL
SSD_HEAD_DIM = 64
SSD_HEADS = SSD_D_INNER // SSD_HEAD_DIM
SSD_GROUPS = 8
SSD_STATE = 128
SSD_CONV_W = 3
SSD_CHUNK = 128
SSD_GN = SSD_GROUPS * SSD_STATE
SSD_CONV_CH = SSD_D_INNER + 2 * SSD_GN
SSD_IN_COLS = SSD_D_INNER + SSD_CONV_CH + 2 * SSD_HEADS

SG_HALF = 3 * D_MODEL
SG_GROUPS = 16
SG_CHUNK = 128

N_A = (DEPTH + 2) // 3
N_B = (DEPTH + 1) // 3
N_C = DEPTH // 3

kernel_name = "hybrid_na_ssd_gmlp_prefix_dit"


def rmsnorm(x, g):
    xf = x.astype(jnp.float32)
    y = xf * lax.rsqrt(jnp.mean(xf * xf, axis=-1, keepdims=True) + EPS)
    return (y * g.astype(jnp.float32)).astype(x.dtype)


def layernorm(x, g, b):
    xf = x.astype(jnp.float32)
    mu = jnp.mean(xf, axis=-1, keepdims=True)
    xc = xf - mu
    var = jnp.mean(xc * xc, axis=-1, keepdims=True)
    return (xc * lax.rsqrt(var + EPS) * g.astype(jnp.float32) + b.astype(jnp.float32)).astype(x.dtype)


def window_starts(n, win):
    return jnp.clip(jnp.arange(n) - win // 2, 0, n - win)


def centred_depthwise_conv(u, w, b):
    k = w.shape[0]
    pad = k // 2
    out = lax.conv_general_dilated(
        u, w[:, None, :], window_strides=(1,), padding=[(pad, k - 1 - pad)],
        dimension_numbers=("NWC", "WIO", "NWC"), feature_group_count=u.shape[-1])
    return out + b


def neighbourhood_attention(h_lat, h_ctx, w_in, rpb, w_out, need_ctx):
    bsz, seq, _ = h_lat.shape
    rows = seq // GRID_W
    kr = min(NA_WIN_ROWS, rows)
    scale = NA_HEAD_DIM ** -0.5
    q, k, v, z = jnp.split(h_lat @ w_in, 4, axis=-1)
    grid = lambda t: t.reshape(bsz, rows, GRID_W, NA_HEADS, NA_HEAD_DIM)
    q, k, v = grid(q) * scale, grid(k), grid(v)
    heads = lambda t: t.reshape(bsz, t.shape[1], NA_HEADS, NA_HEAD_DIM)
    if need_ctx:
        qc, kc, vc, zc = jnp.split(h_ctx @ w_in, 4, axis=-1)
    else:
        kc, vc = jnp.split(h_ctx @ w_in[:, D_MODEL:3 * D_MODEL], 2, axis=-1)
    kc, vc = heads(kc), heads(vc)

    row_idx = window_starts(rows, kr)[:, None] + jnp.arange(kr)[None, :]
    k_g = jnp.take(k, row_idx, axis=1)
    v_g = jnp.take(v, row_idx, axis=1)
    s_loc = jnp.einsum("brqhd,brjkhd->bhrqjk", q, k_g).astype(jnp.float32)

    dr = row_idx - jnp.arange(rows)[:, None]
    qcol = jnp.arange(GRID_W)
    c_start = window_starts(GRID_W, NA_WIN_COLS)
    col_ok = (qcol[None, :] >= c_start[:, None]) & (qcol[None, :] < c_start[:, None] + NA_WIN_COLS)
    dc = jnp.clip(qcol[None, :] - qcol[:, None], -(NA_WIN_COLS - 1), NA_WIN_COLS - 1)
    bias = rpb[:, dr[:, None, :, None] + (NA_WIN_ROWS - 1), dc[None, :, None, :] + (NA_WIN_COLS - 1)]
    s_loc = s_loc + bias[None].astype(jnp.float32)
    s_loc = jnp.where(col_ok[None, None, None, :, None, :], s_loc, -jnp.inf)

    s_ctx = jnp.einsum("brqhd,blhd->bhrql", q, kc).astype(jnp.float32)
    n_loc = kr * GRID_W
    p = jax.nn.softmax(jnp.concatenate(
        [s_loc.reshape(bsz, NA_HEADS, rows, GRID_W, n_loc), s_ctx], axis=-1), axis=-1).astype(v.dtype)
    p_loc = p[..., :n_loc].reshape(bsz, NA_HEADS, rows, GRID_W, kr, GRID_W)
    p_ctx = p[..., n_loc:]
    o = (jnp.einsum("bhrqjk,brjkhd->brqhd", p_loc, v_g)
         + jnp.einsum("bhrql,blhd->brqhd", p_ctx, vc))
    y_lat = (o.reshape(bsz, seq, D_MODEL) * jax.nn.silu(z)) @ w_out

    y_ctx = None
    if need_ctx:
        qc = heads(qc) * scale
        pc = jax.nn.softmax(jnp.einsum("bqhd,bkhd->bhqk", qc, kc).astype(jnp.float32), axis=-1).astype(vc.dtype)
        oc = jnp.einsum("bhqk,bkhd->bqhd", pc, vc).reshape(bsz, h_ctx.shape[1], D_MODEL)
        y_ctx = (oc * jax.nn.silu(zc)) @ w_out
    return y_lat, y_ctx


def ssd_scan(x, da, bm, cm, init_state):
    b, l, h, p = x.shape
    g, n = bm.shape[2], bm.shape[3]
    e = h // g
    q = SSD_CHUNK
    c = l // q
    x = x.reshape(b, c, q, g, e, p)
    bm = bm.reshape(b, c, q, g, n)
    cm = cm.reshape(b, c, q, g, n)
    a_cum = jnp.cumsum(da.astype(jnp.float32).reshape(b, c, q, g, e), axis=2)

    seg = a_cum[:, :, :, None] - a_cum[:, :, None, :]
    lower = jnp.tril(jnp.ones((q, q), dtype=bool))
    decay = jnp.exp(jnp.where(lower[:, :, None, None], seg, -jnp.inf))
    cb = jnp.einsum("bclgn,bcsgn->bclsg", cm, bm)
    y_diag = jnp.einsum("bclsge,bcsgep->bclgep", cb[..., None] * decay, x)

    decay_states = jnp.exp(a_cum[:, :, -1:] - a_cum)
    states = jnp.einsum("bcqgn,bcqge,bcqgep->bcgepn", bm, decay_states, x)
    chunk_decay = jnp.exp(a_cum[:, :, -1])

    def step(carry, inp):
        st, dec = inp
        return carry * dec[..., None, None] + st, carry

    init = init_state.astype(jnp.float32).reshape(b, g, e, p, n)
    final, prev = lax.scan(step, init, (jnp.moveaxis(states.astype(jnp.float32), 1, 0),
                                        jnp.moveaxis(chunk_decay, 1, 0)))
    prev = jnp.moveaxis(prev, 0, 1)
    y_off = jnp.einsum("bcqgn,bcgepn,bcqge->bcqgep", cm, prev, jnp.exp(a_cum))
    y = (y_diag + y_off).reshape(b, l, h, p)
    return y, final.reshape(b, h, p, n)


def ssd_mixer(h_lat, h_ctx, w_in, conv_w, conv_b, dt_bias, a_log, d_skip, out_norm_g, w_out, need_ctx):
    a = -jnp.exp(a_log.astype(jnp.float32))

    def project(h):
        bsz, l, _ = h.shape
        proj = h @ w_in
        z = proj[..., :SSD_D_INNER]
        xbc = jax.nn.silu(centred_depthwise_conv(
            proj[..., SSD_D_INNER:SSD_D_INNER + SSD_CONV_CH], conv_w, conv_b))
        xs = xbc[..., :SSD_D_INNER].reshape(bsz, l, SSD_HEADS, SSD_HEAD_DIM)
        bm = xbc[..., SSD_D_INNER:SSD_D_INNER + SSD_GN].reshape(bsz, l, SSD_GROUPS, SSD_STATE)
        cm = xbc[..., SSD_D_INNER + SSD_GN:].reshape(bsz, l, SSD_GROUPS, SSD_STATE)
        dt_raw = proj[..., SSD_D_INNER + SSD_CONV_CH:].reshape(bsz, l, 2, SSD_HEADS)
        dt = jax.nn.softplus((dt_raw + dt_bias).astype(jnp.float32))
        return z, xs, bm, cm, dt

    def scan_dir(xs, bm, cm, dt, d, init, reverse):
        xd = xs * dt[:, :, d, :, None]
        da = dt[:, :, d, :] * a[d]
        if reverse:
            flip = lambda t: jnp.flip(t, axis=1)
            y, fin = ssd_scan(flip(xd), flip(da), flip(bm), flip(cm), init)
            return flip(y), fin
        return ssd_scan(xd, da, bm, cm, init)

    z_c, x_c, b_c, c_c, dt_c = project(h_ctx)
    z_l, x_l, b_l, c_l, dt_l = project(h_lat)
    zero = jnp.zeros((h_lat.shape[0], SSD_HEADS, SSD_HEAD_DIM, SSD_STATE), jnp.float32)
    yc_f, sc_f = scan_dir(x_c, b_c, c_c, dt_c, 0, zero, False)
    yc_b, sc_b = scan_dir(x_c, b_c, c_c, dt_c, 1, zero, True)
    yl_f, _ = scan_dir(x_l, b_l, c_l, dt_l, 0, sc_f, False)
    yl_b, _ = scan_dir(x_l, b_l, c_l, dt_l, 1, sc_b, True)

    def finish(y, xs, z):
        y = y + xs * d_skip[:, None]
        y = y.reshape(z.shape).astype(z.dtype)
        return rmsnorm(y * jax.nn.silu(z), out_norm_g) @ w_out

    y_lat = finish(yl_f + yl_b, x_l, z_l)
    y_ctx = finish(yc_f + yc_b, x_c, z_c) if need_ctx else None
    return y_lat, y_ctx


def spatial_gating_mixer(h_lat, h_ctx, w_in, ln_g, ln_b, w_s, b_s, w_out, need_ctx):
    def branch(h):
        bsz, l, _ = h.shape
        proj = h @ w_in
        z = proj[..., :SG_HALF]
        u, v = jnp.split(jax.nn.gelu(proj[..., SG_HALF:]), 2, axis=-1)
        v = layernorm(v, ln_g, ln_b)
        v = v.reshape(bsz, l // SG_CHUNK, SG_CHUNK, SG_GROUPS, SG_HALF // SG_GROUPS)
        sv = jnp.einsum("gts,bcsgk->bctgk", w_s, v) + jnp.transpose(b_s)[None, None, :, :, None]
        y = u * sv.reshape(bsz, l, SG_HALF) * jax.nn.silu(z)
        return y @ w_out

    y_lat = branch(h_lat)
    y_ctx = branch(h_ctx) if need_ctx else None
    return y_lat, y_ctx


def setup_inputs(seed: int = 0) -> dict:
    key = jax.random.key(seed)
    ks = jax.random.split(key, 25)
    nrm = lambda k, shape, s: jax.random.normal(k, shape, jnp.float32) * s
    d = D_MODEL
    u_dt = jax.random.uniform(ks[13], (N_B, 2, SSD_HEADS), jnp.float32)
    dt0 = jnp.exp(u_dt * (math.log(0.1) - math.log(1e-3)) + math.log(1e-3))
    return {
        "x": nrm(ks[0], (BATCH, SEQ, d), 1.0),
        "c": nrm(ks[1], (BATCH, d), 1.0),
        "ctx": nrm(ks[2], (BATCH, CTX_LEN, d), 1.0),
        "c_ctx": nrm(ks[3], (d,), 1.0),
        "norm_g": 1.0 + nrm(ks[4], (DEPTH, d), 0.02),
        "ada_w": nrm(ks[5], (DEPTH, d, 3 * d), d ** -0.5),
        "ada_b": nrm(ks[6], (DEPTH, 3 * d), 0.02),
        "na_w_in": nrm(ks[7], (N_A, d, 4 * d), d ** -0.5),
        "na_rpb": nrm(ks[8], (N_A, NA_HEADS, 2 * NA_WIN_ROWS - 1, 2 * NA_WIN_COLS - 1), 0.2),
        "na_w_out": nrm(ks[9], (N_A, d, d), d ** -0.5),
        "ssd_w_in": nrm(ks[10], (N_B, d, SSD_IN_COLS), d ** -0.5),
        "ssd_conv_w": nrm(ks[11], (N_B, SSD_CONV_W, SSD_CONV_CH), SSD_CONV_W ** -0.5),
        "ssd_conv_b": nrm(ks[12], (N_B, SSD_CONV_CH), 0.02),
        "ssd_dt_bias": dt0 + jnp.log(-jnp.expm1(-dt0)),
        "ssd_a_log": jnp.log(jax.random.uniform(ks[14], (N_B, 2, SSD_HEADS), jnp.float32, 1.0, 16.0)),
        "ssd_d_skip": 1.0 + nrm(ks[15], (N_B, SSD_HEADS), 0.02),
        "ssd_norm_g": 1.0 + nrm(ks[16], (N_B, SSD_D_INNER), 0.02),
        "ssd_w_out": nrm(ks[17], (N_B, SSD_D_INNER, d), SSD_D_INNER ** -0.5),
        "sg_w_in": nrm(ks[18], (N_C, d, 3 * SG_HALF), d ** -0.5),
        "sg_ln_g": 1.0 + nrm(ks[19], (N_C, SG_HALF), 0.02),
        "sg_ln_b": nrm(ks[20], (N_C, SG_HALF), 0.02),
        "sg_w_s": nrm(ks[21], (N_C, SG_GROUPS, SG_CHUNK, SG_CHUNK), SG_CHUNK ** -0.5),
        "sg_b_s": 1.0 + nrm(ks[22], (N_C, SG_GROUPS, SG_CHUNK), 0.02),
        "sg_w_out": nrm(ks[23], (N_C, SG_HALF, d), SG_HALF ** -0.5),
        "final_norm_g": 1.0 + nrm(ks[24], (d,), 0.02),
    }


def reference(x, c, ctx, c_ctx, norm_g, ada_w, ada_b, na_w_in, na_rpb, na_w_out,
              ssd_w_in, ssd_conv_w, ssd_conv_b, ssd_dt_bias, ssd_a_log, ssd_d_skip,
              ssd_norm_g, ssd_w_out, sg_w_in, sg_ln_g, sg_ln_b, sg_w_s, sg_b_s, sg_w_out,
              final_norm_g):
    h_lat, h_ctx = x, ctx
    for i in range(DEPTH):
        kind, j = i % N_MIXERS, i // N_MIXERS
        need_ctx = i < DEPTH - 1
        mod_lat = (jax.nn.silu(c) @ ada_w[i] + ada_b[i])[:, None, :]
        mod_ctx = jax.nn.silu(c_ctx) @ ada_w[i] + ada_b[i]
        sh_l, sc_l, g_l = jnp.split(mod_lat, 3, axis=-1)
        sh_c, sc_c, g_c = jnp.split(mod_ctx, 3, axis=-1)
        n_lat = rmsnorm(h_lat, norm_g[i]) * (1.0 + sc_l) + sh_l
        n_ctx = rmsnorm(h_ctx, norm_g[i]) * (1.0 + sc_c) + sh_c
        if kind == 0:
            y_lat, y_ctx = neighbourhood_attention(n_lat, n_ctx, na_w_in[j], na_rpb[j], na_w_out[j], need_ctx)
        elif kind == 1:
            y_lat, y_ctx = ssd_mixer(n_lat, n_ctx, ssd_w_in[j], ssd_conv_w[j], ssd_conv_b[j], ssd_dt_bias[j],
                                     ssd_a_log[j], ssd_d_skip[j], ssd_norm_g[j], ssd_w_out[j], need_ctx)
        else:
            y_lat, y_ctx = spatial_gating_mixer(n_lat, n_ctx, sg_w_in[j], sg_ln_g[j], sg_ln_b[j],
                                                sg_w_s[j], sg_b_s[j], sg_w_out[j], need_ctx)
        h_lat = h_lat + g_l * y_lat
        if need_ctx:
            h_ctx = h_ctx + g_c * y_ctx
    return rmsnorm(h_lat, final_norm_g)
```

```python
import functools
import math

import numpy as np
import jax
import jax.numpy as jnp
from jax import lax
from jax.experimental import pallas as pl
from jax.experimental.pallas import tpu as pltpu

D_MODEL = 2048
BATCH = 4
SEQ = 2048
DEPTH = 4
GRID_W = 64
CTX_LEN = 256
N_MIXERS = 3
EPS = 1e-6

NA_HEADS = 16
NA_HEAD_DIM = D_MODEL // NA_HEADS
NA_WIN_ROWS = 8
NA_WIN_COLS = 16
NA_ROWS = SEQ // GRID_W
NA_QROWS = 4
NA_KROWS = NA_QROWS + NA_WIN_ROWS
NA_TQ = NA_QROWS * GRID_W
NA_TK = NA_KROWS * GRID_W
NA_TILES = NA_ROWS // NA_QROWS

SSD_D_INNER = 2 * D_MODEL
SSD_HEAD_DIM = 64
SSD_HEADS = SSD_D_INNER // SSD_HEAD_DIM
SSD_GROUPS = 8
SSD_STATE = 128
SSD_CHUNK = 128
SSD_GN = SSD_GROUPS * SSD_STATE
SSD_CONV_CH = SSD_D_INNER + 2 * SSD_GN
SSD_HPG = SSD_HEADS // SSD_GROUPS
SSD_GW = SSD_HPG * SSD_HEAD_DIM
SSD_LTOT = CTX_LEN + SEQ
SSD_NCHUNK = SSD_LTOT // SSD_CHUNK
SSD_CCHUNK = CTX_LEN // SSD_CHUNK

SG_HALF = 3 * D_MODEL
SG_GROUPS = 16
SG_CHUNK = 128
SG_GW = SG_HALF // SG_GROUPS

NEG = -1e30
VMEM_LIMIT = 56 * 1024 * 1024

F32 = jnp.float32
BF16 = jnp.bfloat16


def _cparams(sem):
    return pltpu.CompilerParams(dimension_semantics=sem, vmem_limit_bytes=VMEM_LIMIT)


def _silu(x):
    return x * (1.0 / (1.0 + jnp.exp(-x)))


def _gelu_tanh(x):
    c = math.sqrt(2.0 / math.pi)
    return 0.5 * x * (1.0 + jnp.tanh(c * (x + 0.044715 * (x * x * x))))


def _softplus(x):
    return jnp.maximum(x, 0.0) + jnp.log(1.0 + jnp.exp(-jnp.abs(x)))


def _ada_kernel(c_ref, w_ref, b_ref, o_ref):
    a = _silu(c_ref[...]).astype(BF16)
    o_ref[0] = jnp.dot(a, w_ref[0].astype(BF16), preferred_element_type=F32) + b_ref[0]


def _ada_all(c_rows, ada_w, ada_b):
    tn = 1024
    return pl.pallas_call(
        _ada_kernel,
        out_shape=jax.ShapeDtypeStruct((DEPTH, 8, 3 * D_MODEL), F32),
        grid=(DEPTH, 3 * D_MODEL // tn),
        in_specs=[pl.BlockSpec((8, D_MODEL), lambda l, j: (0, 0)),
                  pl.BlockSpec((1, D_MODEL, tn), lambda l, j: (l, 0, j)),
                  pl.BlockSpec((1, 1, tn), lambda l, j: (l, 0, j))],
        out_specs=pl.BlockSpec((1, 8, tn), lambda l, j: (l, 0, j)),
        compiler_params=_cparams(("arbitrary", "arbitrary")),
        name="ada_mod",
    )(c_rows, ada_w, ada_b.reshape(DEPTH, 1, 3 * D_MODEL))


def _rms_kernel(x_ref, g_ref, *rest, modulate):
    o_ref = rest[-1]
    x = x_ref[...].astype(F32)
    y = x * lax.rsqrt(jnp.mean(x * x, axis=-1, keepdims=True) + EPS) * g_ref[...]
    if modulate:
        sc_ref, sh_ref = rest[0], rest[1]
        y = y * (1.0 + sc_ref[0]) + sh_ref[0]
    o_ref[...] = y.astype(o_ref.dtype)


def _rmsnorm(x, g, scale=None, shift=None, *, out_dtype, tr=512):
    rows, w = x.shape
    modulate = scale is not None
    in_specs = [pl.BlockSpec((tr, w), lambda i: (i, 0)),
                pl.BlockSpec((1, w), lambda i: (0, 0))]
    args = [x, g.reshape(1, w)]
    if modulate:
        bpb = rows // scale.shape[0] // tr
        spec = pl.BlockSpec((1, 1, w), lambda i: (i // bpb, 0, 0))
        in_specs += [spec, spec]
        args += [scale, shift]
    return pl.pallas_call(
        functools.partial(_rms_kernel, modulate=modulate),
        out_shape=jax.ShapeDtypeStruct((rows, w), out_dtype),
        grid=(rows // tr,),
        in_specs=in_specs,
        out_specs=pl.BlockSpec((tr, w), lambda i: (i, 0)),
        compiler_params=_cparams(("parallel",)),
        name="rmsnorm",
    )(*args)


def _mm_kernel(a_ref, w_ref, *rest, nk, gelu_from, resid):
    if resid:
        h_ref, g_ref = rest[0], rest[1]
        rest = rest[2:]
    o_ref = rest[0]
    acc_ref = rest[1] if nk > 1 else None

    def finish(acc):
        if resid:
            o_ref[...] = (h_ref[...] + g_ref[0] * acc).astype(o_ref.dtype)
        elif gelu_from is not None:
            j = pl.program_id(1)

            @pl.when(j >= gelu_from)
            def _():
                o_ref[...] = _gelu_tanh(acc).astype(o_ref.dtype)

            @pl.when(j < gelu_from)
            def _():
                o_ref[...] = acc.astype(o_ref.dtype)
        else:
            o_ref[...] = acc.astype(o_ref.dtype)

    part = jnp.dot(a_ref[...], w_ref[...], preferred_element_type=F32)
    if nk == 1:
        finish(part)
    else:
        k = pl.program_id(2)

        @pl.when(k == 0)
        def _():
            acc_ref[...] = part

        @pl.when(jnp.logical_and(k > 0, k < nk - 1))
        def _():
            acc_ref[...] += part

        @pl.when(k == nk - 1)
        def _():
            finish(acc_ref[...] + part)


def _matmul(a, w, *, out_dtype, tm, tn, tk, gelu_from_col=None, resid=None, gate=None):
    m, kdim = a.shape
    n = w.shape[1]
    nk = kdim // tk
    in_specs = [pl.BlockSpec((tm, tk), lambda i, j, k: (i, k)),
                pl.BlockSpec((tk, tn), lambda i, j, k: (k, j))]
    args = [a, w]
    if resid is not None:
        bpb = m // gate.shape[0] // tm
        in_specs += [pl.BlockSpec((tm, tn), lambda i, j, k: (i, j)),
                     pl.BlockSpec((1, 1, tn), lambda i, j, k: (i // bpb, 0, j))]
        args += [resid, gate]
    scratch = [pltpu.VMEM((tm, tn), F32)] if nk > 1 else []
    return pl.pallas_call(
        functools.partial(_mm_kernel, nk=nk,
                          gelu_from=None if gelu_from_col is None else gelu_from_col // tn,
                          resid=resid is not None),
        out_shape=jax.ShapeDtypeStruct((m, n), out_dtype),
        grid=(m // tm, n // tn, nk),
        in_specs=in_specs,
        out_specs=pl.BlockSpec((tm, tn), lambda i, j, k: (i, j)),
        scratch_shapes=scratch,
        compiler_params=_cparams(("parallel", "parallel", "arbitrary")),
        name="matmul",
    )(*args)


def _na_bias_table(rpb):
    qcol = np.arange(GRID_W)
    c_start = np.clip(qcol - NA_WIN_COLS // 2, 0, GRID_W - NA_WIN_COLS)
    col_ok = (qcol[None, :] >= c_start[:, None]) & (qcol[None, :] < c_start[:, None] + NA_WIN_COLS)
    dc = np.clip(qcol[None, :] - qcol[:, None], -(NA_WIN_COLS - 1), NA_WIN_COLS - 1) + NA_WIN_COLS - 1
    idx_r, ok_r = [], []
    for t in (0, 1, NA_TILES - 1):
        ws = min(max(NA_QROWS * t - NA_WIN_ROWS // 2, 0), NA_ROWS - NA_KROWS)
        r = NA_QROWS * t + np.arange(NA_QROWS)
        rs = np.clip(r - NA_WIN_ROWS // 2, 0, NA_ROWS - NA_WIN_ROWS)
        kabs = ws + np.arange(NA_KROWS)
        ok_r.append((kabs[None, :] >= rs[:, None]) & (kabs[None, :] < rs[:, None] + NA_WIN_ROWS))
        idx_r.append(np.clip(kabs[None, :] - r[:, None], -(NA_WIN_ROWS - 1), NA_WIN_ROWS - 1) + NA_WIN_ROWS - 1)
    idx_r = np.stack(idx_r)
    ok_r = np.stack(ok_r)
    by_col = jnp.where(col_ok[None, None], rpb[:, :, dc], NEG)
    tab = by_col[:, idx_r]
    tab = jnp.where(ok_r[None, :, :, :, None, None], tab, NEG)
    tab = tab.transpose(0, 1, 2, 4, 3, 5)
    return tab.reshape(NA_HEADS, 3, NA_TQ, NA_TK).astype(F32)


_NT = (((1,), (1,)), ((), ()))


def _na_kernel(q_ref, k_ref, v_ref, z_ref, kc_ref, vc_ref, bias_ref, o_ref):
    kc = kc_ref[...]
    vc = vc_ref[...]

    def tile(t, carry):
        r0 = pl.multiple_of(t * NA_TQ, NA_TQ)
        ws = jnp.clip(NA_QROWS * t - NA_WIN_ROWS // 2, 0, NA_ROWS - NA_KROWS)
        ks = pl.multiple_of(ws * GRID_W, GRID_W)
        pat = jnp.where(t == 0, 0, jnp.where(t == NA_TILES - 1, 2, 1))
        q = q_ref[pl.ds(r0, NA_TQ), :]
        kw = k_ref[pl.ds(ks, NA_TK), :]
        vw = v_ref[pl.ds(ks, NA_TK), :]
        s1 = lax.dot_general(q, kw, _NT, preferred_element_type=F32) + bias_ref[0, pat]
        s2 = lax.dot_general(q, kc, _NT, preferred_element_type=F32)
        m = jnp.maximum(jnp.max(s1, axis=-1, keepdims=True), jnp.max(s2, axis=-1, keepdims=True))
        p1 = jnp.exp(s1 - m)
        p2 = jnp.exp(s2 - m)
        l = jnp.sum(p1, axis=-1, keepdims=True) + jnp.sum(p2, axis=-1, keepdims=True)
        o = (jnp.dot(p1.astype(BF16), vw, preferred_element_type=F32)
             + jnp.dot(p2.astype(BF16), vc, preferred_element_type=F32)) / l
        z = z_ref[pl.ds(r0, NA_TQ), :].astype(F32)
        o_ref[pl.ds(r0, NA_TQ), :] = (o * _silu(z)).astype(o_ref.dtype)
        return carry

    lax.fori_loop(0, NA_TILES, tile, 0)


def _na_latent(p_lat, p_ctx, bias_tab, kc_col, vc_col):
    hd = NA_HEAD_DIM
    lat = lambda off: pl.BlockSpec((SEQ, hd), lambda h, b: (b, off + h))
    ctx = lambda off: pl.BlockSpec((CTX_LEN, hd), lambda h, b: (b, off + h))
    return pl.pallas_call(
        _na_kernel,
        out_shape=jax.ShapeDtypeStruct((BATCH * SEQ, D_MODEL), BF16),
        grid=(NA_HEADS, BATCH),
        in_specs=[lat(0), lat(NA_HEADS), lat(2 * NA_HEADS), lat(3 * NA_HEADS),
                  ctx(kc_col), ctx(vc_col),
                  pl.BlockSpec((1, 3, NA_TQ, NA_TK), lambda h, b: (h, 0, 0, 0))],
        out_specs=pl.BlockSpec((SEQ, hd), lambda h, b: (b, h)),
        compiler_params=_cparams(("parallel", "parallel")),
        name="na_latent",
    )(p_lat, p_lat, p_lat, p_lat, p_ctx, p_ctx, bias_tab)


def _na_ctx_kernel(q_ref, k_ref, v_ref, z_ref, o_ref):
    s = lax.dot_general(q_ref[...], k_ref[...], _NT, preferred_element_type=F32)
    p = jnp.exp(s - jnp.max(s, axis=-1, keepdims=True))
    l = jnp.sum(p, axis=-1, keepdims=True)
    o = jnp.dot(p.astype(BF16), v_ref[...], preferred_element_type=F32) / l
    o_ref[...] = (o * _silu(z_ref[...].astype(F32))).astype(o_ref.dtype)


def _na_context(p_ctx):
    hd = NA_HEAD_DIM
    blk = lambda off: pl.BlockSpec((CTX_LEN, hd), lambda h, b: (b, off + h))
    return pl.pallas_call(
        _na_ctx_kernel,
        out_shape=jax.ShapeDtypeStruct((BATCH * CTX_LEN, D_MODEL), BF16),
        grid=(NA_HEADS, BATCH),
        in_specs=[blk(0), blk(NA_HEADS), blk(2 * NA_HEADS), blk(3 * NA_HEADS)],
        out_specs=blk(0),
        compiler_params=_cparams(("parallel", "parallel")),
        name="na_context",
    )(p_ctx, p_ctx, p_ctx, p_ctx)


def _conv_kernel(uc_ref, ul_ref, w_ref, b_ref, o_ref):
    w = w_ref[...]
    b = b_ref[...]

    def conv(u_ref, r0, rows):
        u = u_ref[...].astype(F32)
        row = lax.broadcasted_iota(jnp.int32, u.shape, 0)
        up = jnp.where(row == 0, 0.0, pltpu.roll(u, 1, 0))
        un = jnp.where(row == rows - 1, 0.0, pltpu.roll(u, rows - 1, 0))
        y = w[0:1] * up + w[1:2] * u + w[2:3] * un + b
        o_ref[0, r0:r0 + rows, :] = _silu(y).astype(o_ref.dtype)

    conv(uc_ref, 0, CTX_LEN)
    conv(ul_ref, CTX_LEN, SEQ)


def _ssd_conv(p_ctx, p_lat, conv_w, conv_b):
    tc = 256
    off = SSD_D_INNER // tc
    return pl.pallas_call(
        _conv_kernel,
        out_shape=jax.ShapeDtypeStruct((BATCH, SSD_LTOT, SSD_CONV_CH), BF16),
        grid=(BATCH, SSD_CONV_CH // tc),
        in_specs=[pl.BlockSpec((CTX_LEN, tc), lambda b, j: (b, off + j)),
                  pl.BlockSpec((SEQ, tc), lambda b, j: (b, off + j)),
                  pl.BlockSpec((3, tc), lambda b, j: (0, j)),
                  pl.BlockSpec((1, tc), lambda b, j: (0, j))],
        out_specs=pl.BlockSpec((1, SSD_LTOT, tc), lambda b, j: (b, 0, j)),
        compiler_params=_cparams(("parallel", "parallel")),
        name="ssd_conv",
    )(p_ctx, p_lat, conv_w, conv_b.reshape(1, SSD_CONV_CH))


def _ssd_kernel(x_ref, b_ref, c_ref, zc_ref, zl_ref, dtc_ref, dtr_ref,
                pc_ref, pr_ref, dskip_ref, yc_ref, yl_ref, yacc_ref, s_ref):
    q = SSD_CHUNK
    hp = SSD_HPG
    li = lax.broadcasted_iota(jnp.int32, (q, q), 0)
    si = lax.broadcasted_iota(jnp.int32, (q, q), 1)
    lower = li >= si
    upper = li <= si
    tri_lo = lower.astype(F32)
    tri_up = upper.astype(F32)
    left = lax.broadcasted_iota(jnp.int32, (q, q), 1) < SSD_HEAD_DIM
    left_row = lax.broadcasted_iota(jnp.int32, (1, q), 1) < SSD_HEAD_DIM

    bias_c = pc_ref[0, 0:1, :]
    a_c = -jnp.exp(pc_ref[0, 1:2, :])
    bias_r = pr_ref[0, :, 0:1]
    a_r = -jnp.exp(pr_ref[0, :, 1:2])

    yacc_ref[...] = x_ref[0].astype(F32) * dskip_ref[0]
    s_ref[...] = jnp.zeros_like(s_ref)

    def pair(mask, a, b):
        return jnp.where(mask, a, b)

    def chunk(c, d):
        r0 = pl.multiple_of(c * q, q)
        tri = tri_lo if d == 0 else tri_up
        tri_t = tri_up if d == 0 else tri_lo
        mask = lower if d == 0 else upper
        xb = x_ref[0, pl.ds(r0, q), :]
        bm = b_ref[0, pl.ds(r0, q), :]
        cm = c_ref[0, pl.ds(r0, q), :]
        dtc = _softplus(dtc_ref[0, 0, pl.ds(r0, q), :] + bias_c)
        acum_c = jnp.dot(tri, dtc * a_c, preferred_element_type=F32,
                         precision=lax.Precision.HIGHEST)
        dtr = _softplus(dtr_ref[0, 0, c] + bias_r)
        acum_r = jnp.dot(dtr * a_r, tri_t, preferred_element_type=F32,
                         precision=lax.Precision.HIGHEST)
        tot_c = acum_c[q - 1:q, :] if d == 0 else acum_c[0:1, :]
        ds_c = jnp.exp(tot_c - acum_c)
        ea_c = jnp.exp(acum_c)
        etot_c = jnp.exp(tot_c)
        cb = lax.dot_general(cm, bm, _NT, preferred_element_type=F32)
        bt = bm.astype(F32).T.astype(BF16)
        sprev = s_ref[d]
        yoff = jnp.dot(cm, sprev.astype(BF16), preferred_element_type=F32)
        for p in range(hp // 2):
            l0 = d * hp + 2 * p
            l1 = l0 + 1
            cs = slice(p * q, (p + 1) * q)
            xd = xb[:, cs].astype(F32) * pair(left, dtc[:, l0:l0 + 1], dtc[:, l1:l1 + 1])
            xd_b = xd.astype(BF16)
            yd = []
            for ln in (l0, l1):
                diff = acum_c[:, ln:ln + 1] - acum_r[ln:ln + 1, :]
                m_h = (cb * jnp.exp(jnp.where(mask, diff, NEG))).astype(BF16)
                yd.append(jnp.dot(m_h, xd_b, preferred_element_type=F32))
            y = pair(left, yd[0], yd[1]) + pair(left, ea_c[:, l0:l0 + 1], ea_c[:, l1:l1 + 1]) * yoff[:, cs]
            yacc_ref[pl.ds(r0, q), cs] += y
            xs = (xd * pair(left, ds_c[:, l0:l0 + 1], ds_c[:, l1:l1 + 1])).astype(BF16)
            contrib = jnp.dot(bt, xs, preferred_element_type=F32)
            s_ref[d, :, cs] = sprev[:, cs] * pair(left_row, etot_c[:, l0:l0 + 1], etot_c[:, l1:l1 + 1]) + contrib

    def step(k, carry):
        chunk(k, 0)
        chunk(jnp.where(k < SSD_CCHUNK, SSD_CCHUNK - 1 - k, SSD_NCHUNK - 1 + SSD_CCHUNK - k), 1)
        return carry

    lax.fori_loop(0, SSD_NCHUNK, step, 0)

    yc_ref[...] = (yacc_ref[0:CTX_LEN, :] * _silu(zc_ref[...].astype(F32))).astype(yc_ref.dtype)
    yl_ref[...] = (yacc_ref[CTX_LEN:, :] * _silu(zl_ref[...].astype(F32))).astype(yl_ref.dtype)


def _ssd_scan(xbc, p_ctx, p_lat, dt_ctx, dt_lat, dt_bias, a_log, d_skip):
    g, hp = SSD_GROUPS, SSD_HPG
    dt = jnp.concatenate([dt_ctx.reshape(BATCH, CTX_LEN, 2, g, hp), dt_lat.reshape(BATCH, SEQ, 2, g, hp)], axis=1)
    dt_col = dt.transpose(0, 3, 1, 2, 4).reshape(BATCH, g, SSD_LTOT, 2 * hp)
    dt_row = dt_col.reshape(BATCH, g, SSD_NCHUNK, SSD_CHUNK, 2 * hp).transpose(0, 1, 2, 4, 3)
    par = jnp.stack([dt_bias, a_log]).reshape(2, 2, g, hp).transpose(2, 0, 1, 3).reshape(g, 2, 2 * hp)
    par_col = jnp.pad(par, ((0, 0), (0, 6), (0, 0)))
    par_row = jnp.pad(par.transpose(0, 2, 1), ((0, 0), (0, 0), (0, 126)))
    dskip = jnp.repeat(d_skip, SSD_HEAD_DIM).reshape(g, 1, SSD_GW)
    xoff = SSD_D_INNER // SSD_STATE
    return pl.pallas_call(
        _ssd_kernel,
        out_shape=(jax.ShapeDtypeStruct((BATCH * CTX_LEN, SSD_D_INNER), BF16),
                   jax.ShapeDtypeStruct((BATCH * SEQ, SSD_D_INNER), BF16)),
        grid=(BATCH, g),
        in_specs=[pl.BlockSpec((1, SSD_LTOT, SSD_GW), lambda b, j: (b, 0, j)),
                  pl.BlockSpec((1, SSD_LTOT, SSD_STATE), lambda b, j: (b, 0, xoff + j)),
                  pl.BlockSpec((1, SSD_LTOT, SSD_STATE), lambda b, j: (b, 0, xoff + g + j)),
                  pl.BlockSpec((CTX_LEN, SSD_GW), lambda b, j: (b, j)),
                  pl.BlockSpec((SEQ, SSD_GW), lambda b, j: (b, j)),
                  pl.BlockSpec((1, 1, SSD_LTOT, 2 * hp), lambda b, j: (b, j, 0, 0)),
                  pl.BlockSpec((1, 1, SSD_NCHUNK, 2 * hp, SSD_CHUNK), lambda b, j: (b, j, 0, 0, 0)),
                  pl.BlockSpec((1, 8, 2 * hp), lambda b, j: (j, 0, 0)),
                  pl.BlockSpec((1, 2 * hp, 128), lambda b, j: (j, 0, 0)),
                  pl.BlockSpec((1, 1, SSD_GW), lambda b, j: (j, 0, 0))],
        out_specs=(pl.BlockSpec((CTX_LEN, SSD_GW), lambda b, j: (b, j)),
                   pl.BlockSpec((SEQ, SSD_GW), lambda b, j: (b, j))),
        scratch_shapes=[pltpu.VMEM((SSD_LTOT, SSD_GW), F32),
                        pltpu.VMEM((2, SSD_STATE, SSD_GW), F32)],
        compiler_params=_cparams(("parallel", "parallel")),
        name="ssd_scan",
    )(xbc, xbc, xbc, p_ctx, p_lat, dt_col, dt_row, par_col, par_row, dskip)


def _sg_kernel(z_ref, u_ref, v_ref, g_ref, b_ref, ws_ref, bs_ref, o_ref):
    v = v_ref[...].astype(F32)
    mu = jnp.mean(v, axis=-1, keepdims=True)
    vc = v - mu
    var = jnp.mean(vc * vc, axis=-1, keepdims=True)
    vn = (vc * lax.rsqrt(var + EPS) * g_ref[...] + b_ref[...]).astype(BF16)
    for g in range(SG_GROUPS):
        cs = slice(g * SG_GW, (g + 1) * SG_GW)
        sv = jnp.dot(ws_ref[g], vn[:, cs], preferred_element_type=F32) + bs_ref[:, g:g + 1]
        o_ref[:, cs] = (u_ref[:, cs].astype(F32) * sv * _silu(z_ref[:, cs].astype(F32))).astype(o_ref.dtype)


def _sg_gate(p, ln_g, ln_b, w_s, b_s_t):
    rows = p.shape[0]
    blk = lambda j: pl.BlockSpec((SG_CHUNK, SG_HALF), lambda i: (i, j))
    return pl.pallas_call(
        _sg_kernel,
        out_shape=jax.ShapeDtypeStruct((rows, SG_HALF), BF16),
        grid=(rows // SG_CHUNK,),
        in_specs=[blk(0), blk(1), blk(2),
                  pl.BlockSpec((1, SG_HALF), lambda i: (0, 0)),
                  pl.BlockSpec((1, SG_HALF), lambda i: (0, 0)),
                  pl.BlockSpec((SG_GROUPS, SG_CHUNK, SG_CHUNK), lambda i: (0, 0, 0)),
                  pl.BlockSpec((SG_CHUNK, SG_GROUPS), lambda i: (0, 0))],
        out_specs=blk(0),
        compiler_params=_cparams(("parallel",)),
        name="sg_gate",
    )(p, p, p, ln_g.reshape(1, SG_HALF), ln_b.reshape(1, SG_HALF), w_s, b_s_t)


def _proj_tiles(n):
    return dict(tm=1024, tn=1024 if n % 1024 == 0 else 512, tk=D_MODEL)


def kernel(x, c, ctx, c_ctx, norm_g, ada_w, ada_b, na_w_in, na_rpb, na_w_out,
           ssd_w_in, ssd_conv_w, ssd_conv_b, ssd_dt_bias, ssd_a_log, ssd_d_skip,
           ssd_norm_g, ssd_w_out, sg_w_in, sg_ln_g, sg_ln_b, sg_w_s, sg_b_s, sg_w_out,
           final_norm_g):
    d = D_MODEL
    c_rows = jnp.concatenate([c, c_ctx[None], jnp.zeros((8 - BATCH - 1, d), F32)], axis=0)
    mods = _ada_all(c_rows, ada_w, ada_b)

    h_lat = x.reshape(BATCH * SEQ, d)
    h_ctx = ctx.reshape(BATCH * CTX_LEN, d)
    scale = NA_HEAD_DIM ** -0.5

    for i in range(DEPTH):
        kind, j = i % N_MIXERS, i // N_MIXERS
        need_ctx = i < DEPTH - 1
        mod_l = mods[i, :BATCH].reshape(BATCH, 1, 3 * d)
        mod_c = mods[i, BATCH:BATCH + 1].reshape(1, 1, 3 * d)
        sh_l, sc_l, g_l = mod_l[..., :d], mod_l[..., d:2 * d], mod_l[..., 2 * d:]
        sh_c, sc_c, g_c = mod_c[..., :d], mod_c[..., d:2 * d], mod_c[..., 2 * d:]
        n_lat = _rmsnorm(h_lat, norm_g[i], sc_l, sh_l, out_dtype=BF16)
        n_ctx = _rmsnorm(h_ctx, norm_g[i], sc_c, sh_c, out_dtype=BF16)

        if kind == 0:
            w = na_w_in[j]
            w_in = jnp.concatenate([w[:, :d] * scale, w[:, d:]], axis=1).astype(BF16)
            p_lat = _matmul(n_lat, w_in, out_dtype=BF16, **_proj_tiles(4 * d))
            bias_tab = _na_bias_table(na_rpb[j])
            if need_ctx:
                p_ctx = _matmul(n_ctx, w_in, out_dtype=BF16, **_proj_tiles(4 * d))
                y_lat = _na_latent(p_lat, p_ctx, bias_tab, NA_HEADS, 2 * NA_HEADS)
                y_ctx = _na_context(p_ctx)
            else:
                p_ctx = _matmul(n_ctx, w_in[:, d:3 * d], out_dtype=BF16, **_proj_tiles(2 * d))
                y_lat = _na_latent(p_lat, p_ctx, bias_tab, 0, NA_HEADS)
                y_ctx = None
            w_out = na_w_out[j].astype(BF16)
            out_tk = d
        elif kind == 1:
            w = ssd_w_in[j]
            n_main = SSD_D_INNER + SSD_CONV_CH
            w_main = w[:, :n_main].astype(BF16)
            w_dt = w[:, n_main:].astype(BF16)
            p_lat = _matmul(n_lat, w_main, out_dtype=BF16, **_proj_tiles(n_main))
            p_ctx = _matmul(n_ctx, w_main, out_dtype=BF16, **_proj_tiles(n_main))
            dt_lat = _matmul(n_lat, w_dt, out_dtype=F32, tm=1024, tn=2 * SSD_HEADS, tk=d)
            dt_ctx = _matmul(n_ctx, w_dt, out_dtype=F32, tm=1024, tn=2 * SSD_HEADS, tk=d)
            xbc = _ssd_conv(p_ctx, p_lat, ssd_conv_w[j], ssd_conv_b[j])
            yg_ctx, yg_lat = _ssd_scan(xbc, p_ctx, p_lat, dt_ctx, dt_lat,
                                       ssd_dt_bias[j], ssd_a_log[j], ssd_d_skip[j])
            y_lat = _rmsnorm(yg_lat, ssd_norm_g[j], out_dtype=BF16)
            y_ctx = _rmsnorm(yg_ctx, ssd_norm_g[j], out_dtype=BF16) if need_ctx else None
            w_out = ssd_w_out[j].astype(BF16)
            out_tk = d
        else:
            w_in = sg_w_in[j].astype(BF16)
            b_s_t = jnp.transpose(sg_b_s[j])
            w_s = sg_w_s[j].astype(BF16)
            p_lat = _matmul(n_lat, w_in, out_dtype=BF16, gelu_from_col=SG_HALF, **_proj_tiles(3 * SG_HALF))
            y_lat = _sg_gate(p_lat, sg_ln_g[j], sg_ln_b[j], w_s, b_s_t)
            y_ctx = None
            if need_ctx:
                p_ctx = _matmul(n_ctx, w_in, out_dtype=BF16, gelu_from_col=SG_HALF, **_proj_tiles(3 * SG_HALF))
                y_ctx = _sg_gate(p_ctx, sg_ln_g[j], sg_ln_b[j], w_s, b_s_t)
            w_out = sg_w_out[j].astype(BF16)
            out_tk = d

        h_lat = _matmul(y_lat, w_out, out_dtype=F32, tm=1024, tn=1024, tk=out_tk, resid=h_lat, gate=g_l)
        if need_ctx:
            h_ctx = _matmul(y_ctx, w_out, out_dtype=F32, tm=1024, tn=1024, tk=out_tk, resid=h_ctx, gate=g_c)

    out = _rmsnorm(h_lat, final_norm_g, out_dtype=F32)
    return out.reshape(BATCH, SEQ, d)
```

```python
import functools
import math

import numpy as np
import jax
import jax.numpy as jnp
from jax import lax
from jax.experimental import pallas as pl
from jax.experimental.pallas import tpu as pltpu

D_MODEL = 2048
BATCH = 4
SEQ = 2048
DEPTH = 4
GRID_W = 64
CTX_LEN = 256
N_MIXERS = 3
EPS = 1e-6
LTOT = CTX_LEN + SEQ
ROWS = BATCH * LTOT

NA_HEADS = 16
NA_HEAD_DIM = D_MODEL // NA_HEADS
NA_WIN_ROWS = 8
NA_WIN_COLS = 16
NA_ROWS = SEQ // GRID_W
NA_QROWS = 4
NA_KROWS = NA_QROWS + NA_WIN_ROWS
NA_TQ = NA_QROWS * GRID_W
NA_TK = NA_KROWS * GRID_W
NA_TILES = NA_ROWS // NA_QROWS
NA_NDR = 2 * NA_WIN_ROWS - 1

SSD_D_INNER = 2 * D_MODEL
SSD_HEAD_DIM = 64
SSD_HEADS = SSD_D_INNER // SSD_HEAD_DIM
SSD_GROUPS = 8
SSD_STATE = 128
SSD_CHUNK = 128
SSD_GN = SSD_GROUPS * SSD_STATE
SSD_CONV_CH = SSD_D_INNER + 2 * SSD_GN
SSD_MAIN = SSD_D_INNER + SSD_CONV_CH
SSD_HPG = SSD_HEADS // SSD_GROUPS
SSD_GW = SSD_HPG * SSD_HEAD_DIM
SSD_NCHUNK = LTOT // SSD_CHUNK
SSD_CCHUNK = CTX_LEN // SSD_CHUNK
SSD_DL = 2 * SSD_HPG

SG_HALF = 3 * D_MODEL
SG_GROUPS = 16
SG_CHUNK = 128
SG_GW = SG_HALF // SG_GROUPS

NEG = -1e30
VMEM_LIMIT = 56 * 1024 * 1024
MOD_ROWS = 8

PROJ_TM, PROJ_TN = 1024, 1024
OUT_TM, OUT_TK = 768, 512
ROW_CHUNK = 128

F32 = jnp.float32
BF16 = jnp.bfloat16
_NT = (((1,), (1,)), ((), ()))


def _cparams(sem):
    return pltpu.CompilerParams(dimension_semantics=sem, vmem_limit_bytes=VMEM_LIMIT)


def _silu(x):
    return x * (1.0 / (1.0 + jnp.exp(-x)))


def _gelu_tanh(x):
    c = math.sqrt(2.0 / math.pi)
    return 0.5 * x * (1.0 + jnp.tanh(c * (x + 0.044715 * (x * x * x))))


def _softplus(x):
    return jnp.maximum(x, 0.0) + jnp.log(1.0 + jnp.exp(-jnp.abs(x)))


def _ada_kernel(c_ref, w_ref, b_ref, o_ref):
    a = _silu(c_ref[...]).astype(BF16)
    o_ref[0] = jnp.dot(a, w_ref[0].astype(BF16), preferred_element_type=F32) + b_ref[0]


def _ada_all(c_rows, ada_w, ada_b):
    tn = 1024
    return pl.pallas_call(
        _ada_kernel,
        out_shape=jax.ShapeDtypeStruct((DEPTH, MOD_ROWS, 3 * D_MODEL), F32),
        grid=(DEPTH, 3 * D_MODEL // tn),
        in_specs=[pl.BlockSpec((MOD_ROWS, D_MODEL), lambda l, j: (0, 0)),
                  pl.BlockSpec((1, D_MODEL, tn), lambda l, j: (l, 0, j)),
                  pl.BlockSpec((1, 1, tn), lambda l, j: (l, 0, j))],
        out_specs=pl.BlockSpec((1, MOD_ROWS, tn), lambda l, j: (l, 0, j)),
        compiler_params=_cparams(("arbitrary", "arbitrary")),
        name="ada_mod",
    )(c_rows, ada_w, ada_b.reshape(DEPTH, 1, 3 * D_MODEL))


def _mod_rows(mod_ref, row):
    m = mod_ref[pl.ds(row, 1), :]
    return m[:, :D_MODEL], m[:, D_MODEL:2 * D_MODEL], m[:, 2 * D_MODEL:]


def _norm_mod_kernel(x_ref, g_ref, mod_ref, o_ref):
    i = pl.program_id(0)
    per = LTOT // CTX_LEN
    row = jnp.where(i % per == 0, BATCH, i // per)
    shift, scale, _ = _mod_rows(mod_ref, row)
    x = x_ref[...]
    y = x * lax.rsqrt(jnp.mean(x * x, axis=-1, keepdims=True) + EPS) * g_ref[...]
    o_ref[...] = (y * (1.0 + scale) + shift).astype(o_ref.dtype)


def _norm_mod(h, g, mod):
    return pl.pallas_call(
        _norm_mod_kernel,
        out_shape=jax.ShapeDtypeStruct((ROWS, D_MODEL), BF16),
        grid=(ROWS // CTX_LEN,),
        in_specs=[pl.BlockSpec((CTX_LEN, D_MODEL), lambda i: (i, 0)),
                  pl.BlockSpec((1, D_MODEL), lambda i: (0, 0)),
                  pl.BlockSpec((MOD_ROWS, 3 * D_MODEL), lambda i: (0, 0))],
        out_specs=pl.BlockSpec((CTX_LEN, D_MODEL), lambda i: (i, 0)),
        compiler_params=_cparams(("parallel",)),
        name="norm_mod",
    )(h, g.reshape(1, D_MODEL), mod)


def _rms_kernel(x_ref, g_ref, o_ref):
    x = x_ref[...].astype(F32)
    y = x * lax.rsqrt(jnp.mean(x * x, axis=-1, keepdims=True) + EPS) * g_ref[...]
    o_ref[...] = y.astype(o_ref.dtype)


def _rmsnorm(x, g, *, out_dtype, tr=512):
    rows, w = x.shape
    return pl.pallas_call(
        _rms_kernel,
        out_shape=jax.ShapeDtypeStruct((rows, w), out_dtype),
        grid=(rows // tr,),
        in_specs=[pl.BlockSpec((tr, w), lambda i: (i, 0)),
                  pl.BlockSpec((1, w), lambda i: (0, 0))],
        out_specs=pl.BlockSpec((tr, w), lambda i: (i, 0)),
        compiler_params=_cparams(("parallel",)),
        name="rmsnorm",
    )(x, g.reshape(1, w))


def _proj_kernel(a_ref, w_ref, o_ref, wb_ref, *, gelu_from, scale_tiles, scale):
    j = pl.program_id(0)
    i = pl.program_id(1)

    @pl.when(i == 0)
    def _():
        w = w_ref[...]
        if scale_tiles:
            w = w * jnp.where(j < scale_tiles, scale, 1.0)
        wb_ref[...] = w.astype(BF16)

    acc = jnp.dot(a_ref[...], wb_ref[...], preferred_element_type=F32)
    if gelu_from is None:
        o_ref[...] = acc.astype(o_ref.dtype)
    else:
        @pl.when(j >= gelu_from)
        def _():
            o_ref[...] = _gelu_tanh(acc).astype(o_ref.dtype)

        @pl.when(j < gelu_from)
        def _():
            o_ref[...] = acc.astype(o_ref.dtype)


def _proj(a, w, *, n, out_dtype, tn=PROJ_TN, col0=0, gelu_from_col=None, scale_cols=0, scale=1.0):
    m, kdim = a.shape
    off = col0 // tn
    return pl.pallas_call(
        functools.partial(_proj_kernel,
                          gelu_from=None if gelu_from_col is None else gelu_from_col // tn,
                          scale_tiles=scale_cols // tn, scale=scale),
        out_shape=jax.ShapeDtypeStruct((m, n), out_dtype),
        grid=(n // tn, m // PROJ_TM),
        in_specs=[pl.BlockSpec((PROJ_TM, kdim), lambda j, i: (i, 0)),
                  pl.BlockSpec((kdim, tn), lambda j, i: (0, off + j))],
        out_specs=pl.BlockSpec((PROJ_TM, tn), lambda j, i: (i, j)),
        scratch_shapes=[pltpu.VMEM((kdim, tn), BF16)],
        compiler_params=_cparams(("parallel", "arbitrary")),
        name="proj",
    )(a, w)


def _outproj_kernel(a_ref, w_ref, h_ref, mod_ref, ng_ref, nmod_ref, *rest, nk, final):
    if final:
        n_ref, acc_ref = rest
        hn_ref = None
    else:
        hn_ref, n_ref, acc_ref = rest
    i = pl.program_id(0)
    k = pl.program_id(1)
    part = jnp.dot(a_ref[...], w_ref[...], preferred_element_type=F32)

    @pl.when(k == 0)
    def _():
        acc_ref[...] = part

    @pl.when(k > 0)
    def _():
        acc_ref[...] += part

    @pl.when(k == nk - 1)
    def _():
        per = LTOT // OUT_TM
        b = i // per
        first = i % per == 0
        for r in range(OUT_TM // ROW_CHUNK):
            row = jnp.where(jnp.logical_and(first, r < CTX_LEN // ROW_CHUNK), BATCH, b)
            rs = slice(r * ROW_CHUNK, (r + 1) * ROW_CHUNK)
            _, _, gate = _mod_rows(mod_ref, row)
            hn = h_ref[rs, :] + gate * acc_ref[rs, :]
            y = hn * lax.rsqrt(jnp.mean(hn * hn, axis=-1, keepdims=True) + EPS) * ng_ref[...]
            if final:
                n_ref[rs, :] = y
            else:
                shift, scale, _ = _mod_rows(nmod_ref, row)
                hn_ref[rs, :] = hn
                n_ref[rs, :] = (y * (1.0 + scale) + shift).astype(n_ref.dtype)


def _outproj(a, w, h, mod, next_g, next_mod, *, final):
    m, kdim = a.shape
    nk = kdim // OUT_TK
    full = lambda i, k: (i, 0)
    const = lambda i, k: (0, 0)
    row_spec = pl.BlockSpec((OUT_TM, D_MODEL), full)
    if final:
        out_shape = jax.ShapeDtypeStruct((m, D_MODEL), F32)
        out_specs = row_spec
    else:
        out_shape = (jax.ShapeDtypeStruct((m, D_MODEL), F32), jax.ShapeDtypeStruct((m, D_MODEL), BF16))
        out_specs = (row_spec, row_spec)
    return pl.pallas_call(
        functools.partial(_outproj_kernel, nk=nk, final=final),
        out_shape=out_shape,
        grid=(m // OUT_TM, nk),
        in_specs=[pl.BlockSpec((OUT_TM, OUT_TK), lambda i, k: (i, k)),
                  pl.BlockSpec((OUT_TK, D_MODEL), lambda i, k: (k, 0)),
                  row_spec,
                  pl.BlockSpec((MOD_ROWS, 3 * D_MODEL), const),
                  pl.BlockSpec((1, D_MODEL), const),
                  pl.BlockSpec((MOD_ROWS, 3 * D_MODEL), const)],
        out_specs=out_specs,
        scratch_shapes=[pltpu.VMEM((OUT_TM, D_MODEL), F32)],
        compiler_params=_cparams(("parallel", "arbitrary")),
        name="outproj",
    )(a, w, h, mod, next_g.reshape(1, D_MODEL), next_mod)


def _na_tile_patterns():
    pats = []
    for t in (0, 1, NA_TILES - 1):
        ws = min(max(NA_QROWS * t - NA_WIN_ROWS // 2, 0), NA_ROWS - NA_KROWS)
        pat = np.full((NA_QROWS, NA_KROWS), NA_NDR, dtype=np.int64)
        for a in range(NA_QROWS):
            r = NA_QROWS * t + a
            rs = min(max(r - NA_WIN_ROWS // 2, 0), NA_ROWS - NA_WIN_ROWS)
            for jj in range(NA_KROWS):
                kabs = ws + jj
                if rs <= kabs < rs + NA_WIN_ROWS:
                    pat[a, jj] = kabs - r + NA_WIN_ROWS - 1
        pats.append(pat)
    return pats


def _na_bias_slabs(rpb):
    qcol = np.arange(GRID_W)
    c_start = np.clip(qcol - NA_WIN_COLS // 2, 0, GRID_W - NA_WIN_COLS)
    col_ok = (qcol[None, :] >= c_start[:, None]) & (qcol[None, :] < c_start[:, None] + NA_WIN_COLS)
    dc = np.clip(qcol[None, :] - qcol[:, None], -(NA_WIN_COLS - 1), NA_WIN_COLS - 1) + NA_WIN_COLS - 1
    by_col = jnp.where(col_ok[None, None], rpb[:, :, dc], NEG)
    by_col = jnp.concatenate([by_col, jnp.full((NA_HEADS, 1, GRID_W, GRID_W), NEG, F32)], axis=1)
    return jnp.concatenate([by_col, by_col], axis=-1).astype(F32)


def _na_kernel(q_ref, k_ref, v_ref, z_ref, slab_ref, o_ref, bias_ref, *, need_ctx):
    @pl.when(pl.program_id(1) == 0)
    def _():
        left = lax.broadcasted_iota(jnp.int32, (GRID_W, 2 * GRID_W), 1) < GRID_W
        for p, pat in enumerate(_na_tile_patterns()):
            for a in range(NA_QROWS):
                for jp in range(NA_KROWS // 2):
                    blk = jnp.where(left, slab_ref[0, int(pat[a, 2 * jp])], slab_ref[0, int(pat[a, 2 * jp + 1])])
                    bias_ref[p, a * GRID_W:(a + 1) * GRID_W, jp * 2 * GRID_W:(jp + 1) * 2 * GRID_W] = blk

    kc = k_ref[0, 0:CTX_LEN, :]
    vc = v_ref[0, 0:CTX_LEN, :]

    def finish(o, l, r0):
        z = z_ref[0, pl.ds(r0, NA_TQ), :].astype(F32)
        o_ref[0, pl.ds(r0, NA_TQ), :] = (o / l * _silu(z)).astype(o_ref.dtype)

    if need_ctx:
        s = lax.dot_general(q_ref[0, 0:CTX_LEN, :], kc, _NT, preferred_element_type=F32)
        p = jnp.exp(s - jnp.max(s, axis=-1, keepdims=True))
        finish(jnp.dot(p.astype(BF16), vc, preferred_element_type=F32), jnp.sum(p, axis=-1, keepdims=True), 0)
    else:
        o_ref[0, 0:CTX_LEN, :] = jnp.zeros((CTX_LEN, NA_HEAD_DIM), o_ref.dtype)

    def tile(t, carry):
        r0 = pl.multiple_of(CTX_LEN + t * NA_TQ, NA_TQ)
        ws = jnp.clip(NA_QROWS * t - NA_WIN_ROWS // 2, 0, NA_ROWS - NA_KROWS)
        ks = pl.multiple_of(CTX_LEN + ws * GRID_W, GRID_W)
        pat = jnp.where(t == 0, 0, jnp.where(t == NA_TILES - 1, 2, 1))
        q = q_ref[0, pl.ds(r0, NA_TQ), :]
        kw = k_ref[0, pl.ds(ks, NA_TK), :]
        vw = v_ref[0, pl.ds(ks, NA_TK), :]
        s1 = lax.dot_general(q, kw, _NT, preferred_element_type=F32) + bias_ref[pat]
        s2 = lax.dot_general(q, kc, _NT, preferred_element_type=F32)
        m = jnp.maximum(jnp.max(s1, axis=-1, keepdims=True), jnp.max(s2, axis=-1, keepdims=True))
        p1 = jnp.exp(s1 - m)
        p2 = jnp.exp(s2 - m)
        l = jnp.sum(p1, axis=-1, keepdims=True) + jnp.sum(p2, axis=-1, keepdims=True)
        o = (jnp.dot(p1.astype(BF16), vw, preferred_element_type=F32)
             + jnp.dot(p2.astype(BF16), vc, preferred_element_type=F32))
        finish(o, l, r0)
        return carry

    lax.fori_loop(0, NA_TILES, tile, 0, unroll=2)


def _na_attention(p, slabs, need_ctx):
    hd = NA_HEAD_DIM
    blk = lambda off: pl.BlockSpec((1, LTOT, hd), lambda h, b: (b, 0, off + h))
    return pl.pallas_call(
        functools.partial(_na_kernel, need_ctx=need_ctx),
        out_shape=jax.ShapeDtypeStruct((BATCH, LTOT, D_MODEL), BF16),
        grid=(NA_HEADS, BATCH),
        in_specs=[blk(0), blk(NA_HEADS), blk(2 * NA_HEADS), blk(3 * NA_HEADS),
                  pl.BlockSpec((1, NA_NDR + 1, GRID_W, 2 * GRID_W), lambda h, b: (h, 0, 0, 0))],
        out_specs=blk(0),
        scratch_shapes=[pltpu.VMEM((3, NA_TQ, NA_TK), F32)],
        compiler_params=_cparams(("parallel", "arbitrary")),
        name="na_attention",
    )(p, p, p, p, slabs)


def _conv_kernel(u_ref, w_ref, b_ref, o_ref):
    u = u_ref[0].astype(F32)
    row = lax.broadcasted_iota(jnp.int32, u.shape, 0)
    seg_first = jnp.logical_or(row == 0, row == CTX_LEN)
    seg_last = jnp.logical_or(row == CTX_LEN - 1, row == LTOT - 1)
    up = jnp.where(seg_first, 0.0, pltpu.roll(u, 1, 0))
    un = jnp.where(seg_last, 0.0, pltpu.roll(u, LTOT - 1, 0))
    w = w_ref[...]
    y = w[0:1] * up + w[1:2] * u + w[2:3] * un + b_ref[...]
    o_ref[0] = _silu(y).astype(o_ref.dtype)


def _ssd_conv(p, conv_w, conv_b):
    tc = 256
    off = SSD_D_INNER // tc
    return pl.pallas_call(
        _conv_kernel,
        out_shape=jax.ShapeDtypeStruct((BATCH, LTOT, SSD_CONV_CH), BF16),
        grid=(BATCH, SSD_CONV_CH // tc),
        in_specs=[pl.BlockSpec((1, LTOT, tc), lambda b, j: (b, 0, off + j)),
                  pl.BlockSpec((3, tc), lambda b, j: (0, j)),
                  pl.BlockSpec((1, tc), lambda b, j: (0, j))],
        out_specs=pl.BlockSpec((1, LTOT, tc), lambda b, j: (b, 0, j)),
        compiler_params=_cparams(("parallel", "parallel")),
        name="ssd_conv",
    )(p, conv_w, conv_b.reshape(1, SSD_CONV_CH))


def _split3(v):
    hi = v.astype(BF16)
    r1 = v - hi.astype(F32)
    mid = r1.astype(BF16)
    lo = (r1 - mid.astype(F32)).astype(BF16)
    lane = lax.broadcasted_iota(jnp.int32, v.shape, 1)
    return jnp.where(lane < SSD_DL, hi, jnp.where(lane < 2 * SSD_DL, mid, lo))


def _ssd_kernel(x_ref, b_ref, c_ref, z_ref, dtc_ref, dtr_ref, pc_ref, pr_ref, dskip_ref,
                y_ref, yacc_ref, s_ref):
    q = SSD_CHUNK
    hp = SSD_HPG
    li = lax.broadcasted_iota(jnp.int32, (q, q), 0)
    si = lax.broadcasted_iota(jnp.int32, (q, q), 1)
    lower = li >= si
    upper = li <= si
    tri_lo = lower.astype(F32)
    tri_up = upper.astype(F32)
    left = si < SSD_HEAD_DIM

    def expand_matrix(d):
        r = lax.broadcasted_iota(jnp.int32, (3 * SSD_DL, SSD_GW), 0) % SSD_DL
        c = lax.broadcasted_iota(jnp.int32, (3 * SSD_DL, SSD_GW), 1) // SSD_HEAD_DIM
        return (r == d * hp + c).astype(BF16)

    expand = (expand_matrix(0), expand_matrix(1))

    bias_c = pc_ref[0, 0:1, :]
    a_c = -jnp.exp(pc_ref[0, 1:2, :])
    bias_r = pr_ref[0, :, 0:1]
    a_r = -jnp.exp(pr_ref[0, :, 1:2])

    yacc_ref[...] = x_ref[0].astype(F32) * dskip_ref[0]
    s_ref[...] = jnp.zeros_like(s_ref)

    def chunk(c, d):
        r0 = pl.multiple_of(c * q, q)
        tri = tri_lo if d == 0 else tri_up
        tri_t = tri_up if d == 0 else tri_lo
        mask = lower if d == 0 else upper
        ex = expand[d]
        xf = x_ref[0, pl.ds(r0, q), :].astype(F32)
        bm = b_ref[0, pl.ds(r0, q), :]
        cm = c_ref[0, pl.ds(r0, q), :]
        dtc = _softplus(dtc_ref[0, 0, pl.ds(r0, q), :] + bias_c)
        acum_c = jnp.dot(tri, dtc * a_c, preferred_element_type=F32,
                         precision=lax.Precision.HIGHEST)
        dtr = _softplus(dtr_ref[0, 0, c] + bias_r)
        acum_r = jnp.dot(dtr * a_r, tri_t, preferred_element_type=F32,
                         precision=lax.Precision.HIGHEST)
        tot_c = acum_c[q - 1:q, :] if d == 0 else acum_c[0:1, :]
        ea_c = jnp.exp(acum_c)
        dw_c = dtc * jnp.exp(tot_c - acum_c)
        dt_x = jnp.dot(_split3(dtc), ex, preferred_element_type=F32)
        ea_x = jnp.dot(_split3(ea_c), ex, preferred_element_type=F32)
        dw_x = jnp.dot(_split3(dw_c), ex, preferred_element_type=F32)
        etot_x = ea_x[q - 1:q, :] if d == 0 else ea_x[0:1, :]
        cb = lax.dot_general(cm, bm, _NT, preferred_element_type=F32)
        bt = bm.astype(F32).T.astype(BF16)
        sprev = s_ref[d]
        yoff = jnp.dot(cm, sprev.astype(BF16), preferred_element_type=F32)
        xd_b = (xf * dt_x).astype(BF16)
        s_ref[d] = sprev * etot_x + jnp.dot(bt, (xf * dw_x).astype(BF16), preferred_element_type=F32)
        zero = jnp.zeros((q, q), BF16)
        for p in range(hp // 2):
            cs = slice(p * q, (p + 1) * q)
            m_pair = []
            for ln in (d * hp + 2 * p, d * hp + 2 * p + 1):
                diff = acum_c[:, ln:ln + 1] - acum_r[ln:ln + 1, :]
                m_pair.append((cb * jnp.exp(jnp.where(mask, diff, NEG))).astype(BF16))
            xd_p = xd_b[:, cs]
            rhs = jnp.concatenate([jnp.where(left, xd_p, zero), jnp.where(left, zero, xd_p)], axis=0)
            yd = jnp.dot(jnp.concatenate(m_pair, axis=1), rhs, preferred_element_type=F32)
            yacc_ref[pl.ds(r0, q), cs] += yd + ea_x[:, cs] * yoff[:, cs]

    def step(k, carry):
        chunk(k, 0)
        chunk(jnp.where(k < SSD_CCHUNK, SSD_CCHUNK - 1 - k, SSD_NCHUNK - 1 + SSD_CCHUNK - k), 1)
        return carry

    lax.fori_loop(0, SSD_NCHUNK, step, 0)

    y_ref[0] = (yacc_ref[...] * _silu(z_ref[0].astype(F32))).astype(y_ref.dtype)


def _ssd_scan(xbc, p, dt_raw, dt_bias, a_log, d_skip):
    g, hp, dl = SSD_GROUPS, SSD_HPG, SSD_DL
    dt = dt_raw.reshape(BATCH, LTOT, 2, g, hp).transpose(0, 3, 1, 2, 4).reshape(BATCH, g, LTOT, dl)
    dt_col = jnp.tile(dt, (1, 1, 1, 3))
    dt_row = dt.reshape(BATCH, g, SSD_NCHUNK, SSD_CHUNK, dl).transpose(0, 1, 2, 4, 3)
    par = jnp.stack([dt_bias, a_log]).reshape(2, 2, g, hp).transpose(2, 0, 1, 3).reshape(g, 2, dl)
    par_col = jnp.pad(jnp.tile(par, (1, 1, 3)), ((0, 0), (0, 6), (0, 0)))
    par_row = jnp.pad(par.transpose(0, 2, 1), ((0, 0), (0, 0), (0, 126)))
    dskip = jnp.repeat(d_skip, SSD_HEAD_DIM).reshape(g, 1, SSD_GW)
    xoff = SSD_D_INNER // SSD_STATE
    return pl.pallas_call(
        _ssd_kernel,
        out_shape=jax.ShapeDtypeStruct((BATCH, LTOT, SSD_D_INNER), BF16),
        grid=(BATCH, g),
        in_specs=[pl.BlockSpec((1, LTOT, SSD_GW), lambda b, j: (b, 0, j)),
                  pl.BlockSpec((1, LTOT, SSD_STATE), lambda b, j: (b, 0, xoff + j)),
                  pl.BlockSpec((1, LTOT, SSD_STATE), lambda b, j: (b, 0, xoff + g + j)),
                  pl.BlockSpec((1, LTOT, SSD_GW), lambda b, j: (b, 0, j)),
                  pl.BlockSpec((1, 1, LTOT, 3 * dl), lambda b, j: (b, j, 0, 0)),
                  pl.BlockSpec((1, 1, SSD_NCHUNK, dl, SSD_CHUNK), lambda b, j: (b, j, 0, 0, 0)),
                  pl.BlockSpec((1, 8, 3 * dl), lambda b, j: (j, 0, 0)),
                  pl.BlockSpec((1, dl, 128), lambda b, j: (j, 0, 0)),
                  pl.BlockSpec((1, 1, SSD_GW), lambda b, j: (j, 0, 0))],
        out_specs=pl.BlockSpec((1, LTOT, SSD_GW), lambda b, j: (b, 0, j)),
        scratch_shapes=[pltpu.VMEM((LTOT, SSD_GW), F32),
                        pltpu.VMEM((2, SSD_STATE, SSD_GW), F32)],
        compiler_params=_cparams(("parallel", "parallel")),
        name="ssd_scan",
    )(xbc, xbc, xbc, p, dt_col, dt_row, par_col, par_row, dskip)


def _sg_kernel(z_ref, u_ref, v_ref, g_ref, b_ref, ws_ref, bs_ref, o_ref):
    v = v_ref[...].astype(F32)
    mu = jnp.mean(v, axis=-1, keepdims=True)
    vc = v - mu
    var = jnp.mean(vc * vc, axis=-1, keepdims=True)
    vn = (vc * lax.rsqrt(var + EPS) * g_ref[...] + b_ref[...]).astype(BF16)
    for g in range(SG_GROUPS):
        cs = slice(g * SG_GW, (g + 1) * SG_GW)
        sv = jnp.dot(ws_ref[g], vn[:, cs], preferred_element_type=F32) + bs_ref[:, g:g + 1]
        o_ref[:, cs] = (u_ref[:, cs].astype(F32) * sv * _silu(z_ref[:, cs].astype(F32))).astype(o_ref.dtype)


def _sg_gate(p, ln_g, ln_b, w_s, b_s_t):
    rows = p.shape[0]
    blk = lambda j: pl.BlockSpec((SG_CHUNK, SG_HALF), lambda i: (i, j))
    return pl.pallas_call(
        _sg_kernel,
        out_shape=jax.ShapeDtypeStruct((rows, SG_HALF), BF16),
        grid=(rows // SG_CHUNK,),
        in_specs=[blk(0), blk(1), blk(2),
                  pl.BlockSpec((1, SG_HALF), lambda i: (0, 0)),
                  pl.BlockSpec((1, SG_HALF), lambda i: (0, 0)),
                  pl.BlockSpec((SG_GROUPS, SG_CHUNK, SG_CHUNK), lambda i: (0, 0, 0)),
                  pl.BlockSpec((SG_CHUNK, SG_GROUPS), lambda i: (0, 0))],
        out_specs=blk(0),
        compiler_params=_cparams(("parallel",)),
        name="sg_gate",
    )(p, p, p, ln_g.reshape(1, SG_HALF), ln_b.reshape(1, SG_HALF), w_s, b_s_t)


def kernel(x, c, ctx, c_ctx, norm_g, ada_w, ada_b, na_w_in, na_rpb, na_w_out,
           ssd_w_in, ssd_conv_w, ssd_conv_b, ssd_dt_bias, ssd_a_log, ssd_d_skip,
           ssd_norm_g, ssd_w_out, sg_w_in, sg_ln_g, sg_ln_b, sg_w_s, sg_b_s, sg_w_out,
           final_norm_g):
    d = D_MODEL
    c_rows = jnp.concatenate([c, c_ctx[None], jnp.zeros((MOD_ROWS - BATCH - 1, d), F32)], axis=0)
    mods = _ada_all(c_rows, ada_w, ada_b)

    h = jnp.concatenate([ctx, x], axis=1).reshape(ROWS, d)
    n = _norm_mod(h, norm_g[0], mods[0])

    for i in range(DEPTH):
        kind, j = i % N_MIXERS, i // N_MIXERS
        need_ctx = i < DEPTH - 1
        if kind == 0:
            p = _proj(n, na_w_in[j], n=4 * d, out_dtype=BF16, scale_cols=d, scale=NA_HEAD_DIM ** -0.5)
            y = _na_attention(p.reshape(BATCH, LTOT, 4 * d), _na_bias_slabs(na_rpb[j]), need_ctx)
            y = y.reshape(ROWS, d)
            w_out = na_w_out[j].astype(BF16)
        elif kind == 1:
            p = _proj(n, ssd_w_in[j], n=SSD_MAIN, out_dtype=BF16).reshape(BATCH, LTOT, SSD_MAIN)
            dt_raw = _proj(n, ssd_w_in[j], n=2 * SSD_HEADS, out_dtype=F32, tn=2 * SSD_HEADS, col0=SSD_MAIN)
            xbc = _ssd_conv(p, ssd_conv_w[j], ssd_conv_b[j])
            yg = _ssd_scan(xbc, p, dt_raw, ssd_dt_bias[j], ssd_a_log[j], ssd_d_skip[j])
            y = _rmsnorm(yg.reshape(ROWS, SSD_D_INNER), ssd_norm_g[j], out_dtype=BF16)
            w_out = ssd_w_out[j].astype(BF16)
        else:
            p = _proj(n, sg_w_in[j], n=3 * SG_HALF, out_dtype=BF16, gelu_from_col=SG_HALF)
            y = _sg_gate(p, sg_ln_g[j], sg_ln_b[j], sg_w_s[j].astype(BF16), jnp.transpose(sg_b_s[j]))
            w_out = sg_w_out[j].astype(BF16)

        if need_ctx:
            h, n = _outproj(y, w_out, h, mods[i], norm_g[i + 1], mods[i + 1], final=False)
        else:
            out = _outproj(y, w_out, h, mods[i], final_norm_g, mods[i], final=True)

    return out.reshape(BATCH, LTOT, d)[:, CTX_LEN:]
```

```python
import functools
import math

import numpy as np
import jax
import jax.numpy as jnp
from jax import lax
from jax.experimental import pallas as pl
from jax.experimental.pallas import tpu as pltpu

D_MODEL = 2048
BATCH = 4
SEQ = 2048
DEPTH = 4
GRID_W = 64
CTX_LEN = 256
N_MIXERS = 3
EPS = 1e-6
LTOT = CTX_LEN + SEQ
ROWS = BATCH * LTOT

NA_HEADS = 16
NA_HEAD_DIM = D_MODEL // NA_HEADS
NA_WIN_ROWS = 8
NA_WIN_COLS = 16
NA_ROWS = SEQ // GRID_W
NA_QROWS = 4
NA_KROWS = NA_QROWS + NA_WIN_ROWS
NA_TQ = NA_QROWS * GRID_W
NA_TK = NA_KROWS * GRID_W
NA_TILES = NA_ROWS // NA_QROWS
NA_NDR = 2 * NA_WIN_ROWS - 1

SSD_D_INNER = 2 * D_MODEL
SSD_HEAD_DIM = 64
SSD_HEADS = SSD_D_INNER // SSD_HEAD_DIM
SSD_GROUPS = 8
SSD_STATE = 128
SSD_CHUNK = 128
SSD_GN = SSD_GROUPS * SSD_STATE
SSD_CONV_CH = SSD_D_INNER + 2 * SSD_GN
SSD_MAIN = SSD_D_INNER + SSD_CONV_CH
SSD_HPG = SSD_HEADS // SSD_GROUPS
SSD_GW = SSD_HPG * SSD_HEAD_DIM
SSD_NCHUNK = LTOT // SSD_CHUNK
SSD_CCHUNK = CTX_LEN // SSD_CHUNK
SSD_DL = 2 * SSD_HPG

SG_HALF = 3 * D_MODEL
SG_GROUPS = 16
SG_CHUNK = 128
SG_GW = SG_HALF // SG_GROUPS

NEG = -1e30
VMEM_LIMIT = 56 * 1024 * 1024
MOD_ROWS = 8

PROJ_TM, PROJ_TN = 1024, 1024
OUT_TM = 768
LOG2E = math.log2(math.e)

F32 = jnp.float32
BF16 = jnp.bfloat16
_NT = (((1,), (1,)), ((), ()))


def _cparams(sem):
    return pltpu.CompilerParams(dimension_semantics=sem, vmem_limit_bytes=VMEM_LIMIT)


def _silu(x):
    return x * (1.0 / (1.0 + jnp.exp(-x)))


def _gelu_tanh(x):
    c = math.sqrt(2.0 / math.pi)
    return 0.5 * x * (1.0 + jnp.tanh(c * (x + 0.044715 * (x * x * x))))


def _softplus(x):
    return jnp.maximum(x, 0.0) + jnp.log(1.0 + jnp.exp(-jnp.abs(x)))


def _ada_kernel(c_ref, w_ref, b_ref, o_ref):
    a = _silu(c_ref[...]).astype(BF16)
    o_ref[0] = jnp.dot(a, w_ref[0].astype(BF16), preferred_element_type=F32) + b_ref[0]


def _ada_all(c_rows, ada_w, ada_b):
    tn = 1024
    return pl.pallas_call(
        _ada_kernel,
        out_shape=jax.ShapeDtypeStruct((DEPTH, MOD_ROWS, 3 * D_MODEL), F32),
        grid=(DEPTH, 3 * D_MODEL // tn),
        in_specs=[pl.BlockSpec((MOD_ROWS, D_MODEL), lambda l, j: (0, 0)),
                  pl.BlockSpec((1, D_MODEL, tn), lambda l, j: (l, 0, j)),
                  pl.BlockSpec((1, 1, tn), lambda l, j: (l, 0, j))],
        out_specs=pl.BlockSpec((1, MOD_ROWS, tn), lambda l, j: (l, 0, j)),
        compiler_params=_cparams(("arbitrary", "arbitrary")),
        name="ada_mod",
    )(c_rows, ada_w, ada_b.reshape(DEPTH, 1, 3 * D_MODEL))


def _mod_rows(mod_ref, row):
    m = mod_ref[pl.ds(row, 1), :]
    return m[:, :D_MODEL], m[:, D_MODEL:2 * D_MODEL], m[:, 2 * D_MODEL:]


def _norm_mod_kernel(x_ref, g_ref, mod_ref, o_ref):
    i = pl.program_id(0)
    per = LTOT // CTX_LEN
    row = jnp.where(i % per == 0, BATCH, i // per)
    shift, scale, _ = _mod_rows(mod_ref, row)
    x = x_ref[...]
    y = x * lax.rsqrt(jnp.mean(x * x, axis=-1, keepdims=True) + EPS) * g_ref[...]
    o_ref[...] = (y * (1.0 + scale) + shift).astype(o_ref.dtype)


def _norm_mod(h, g, mod):
    return pl.pallas_call(
        _norm_mod_kernel,
        out_shape=jax.ShapeDtypeStruct((ROWS, D_MODEL), BF16),
        grid=(ROWS // CTX_LEN,),
        in_specs=[pl.BlockSpec((CTX_LEN, D_MODEL), lambda i: (i, 0)),
                  pl.BlockSpec((1, D_MODEL), lambda i: (0, 0)),
                  pl.BlockSpec((MOD_ROWS, 3 * D_MODEL), lambda i: (0, 0))],
        out_specs=pl.BlockSpec((CTX_LEN, D_MODEL), lambda i: (i, 0)),
        compiler_params=_cparams(("parallel",)),
        name="norm_mod",
    )(h, g.reshape(1, D_MODEL), mod)


def _rms_kernel(x_ref, g_ref, o_ref):
    x = x_ref[...].astype(F32)
    y = x * lax.rsqrt(jnp.mean(x * x, axis=-1, keepdims=True) + EPS) * g_ref[...]
    o_ref[...] = y.astype(o_ref.dtype)


def _rmsnorm(x, g, *, out_dtype, tr=512):
    rows, w = x.shape
    return pl.pallas_call(
        _rms_kernel,
        out_shape=jax.ShapeDtypeStruct((rows, w), out_dtype),
        grid=(rows // tr,),
        in_specs=[pl.BlockSpec((tr, w), lambda i: (i, 0)),
                  pl.BlockSpec((1, w), lambda i: (0, 0))],
        out_specs=pl.BlockSpec((tr, w), lambda i: (i, 0)),
        compiler_params=_cparams(("parallel",)),
        name="rmsnorm",
    )(x, g.reshape(1, w))


def _proj_kernel(a_ref, w_ref, o_ref, wb_ref, *, gelu_from, scale_tiles, scale):
    j = pl.program_id(0)
    i = pl.program_id(1)

    @pl.when(i == 0)
    def _():
        w = w_ref[0]
        if scale_tiles:
            w = w * jnp.where(j < scale_tiles, scale, 1.0)
        wb_ref[...] = w.astype(BF16)

    acc = jnp.dot(a_ref[...], wb_ref[...], preferred_element_type=F32)
    if gelu_from is None:
        o_ref[...] = acc.astype(o_ref.dtype)
    else:
        @pl.when(j >= gelu_from)
        def _():
            o_ref[...] = _gelu_tanh(acc).astype(o_ref.dtype)

        @pl.when(j < gelu_from)
        def _():
            o_ref[...] = acc.astype(o_ref.dtype)


def _proj(a, w, layer, *, n, out_dtype, tn=PROJ_TN, col0=0, gelu_from_col=None, scale_cols=0, scale=1.0):
    m, kdim = a.shape
    off = col0 // tn
    return pl.pallas_call(
        functools.partial(_proj_kernel,
                          gelu_from=None if gelu_from_col is None else gelu_from_col // tn,
                          scale_tiles=scale_cols // tn, scale=scale),
        out_shape=jax.ShapeDtypeStruct((m, n), out_dtype),
        grid=(n // tn, m // PROJ_TM),
        in_specs=[pl.BlockSpec((PROJ_TM, kdim), lambda j, i: (i, 0)),
                  pl.BlockSpec((1, kdim, tn), lambda j, i: (layer, 0, off + j))],
        out_specs=pl.BlockSpec((PROJ_TM, tn), lambda j, i: (i, j)),
        scratch_shapes=[pltpu.VMEM((kdim, tn), BF16)],
        compiler_params=_cparams(("parallel", "arbitrary")),
        name="proj",
    )(a, w)


def _outproj_kernel(a_ref, w_ref, h_ref, gate_ref, ng_ref, nmod_ref, *rest, nj, tn, final):
    if final:
        n_ref, hs_ref, ss_ref = rest
        hn_ref = None
    else:
        hn_ref, n_ref, hs_ref, ss_ref = rest
    i = pl.program_id(0)
    j = pl.program_id(1)
    per = LTOT // OUT_TM
    b = i // per
    first = i % per == 0
    top = slice(0, CTX_LEN)
    bot = slice(CTX_LEN, OUT_TM)

    def rows(ref, cols):
        lat = ref[pl.ds(b, 1), cols]
        return jnp.where(first, ref[BATCH:BATCH + 1, cols], lat), lat

    acc = jnp.dot(a_ref[...], w_ref[0], preferred_element_type=F32)
    g_top, g_bot = rows(gate_ref.at[0], slice(None))
    hn_t = h_ref[top, :] + g_top * acc[top, :]
    hn_b = h_ref[bot, :] + g_bot * acc[bot, :]
    hs_ref[j, top, :] = hn_t
    hs_ref[j, bot, :] = hn_b
    if not final:
        hn_ref[top, :] = hn_t
        hn_ref[bot, :] = hn_b
    sq_t = jnp.sum(hn_t * hn_t, axis=-1, keepdims=True)
    sq_b = jnp.sum(hn_b * hn_b, axis=-1, keepdims=True)

    @pl.when(j == 0)
    def _():
        ss_ref[top, :] = sq_t
        ss_ref[bot, :] = sq_b

    @pl.when(j > 0)
    def _():
        ss_ref[top, :] += sq_t
        ss_ref[bot, :] += sq_b

    @pl.when(j == nj - 1)
    def _():
        inv = lax.rsqrt(ss_ref[...] * (1.0 / D_MODEL) + EPS)
        for jj in range(nj):
            cs = slice(jj * tn, (jj + 1) * tn)
            y = hs_ref[jj] * inv * ng_ref[:, cs]
            if final:
                n_ref[:, cs] = y
            else:
                sh_top, sh_bot = rows(nmod_ref, cs)
                sc_top, sc_bot = rows(nmod_ref, slice(D_MODEL + jj * tn, D_MODEL + (jj + 1) * tn))
                n_ref[top, cs] = (y[top, :] * (1.0 + sc_top) + sh_top).astype(n_ref.dtype)
                n_ref[bot, cs] = (y[bot, :] * (1.0 + sc_bot) + sh_bot).astype(n_ref.dtype)


def _outproj(a, w, layer, h, mods, mod_layer, next_g, next_mod, *, final):
    m, kdim = a.shape
    tn = 512 if kdim <= 2 * D_MODEL else 256
    nj = D_MODEL // tn
    gate_off = 2 * D_MODEL // tn
    tile = pl.BlockSpec((OUT_TM, tn), lambda i, j: (i, j))
    full = pl.BlockSpec((OUT_TM, D_MODEL), lambda i, j: (i, 0))
    const = lambda i, j: (0, 0)
    if final:
        out_shape = jax.ShapeDtypeStruct((m, D_MODEL), F32)
        out_specs = full
    else:
        out_shape = (jax.ShapeDtypeStruct((m, D_MODEL), F32), jax.ShapeDtypeStruct((m, D_MODEL), BF16))
        out_specs = (tile, full)
    return pl.pallas_call(
        functools.partial(_outproj_kernel, nj=nj, tn=tn, final=final),
        out_shape=out_shape,
        grid=(m // OUT_TM, nj),
        in_specs=[pl.BlockSpec((OUT_TM, kdim), lambda i, j: (i, 0)),
                  pl.BlockSpec((1, kdim, tn), lambda i, j: (layer, 0, j)),
                  tile,
                  pl.BlockSpec((1, MOD_ROWS, tn), lambda i, j: (mod_layer, 0, gate_off + j)),
                  pl.BlockSpec((1, D_MODEL), const),
                  pl.BlockSpec((MOD_ROWS, 3 * D_MODEL), const)],
        out_specs=out_specs,
        scratch_shapes=[pltpu.VMEM((nj, OUT_TM, tn), F32),
                        pltpu.VMEM((OUT_TM, 1), F32)],
        compiler_params=_cparams(("parallel", "arbitrary")),
        name="outproj",
    )(a, w, h, mods, next_g.reshape(1, D_MODEL), next_mod)


def _na_tile_patterns():
    pats = []
    for t in (0, 1, NA_TILES - 1):
        ws = min(max(NA_QROWS * t - NA_WIN_ROWS // 2, 0), NA_ROWS - NA_KROWS)
        pat = np.full((NA_QROWS, NA_KROWS), NA_NDR, dtype=np.int64)
        for a in range(NA_QROWS):
            r = NA_QROWS * t + a
            rs = min(max(r - NA_WIN_ROWS // 2, 0), NA_ROWS - NA_WIN_ROWS)
            for jj in range(NA_KROWS):
                kabs = ws + jj
                if rs <= kabs < rs + NA_WIN_ROWS:
                    pat[a, jj] = kabs - r + NA_WIN_ROWS - 1
        pats.append(pat)
    return pats


def _na_build_bias(rpb_ref, slab_ref, bias_ref):
    w2 = 2 * GRID_W
    qc = lax.broadcasted_iota(jnp.int32, (GRID_W, w2), 0)
    lane = lax.broadcasted_iota(jnp.int32, (GRID_W, w2), 1)
    left = lane < GRID_W
    kc = jnp.where(left, lane, lane - GRID_W)
    c_start = jnp.clip(qc - NA_WIN_COLS // 2, 0, GRID_W - NA_WIN_COLS)
    col_ok = jnp.logical_and(kc >= c_start, kc < c_start + NA_WIN_COLS)
    shift = w2 - (NA_WIN_COLS - 1)
    for dr in range(NA_NDR):
        row = jnp.broadcast_to(rpb_ref[0, dr:dr + 1, :] * LOG2E, (GRID_W, w2))
        lo = pltpu.roll(row, shift, 1, stride=1, stride_axis=0)
        hi = pltpu.roll(row, (shift + GRID_W) % w2, 1, stride=1, stride_axis=0)
        slab_ref[dr] = jnp.where(col_ok, jnp.where(left, lo, hi), NEG)
    slab_ref[NA_NDR] = jnp.full((GRID_W, w2), NEG, F32)
    for p, pat in enumerate(_na_tile_patterns()):
        for a in range(NA_QROWS):
            for jp in range(NA_KROWS // 2):
                blk = jnp.where(left, slab_ref[int(pat[a, 2 * jp])], slab_ref[int(pat[a, 2 * jp + 1])])
                bias_ref[p, a * GRID_W:(a + 1) * GRID_W, jp * w2:(jp + 1) * w2] = blk


def _na_kernel(q_ref, k_ref, v_ref, z_ref, rpb_ref, o_ref, bias_ref, slab_ref, *, need_ctx):
    @pl.when(pl.program_id(1) == 0)
    def _():
        _na_build_bias(rpb_ref, slab_ref, bias_ref)

    kc = k_ref[0, 0:CTX_LEN, :]
    vc = v_ref[0, 0:CTX_LEN, :]

    def finish(o, l, r0):
        z = z_ref[0, pl.ds(r0, NA_TQ), :].astype(F32)
        o_ref[0, pl.ds(r0, NA_TQ), :] = (o * (1.0 / l) * _silu(z)).astype(o_ref.dtype)

    if need_ctx:
        s = lax.dot_general(q_ref[0, 0:CTX_LEN, :], kc, _NT, preferred_element_type=F32)
        p = jnp.exp2(s - jnp.max(s, axis=-1, keepdims=True))
        finish(jnp.dot(p.astype(BF16), vc, preferred_element_type=F32), jnp.sum(p, axis=-1, keepdims=True), 0)
    else:
        o_ref[0, 0:CTX_LEN, :] = jnp.zeros((CTX_LEN, NA_HEAD_DIM), o_ref.dtype)

    def key_start(t):
        ws = jnp.clip(NA_QROWS * t - NA_WIN_ROWS // 2, 0, NA_ROWS - NA_KROWS)
        return pl.multiple_of(CTX_LEN + ws * GRID_W, GRID_W)

    def scores(t):
        q = q_ref[0, pl.ds(pl.multiple_of(CTX_LEN + t * NA_TQ, NA_TQ), NA_TQ), :]
        pat = jnp.where(t == 0, 0, jnp.where(t == NA_TILES - 1, 2, 1))
        s1 = lax.dot_general(q, k_ref[0, pl.ds(key_start(t), NA_TK), :], _NT,
                             preferred_element_type=F32) + bias_ref[pat]
        return s1, lax.dot_general(q, kc, _NT, preferred_element_type=F32)

    def tile(t, carry):
        s1, s2 = carry
        nxt = scores(jnp.minimum(t + 1, NA_TILES - 1))
        m = jnp.maximum(jnp.max(s1, axis=-1, keepdims=True), jnp.max(s2, axis=-1, keepdims=True))
        p1 = jnp.exp2(s1 - m)
        p2 = jnp.exp2(s2 - m)
        l = jnp.sum(p1, axis=-1, keepdims=True) + jnp.sum(p2, axis=-1, keepdims=True)
        vw = v_ref[0, pl.ds(key_start(t), NA_TK), :]
        o = (jnp.dot(p1.astype(BF16), vw, preferred_element_type=F32)
             + jnp.dot(p2.astype(BF16), vc, preferred_element_type=F32))
        finish(o, l, pl.multiple_of(CTX_LEN + t * NA_TQ, NA_TQ))
        return nxt

    lax.fori_loop(0, NA_TILES, tile, scores(0), unroll=2)


def _na_attention(p, rpb, need_ctx):
    hd = NA_HEAD_DIM
    w2 = 2 * GRID_W
    rpb_pad = jnp.pad(rpb, ((0, 0), (0, NA_NDR + 1 - rpb.shape[1]), (0, w2 - rpb.shape[2])))
    blk = lambda off: pl.BlockSpec((1, LTOT, hd), lambda h, b: (b, 0, off + h))
    return pl.pallas_call(
        functools.partial(_na_kernel, need_ctx=need_ctx),
        out_shape=jax.ShapeDtypeStruct((BATCH, LTOT, D_MODEL), BF16),
        grid=(NA_HEADS, BATCH),
        in_specs=[blk(0), blk(NA_HEADS), blk(2 * NA_HEADS), blk(3 * NA_HEADS),
                  pl.BlockSpec((1, NA_NDR + 1, w2), lambda h, b: (h, 0, 0))],
        out_specs=blk(0),
        scratch_shapes=[pltpu.VMEM((3, NA_TQ, NA_TK), F32),
                        pltpu.VMEM((NA_NDR + 1, GRID_W, w2), F32)],
        compiler_params=_cparams(("parallel", "arbitrary")),
        name="na_attention",
    )(p, p, p, p, rpb_pad)


def _conv_kernel(u_ref, w_ref, b_ref, o_ref):
    u = u_ref[0].astype(F32)
    row = lax.broadcasted_iota(jnp.int32, u.shape, 0)
    seg_first = jnp.logical_or(row == 0, row == CTX_LEN)
    seg_last = jnp.logical_or(row == CTX_LEN - 1, row == LTOT - 1)
    up = jnp.where(seg_first, 0.0, pltpu.roll(u, 1, 0))
    un = jnp.where(seg_last, 0.0, pltpu.roll(u, LTOT - 1, 0))
    w = w_ref[...]
    y = w[0:1] * up + w[1:2] * u + w[2:3] * un + b_ref[...]
    o_ref[0] = _silu(y).astype(o_ref.dtype)


def _ssd_conv(p, conv_w, conv_b):
    tc = 256
    off = SSD_D_INNER // tc
    return pl.pallas_call(
        _conv_kernel,
        out_shape=jax.ShapeDtypeStruct((BATCH, LTOT, SSD_CONV_CH), BF16),
        grid=(BATCH, SSD_CONV_CH // tc),
        in_specs=[pl.BlockSpec((1, LTOT, tc), lambda b, j: (b, 0, off + j)),
                  pl.BlockSpec((3, tc), lambda b, j: (0, j)),
                  pl.BlockSpec((1, tc), lambda b, j: (0, j))],
        out_specs=pl.BlockSpec((1, LTOT, tc), lambda b, j: (b, 0, j)),
        compiler_params=_cparams(("parallel", "parallel")),
        name="ssd_conv",
    )(p, conv_w, conv_b.reshape(1, SSD_CONV_CH))


def _bf16_pieces(v):
    hi = v.astype(BF16)
    r1 = v - hi.astype(F32)
    mid = r1.astype(BF16)
    lo = (r1 - mid.astype(F32)).astype(BF16)
    return hi, mid, lo


def _split3(v):
    hi, mid, lo = _bf16_pieces(v)
    lane = lax.broadcasted_iota(jnp.int32, v.shape, 1)
    return jnp.where(lane < SSD_DL, hi, jnp.where(lane < 2 * SSD_DL, mid, lo))


def _ssd_kernel(x_ref, b_ref, c_ref, z_ref, dtc_ref, dtr_ref, pc_ref, pr_ref, dskip_ref,
                y_ref, yacc_ref, s_ref):
    q = SSD_CHUNK
    hp = SSD_HPG
    li = lax.broadcasted_iota(jnp.int32, (q, q), 0)
    si = lax.broadcasted_iota(jnp.int32, (q, q), 1)
    lower = li >= si
    upper = li <= si
    tri_lo = lower.astype(BF16)
    tri_up = upper.astype(BF16)
    tri_lanes = (jnp.concatenate([tri_lo] * 3, axis=1), jnp.concatenate([tri_up] * 3, axis=1))
    tri_rows = (jnp.concatenate([tri_up] * 3, axis=0), jnp.concatenate([tri_lo] * 3, axis=0))
    left = si < SSD_HEAD_DIM

    def expand_matrix(d):
        r = lax.broadcasted_iota(jnp.int32, (3 * SSD_DL, SSD_GW), 0) % SSD_DL
        c = lax.broadcasted_iota(jnp.int32, (3 * SSD_DL, SSD_GW), 1) // SSD_HEAD_DIM
        return (r == d * hp + c).astype(BF16)

    expand = (expand_matrix(0), expand_matrix(1))

    bias_c = pc_ref[0, 0:1, :]
    a_c = -jnp.exp(pc_ref[0, 1:2, :]) * LOG2E
    bias_r = pr_ref[0, :, 0:1]
    a_r = -jnp.exp(pr_ref[0, :, 1:2]) * LOG2E

    yacc_ref[...] = x_ref[0].astype(F32) * dskip_ref[0]
    s_ref[...] = jnp.zeros_like(s_ref)

    def prep(c, d):
        r0 = pl.multiple_of(c * q, q)
        ex = expand[d]
        dtc = _softplus(dtc_ref[0, 0, pl.ds(r0, q), :] + bias_c)
        acum_c = jnp.dot(tri_lanes[d], jnp.concatenate(_bf16_pieces(dtc * a_c), axis=0),
                         preferred_element_type=F32)
        dtr = _softplus(dtr_ref[0, 0, c] + bias_r)
        acum_r = jnp.dot(jnp.concatenate(_bf16_pieces(dtr * a_r), axis=1), tri_rows[d],
                         preferred_element_type=F32)
        tot_c = acum_c[q - 1:q, :] if d == 0 else acum_c[0:1, :]
        ea_c = jnp.exp2(acum_c)
        dw_c = dtc * jnp.exp2(tot_c - acum_c)
        dt_x = jnp.dot(_split3(dtc), ex, preferred_element_type=F32)
        ea_x = jnp.dot(_split3(ea_c), ex, preferred_element_type=F32)
        dw_x = jnp.dot(_split3(dw_c), ex, preferred_element_type=F32)
        return dt_x, ea_x, dw_x, acum_c, acum_r

    def chunk(c, d, prepared):
        dt_x, ea_x, dw_x, acum_c, acum_r = prepared
        r0 = pl.multiple_of(c * q, q)
        mask = lower if d == 0 else upper
        xf = x_ref[0, pl.ds(r0, q), :].astype(F32)
        bm = b_ref[0, pl.ds(r0, q), :]
        cm = c_ref[0, pl.ds(r0, q), :]
        etot_x = ea_x[q - 1:q, :] if d == 0 else ea_x[0:1, :]
        cb = jnp.where(mask, lax.dot_general(cm, bm, _NT, preferred_element_type=F32), 0.0)
        bt = bm.astype(F32).T.astype(BF16)
        sprev = s_ref[d]
        yoff = jnp.dot(cm, sprev.astype(BF16), preferred_element_type=F32)
        xd_b = (xf * dt_x).astype(BF16)
        s_ref[d] = sprev * etot_x + jnp.dot(bt, (xf * dw_x).astype(BF16), preferred_element_type=F32)
        zero = jnp.zeros((q, q), BF16)
        for p in range(hp // 2):
            cs = slice(p * q, (p + 1) * q)
            m_pair = []
            for ln in (d * hp + 2 * p, d * hp + 2 * p + 1):
                diff = jnp.minimum(acum_c[:, ln:ln + 1] - acum_r[ln:ln + 1, :], 0.0)
                m_pair.append((cb * jnp.exp2(diff)).astype(BF16))
            xd_p = xd_b[:, cs]
            rhs = jnp.concatenate([jnp.where(left, xd_p, zero), jnp.where(left, zero, xd_p)], axis=0)
            yd = jnp.dot(jnp.concatenate(m_pair, axis=1), rhs, preferred_element_type=F32)
            yacc_ref[pl.ds(r0, q), cs] += yd + ea_x[:, cs] * yoff[:, cs]

    def bwd_chunk(k):
        return jnp.where(k < SSD_CCHUNK, SSD_CCHUNK - 1 - k, SSD_NCHUNK - 1 + SSD_CCHUNK - k)

    def step(k, carry):
        kn = jnp.minimum(k + 1, SSD_NCHUNK - 1)
        nxt = (prep(kn, 0), prep(bwd_chunk(kn), 1))
        chunk(k, 0, carry[0])
        chunk(bwd_chunk(k), 1, carry[1])
        return nxt

    lax.fori_loop(0, SSD_NCHUNK, step, (prep(0, 0), prep(bwd_chunk(0), 1)))

    y_ref[0] = (yacc_ref[...] * _silu(z_ref[0].astype(F32))).astype(y_ref.dtype)


def _ssd_scan(xbc, p, dt_raw, dt_bias, a_log, d_skip):
    g, hp, dl = SSD_GROUPS, SSD_HPG, SSD_DL
    dt = dt_raw.reshape(BATCH, LTOT, 2, g, hp).transpose(0, 3, 1, 2, 4).reshape(BATCH, g, LTOT, dl)
    dt_col = jnp.tile(dt, (1, 1, 1, 3))
    dt_row = dt.reshape(BATCH, g, SSD_NCHUNK, SSD_CHUNK, dl).transpose(0, 1, 2, 4, 3)
    par = jnp.stack([dt_bias, a_log]).reshape(2, 2, g, hp).transpose(2, 0, 1, 3).reshape(g, 2, dl)
    par_col = jnp.pad(jnp.tile(par, (1, 1, 3)), ((0, 0), (0, 6), (0, 0)))
    par_row = jnp.pad(par.transpose(0, 2, 1), ((0, 0), (0, 0), (0, 126)))
    dskip = jnp.repeat(d_skip, SSD_HEAD_DIM).reshape(g, 1, SSD_GW)
    xoff = SSD_D_INNER // SSD_STATE
    return pl.pallas_call(
        _ssd_kernel,
        out_shape=jax.ShapeDtypeStruct((BATCH, LTOT, SSD_D_INNER), BF16),
        grid=(BATCH, g),
        in_specs=[pl.BlockSpec((1, LTOT, SSD_GW), lambda b, j: (b, 0, j)),
                  pl.BlockSpec((1, LTOT, SSD_STATE), lambda b, j: (b, 0, xoff + j)),
                  pl.BlockSpec((1, LTOT, SSD_STATE), lambda b, j: (b, 0, xoff + g + j)),
                  pl.BlockSpec((1, LTOT, SSD_GW), lambda b, j: (b, 0, j)),
                  pl.BlockSpec((1, 1, LTOT, 3 * dl), lambda b, j: (b, j, 0, 0)),
                  pl.BlockSpec((1, 1, SSD_NCHUNK, dl, SSD_CHUNK), lambda b, j: (b, j, 0, 0, 0)),
                  pl.BlockSpec((1, 8, 3 * dl), lambda b, j: (j, 0, 0)),
                  pl.BlockSpec((1, dl, 128), lambda b, j: (j, 0, 0)),
                  pl.BlockSpec((1, 1, SSD_GW), lambda b, j: (j, 0, 0))],
        out_specs=pl.BlockSpec((1, LTOT, SSD_GW), lambda b, j: (b, 0, j)),
        scratch_shapes=[pltpu.VMEM((LTOT, SSD_GW), F32),
                        pltpu.VMEM((2, SSD_STATE, SSD_GW), F32)],
        compiler_params=_cparams(("parallel", "parallel")),
        name="ssd_scan",
    )(xbc, xbc, xbc, p, dt_col, dt_row, par_col, par_row, dskip)


def _sg_kernel(z_ref, u_ref, v_ref, g_ref, b_ref, ws_ref, bs_ref, o_ref):
    v = v_ref[...].astype(F32)
    mu = jnp.mean(v, axis=-1, keepdims=True)
    vc = v - mu
    var = jnp.mean(vc * vc, axis=-1, keepdims=True)
    vn = (vc * lax.rsqrt(var + EPS) * g_ref[...] + b_ref[...]).astype(BF16)
    for g in range(SG_GROUPS):
        cs = slice(g * SG_GW, (g + 1) * SG_GW)
        sv = jnp.dot(ws_ref[g], vn[:, cs], preferred_element_type=F32) + bs_ref[:, g:g + 1]
        o_ref[:, cs] = (u_ref[:, cs].astype(F32) * sv * _silu(z_ref[:, cs].astype(F32))).astype(o_ref.dtype)


def _sg_gate(p, ln_g, ln_b, w_s, b_s_t):
    rows = p.shape[0]
    blk = lambda j: pl.BlockSpec((SG_CHUNK, SG_HALF), lambda i: (i, j))
    return pl.pallas_call(
        _sg_kernel,
        out_shape=jax.ShapeDtypeStruct((rows, SG_HALF), BF16),
        grid=(rows // SG_CHUNK,),
        in_specs=[blk(0), blk(1), blk(2),
                  pl.BlockSpec((1, SG_HALF), lambda i: (0, 0)),
                  pl.BlockSpec((1, SG_HALF), lambda i: (0, 0)),
                  pl.BlockSpec((SG_GROUPS, SG_CHUNK, SG_CHUNK), lambda i: (0, 0, 0)),
                  pl.BlockSpec((SG_CHUNK, SG_GROUPS), lambda i: (0, 0))],
        out_specs=blk(0),
        compiler_params=_cparams(("parallel",)),
        name="sg_gate",
    )(p, p, p, ln_g.reshape(1, SG_HALF), ln_b.reshape(1, SG_HALF), w_s, b_s_t)


def kernel(x, c, ctx, c_ctx, norm_g, ada_w, ada_b, na_w_in, na_rpb, na_w_out,
           ssd_w_in, ssd_conv_w, ssd_conv_b, ssd_dt_bias, ssd_a_log, ssd_d_skip,
           ssd_norm_g, ssd_w_out, sg_w_in, sg_ln_g, sg_ln_b, sg_w_s, sg_b_s, sg_w_out,
           final_norm_g):
    d = D_MODEL
    c_rows = jnp.concatenate([c, c_ctx[None], jnp.zeros((MOD_ROWS - BATCH - 1, d), F32)], axis=0)
    mods = _ada_all(c_rows, ada_w, ada_b)

    h = jnp.concatenate([ctx, x], axis=1).reshape(ROWS, d)
    n = _norm_mod(h, norm_g[0], mods[0])

    na_w_out_b, ssd_w_out_b, sg_w_out_b = (w.astype(BF16) for w in (na_w_out, ssd_w_out, sg_w_out))

    for i in range(DEPTH):
        kind, j = i % N_MIXERS, i // N_MIXERS
        need_ctx = i < DEPTH - 1
        if kind == 0:
            p = _proj(n, na_w_in, j, n=4 * d, out_dtype=BF16, scale_cols=d, scale=NA_HEAD_DIM ** -0.5 * LOG2E)
            y = _na_attention(p.reshape(BATCH, LTOT, 4 * d), na_rpb[j], need_ctx).reshape(ROWS, d)
            w_out = na_w_out_b
        elif kind == 1:
            p = _proj(n, ssd_w_in, j, n=SSD_MAIN, out_dtype=BF16).reshape(BATCH, LTOT, SSD_MAIN)
            dt_raw = _proj(n, ssd_w_in, j, n=2 * SSD_HEADS, out_dtype=F32, tn=2 * SSD_HEADS, col0=SSD_MAIN)
            xbc = _ssd_conv(p, ssd_conv_w[j], ssd_conv_b[j])
            yg = _ssd_scan(xbc, p, dt_raw, ssd_dt_bias[j], ssd_a_log[j], ssd_d_skip[j])
            y = _rmsnorm(yg.reshape(ROWS, SSD_D_INNER), ssd_norm_g[j], out_dtype=BF16)
            w_out = ssd_w_out_b
        else:
            p = _proj(n, sg_w_in, j, n=3 * SG_HALF, out_dtype=BF16, gelu_from_col=SG_HALF)
            y = _sg_gate(p, sg_ln_g[j], sg_ln_b[j], sg_w_s[j].astype(BF16), jnp.transpose(sg_b_s[j]))
            w_out = sg_w_out_b

        if need_ctx:
            h, n = _outproj(y, w_out, j, h, mods, i, norm_g[i + 1], mods[i + 1], final=False)
        else:
            out = _outproj(y, w_out, j, h, mods, i, final_norm_g, mods[i], final=True)

    return out.reshape(BATCH, LTOT, d)[:, CTX_LEN:]
```

```python
import functools
import math

import numpy as np
import jax
import jax.numpy as jnp
from jax import lax
from jax.experimental import pallas as pl
from jax.experimental.pallas import tpu as pltpu

D_MODEL = 2048
BATCH = 4
SEQ = 2048
DEPTH = 4
GRID_W = 64
CTX_LEN = 256
N_MIXERS = 3
EPS = 1e-6
LTOT = CTX_LEN + SEQ
ROWS = BATCH * LTOT

NA_HEADS = 16
NA_HEAD_DIM = D_MODEL // NA_HEADS
NA_WIN_ROWS = 8
NA_WIN_COLS = 16
NA_ROWS = SEQ // GRID_W
NA_QROWS = 4
NA_KROWS = NA_QROWS + NA_WIN_ROWS
NA_TQ = NA_QROWS * GRID_W
NA_TK = NA_KROWS * GRID_W
NA_TILES = NA_ROWS // NA_QROWS
NA_NDR = 2 * NA_WIN_ROWS - 1

SSD_D_INNER = 2 * D_MODEL
SSD_HEAD_DIM = 64
SSD_HEADS = SSD_D_INNER // SSD_HEAD_DIM
SSD_GROUPS = 8
SSD_STATE = 128
SSD_CHUNK = 128
SSD_GN = SSD_GROUPS * SSD_STATE
SSD_CONV_CH = SSD_D_INNER + 2 * SSD_GN
SSD_MAIN = SSD_D_INNER + SSD_CONV_CH
SSD_HPG = SSD_HEADS // SSD_GROUPS
SSD_GW = SSD_HPG * SSD_HEAD_DIM
SSD_NCHUNK = LTOT // SSD_CHUNK
SSD_CCHUNK = CTX_LEN // SSD_CHUNK
SSD_DL = 2 * SSD_HPG

SG_HALF = 3 * D_MODEL
SG_GROUPS = 16
SG_CHUNK = 128
SG_GW = SG_HALF // SG_GROUPS

NEG = -1e30
VMEM_LIMIT = 56 * 1024 * 1024
MOD_ROWS = 8

PROJ_TM, PROJ_TN = 1024, 1024
GELU_SPLIT = 4
OUT_TM = 768
LOG2E = math.log2(math.e)

F32 = jnp.float32
BF16 = jnp.bfloat16
_NT = (((1,), (1,)), ((), ()))


def _cparams(sem):
    return pltpu.CompilerParams(dimension_semantics=sem, vmem_limit_bytes=VMEM_LIMIT)


def _silu(x):
    return x * (1.0 / (1.0 + jnp.exp(-x)))


def _gelu_tanh(x):
    c = math.sqrt(2.0 / math.pi)
    return 0.5 * x * (1.0 + jnp.tanh(c * (x + 0.044715 * (x * x * x))))


def _softplus(x):
    return jnp.maximum(x, 0.0) + jnp.log(1.0 + jnp.exp(-jnp.abs(x)))


def _ada_kernel(c_ref, w_ref, b_ref, o_ref):
    a = _silu(c_ref[...]).astype(BF16)
    o_ref[0] = jnp.dot(a, w_ref[0].astype(BF16), preferred_element_type=F32) + b_ref[0]


def _ada_all(c_rows, ada_w, ada_b):
    tn = 1024
    return pl.pallas_call(
        _ada_kernel,
        out_shape=jax.ShapeDtypeStruct((DEPTH, MOD_ROWS, 3 * D_MODEL), F32),
        grid=(DEPTH, 3 * D_MODEL // tn),
        in_specs=[pl.BlockSpec((MOD_ROWS, D_MODEL), lambda l, j: (0, 0)),
                  pl.BlockSpec((1, D_MODEL, tn), lambda l, j: (l, 0, j)),
                  pl.BlockSpec((1, 1, tn), lambda l, j: (l, 0, j))],
        out_specs=pl.BlockSpec((1, MOD_ROWS, tn), lambda l, j: (l, 0, j)),
        compiler_params=_cparams(("arbitrary", "arbitrary")),
        name="ada_mod",
    )(c_rows, ada_w, ada_b.reshape(DEPTH, 1, 3 * D_MODEL))


def _mod_rows(mod_ref, row):
    m = mod_ref[pl.ds(row, 1), :]
    return m[:, :D_MODEL], m[:, D_MODEL:2 * D_MODEL], m[:, 2 * D_MODEL:]


def _norm_mod_kernel(ctx_ref, x_ref, g_ref, mod_ref, h_ref, o_ref):
    b = pl.program_id(0)
    r = pl.program_id(1)
    is_ctx = r == 0
    shift, scale, _ = _mod_rows(mod_ref, jnp.where(is_ctx, BATCH, b))
    x = jnp.where(is_ctx, ctx_ref[0], x_ref[0])
    h_ref[...] = x
    y = x * lax.rsqrt(jnp.mean(x * x, axis=-1, keepdims=True) + EPS) * g_ref[...]
    o_ref[...] = (y * (1.0 + scale) + shift).astype(o_ref.dtype)


def _norm_mod(ctx, x, g, mod):
    per = LTOT // CTX_LEN
    rows = pl.BlockSpec((CTX_LEN, D_MODEL), lambda b, r: (b * per + r, 0))
    return pl.pallas_call(
        _norm_mod_kernel,
        out_shape=(jax.ShapeDtypeStruct((ROWS, D_MODEL), F32), jax.ShapeDtypeStruct((ROWS, D_MODEL), BF16)),
        grid=(BATCH, per),
        in_specs=[pl.BlockSpec((1, CTX_LEN, D_MODEL), lambda b, r: (b, 0, 0)),
                  pl.BlockSpec((1, CTX_LEN, D_MODEL), lambda b, r: (b, jnp.maximum(r - 1, 0), 0)),
                  pl.BlockSpec((1, D_MODEL), lambda b, r: (0, 0)),
                  pl.BlockSpec((MOD_ROWS, 3 * D_MODEL), lambda b, r: (0, 0))],
        out_specs=(rows, rows),
        compiler_params=_cparams(("parallel", "parallel")),
        name="norm_mod",
    )(ctx, x, g.reshape(1, D_MODEL), mod)


def _proj_kernel(a_ref, w_ref, o_ref, wb_ref, *, gelu_from, scale_tiles, scale):
    j = pl.program_id(0)
    i = pl.program_id(1)

    @pl.when(i == 0)
    def _():
        w = w_ref[0]
        if scale_tiles:
            w = w * jnp.where(j < scale_tiles, scale, 1.0)
        wb_ref[...] = w.astype(BF16)

    def plain():
        o_ref[...] = jnp.dot(a_ref[...], wb_ref[...], preferred_element_type=F32).astype(o_ref.dtype)

    if gelu_from is None:
        plain()
    else:
        @pl.when(j >= gelu_from)
        def _():
            sub = PROJ_TM // GELU_SPLIT
            for r in range(GELU_SPLIT):
                rs = slice(r * sub, (r + 1) * sub)
                acc = jnp.dot(a_ref[rs, :], wb_ref[...], preferred_element_type=F32)
                o_ref[rs, :] = _gelu_tanh(acc).astype(o_ref.dtype)

        pl.when(j < gelu_from)(plain)


def _proj(a, w, layer, *, n, out_dtype, tn=PROJ_TN, col0=0, gelu_from_col=None, scale_cols=0, scale=1.0):
    m, kdim = a.shape
    off = col0 // tn
    return pl.pallas_call(
        functools.partial(_proj_kernel,
                          gelu_from=None if gelu_from_col is None else gelu_from_col // tn,
                          scale_tiles=scale_cols // tn, scale=scale),
        out_shape=jax.ShapeDtypeStruct((m, n), out_dtype),
        grid=(n // tn, m // PROJ_TM),
        in_specs=[pl.BlockSpec((PROJ_TM, kdim), lambda j, i: (i, 0)),
                  pl.BlockSpec((1, kdim, tn), lambda j, i: (layer, 0, off + j))],
        out_specs=pl.BlockSpec((PROJ_TM, tn), lambda j, i: (i, j)),
        scratch_shapes=[pltpu.VMEM((kdim, tn), BF16)],
        compiler_params=_cparams(("parallel", "arbitrary")),
        name="proj",
    )(a, w)


def _outproj_kernel(a_ref, w_ref, h_ref, gate_ref, ng_ref, nmod_ref, *rest, nj, tn, final, resident, rms_a):
    if final:
        n_ref, hs_ref, ss_ref, ra_ref = rest
        hn_ref = None
    else:
        hn_ref, n_ref, hs_ref, ss_ref, ra_ref = rest
    i = pl.program_id(0)
    j = pl.program_id(1)

    if rms_a:
        @pl.when(j == 0)
        def _():
            kdim = a_ref.shape[1]
            ssq = jnp.zeros((OUT_TM, 1), F32)
            for c0 in range(0, kdim, 512):
                af = a_ref[:, c0:c0 + 512].astype(F32)
                ssq = ssq + jnp.sum(af * af, axis=-1, keepdims=True)
            ra_ref[...] = lax.rsqrt(ssq * (1.0 / kdim) + EPS)
    per = LTOT // OUT_TM
    b = i // per
    first = i % per == 0
    top = slice(0, CTX_LEN)
    bot = slice(CTX_LEN, OUT_TM)

    def rows(ref, cols):
        lat = ref[pl.ds(b, 1), cols]
        return jnp.where(first, ref[BATCH:BATCH + 1, cols], lat), lat

    w = w_ref[0, j] if resident else w_ref[0, 0]
    acc = jnp.dot(a_ref[...], w, preferred_element_type=F32)
    if rms_a:
        acc = acc * ra_ref[...]
    g_top, g_bot = rows(gate_ref.at[0], slice(None))
    hn_t = h_ref[top, :] + g_top * acc[top, :]
    hn_b = h_ref[bot, :] + g_bot * acc[bot, :]
    hs_ref[j, top, :] = hn_t
    hs_ref[j, bot, :] = hn_b
    if not final:
        hn_ref[top, :] = hn_t
        hn_ref[bot, :] = hn_b
    sq_t = jnp.sum(hn_t * hn_t, axis=-1, keepdims=True)
    sq_b = jnp.sum(hn_b * hn_b, axis=-1, keepdims=True)

    @pl.when(j == 0)
    def _():
        ss_ref[top, :] = sq_t
        ss_ref[bot, :] = sq_b

    @pl.when(j > 0)
    def _():
        ss_ref[top, :] += sq_t
        ss_ref[bot, :] += sq_b

    @pl.when(j == nj - 1)
    def _():
        inv = lax.rsqrt(ss_ref[...] * (1.0 / D_MODEL) + EPS)
        for jj in range(nj):
            cs = slice(jj * tn, (jj + 1) * tn)
            y = hs_ref[jj] * inv * ng_ref[:, cs]
            if final:
                n_ref[:, cs] = y
            else:
                sh_top, sh_bot = rows(nmod_ref, cs)
                sc_top, sc_bot = rows(nmod_ref, slice(D_MODEL + jj * tn, D_MODEL + (jj + 1) * tn))
                n_ref[top, cs] = (y[top, :] * (1.0 + sc_top) + sh_top).astype(n_ref.dtype)
                n_ref[bot, cs] = (y[bot, :] * (1.0 + sc_bot) + sh_bot).astype(n_ref.dtype)


def _outproj_tn(kdim):
    return 512 if kdim <= 2 * D_MODEL else 256


def _outproj_weights(w, row_gain=None):
    nl, kdim, _ = w.shape
    tn = _outproj_tn(kdim)
    if row_gain is not None:
        w = w * row_gain[:, :, None]
    return w.astype(BF16).reshape(nl, kdim, D_MODEL // tn, tn).transpose(0, 2, 1, 3)


def _outproj(a, w, layer, h, mods, mod_layer, next_g, next_mod, *, final, rms_a=False):
    m, kdim = a.shape
    tn = _outproj_tn(kdim)
    nj = D_MODEL // tn
    gate_off = 2 * D_MODEL // tn
    resident = kdim <= 2 * D_MODEL
    if resident:
        w_spec = pl.BlockSpec((1, nj, kdim, tn), lambda i, j: (layer, 0, 0, 0), pipeline_mode=pl.Buffered(1))
    else:
        w_spec = pl.BlockSpec((1, 1, kdim, tn), lambda i, j: (layer, j, 0, 0))
    tile = pl.BlockSpec((OUT_TM, tn), lambda i, j: (i, j))
    full = pl.BlockSpec((OUT_TM, D_MODEL), lambda i, j: (i, 0))
    const = lambda i, j: (0, 0)
    if final:
        out_shape = jax.ShapeDtypeStruct((m, D_MODEL), F32)
        out_specs = full
    else:
        out_shape = (jax.ShapeDtypeStruct((m, D_MODEL), F32), jax.ShapeDtypeStruct((m, D_MODEL), BF16))
        out_specs = (tile, full)
    return pl.pallas_call(
        functools.partial(_outproj_kernel, nj=nj, tn=tn, final=final, resident=resident, rms_a=rms_a),
        out_shape=out_shape,
        grid=(m // OUT_TM, nj),
        in_specs=[pl.BlockSpec((OUT_TM, kdim), lambda i, j: (i, 0)),
                  w_spec,
                  tile,
                  pl.BlockSpec((1, MOD_ROWS, tn), lambda i, j: (mod_layer, 0, gate_off + j)),
                  pl.BlockSpec((1, D_MODEL), const),
                  pl.BlockSpec((MOD_ROWS, 3 * D_MODEL), const)],
        out_specs=out_specs,
        scratch_shapes=[pltpu.VMEM((nj, OUT_TM, tn), F32),
                        pltpu.VMEM((OUT_TM, 1), F32),
                        pltpu.VMEM((OUT_TM, 1), F32)],
        compiler_params=_cparams(("parallel", "arbitrary")),
        name="outproj",
    )(a, w, h, mods, next_g.reshape(1, D_MODEL), next_mod)


def _na_tile_patterns():
    pats = []
    for t in (0, 1, NA_TILES - 1):
        ws = min(max(NA_QROWS * t - NA_WIN_ROWS // 2, 0), NA_ROWS - NA_KROWS)
        pat = np.full((NA_QROWS, NA_KROWS), NA_NDR, dtype=np.int64)
        for a in range(NA_QROWS):
            r = NA_QROWS * t + a
            rs = min(max(r - NA_WIN_ROWS // 2, 0), NA_ROWS - NA_WIN_ROWS)
            for jj in range(NA_KROWS):
                kabs = ws + jj
                if rs <= kabs < rs + NA_WIN_ROWS:
                    pat[a, jj] = kabs - r + NA_WIN_ROWS - 1
        pats.append(pat)
    return pats


def _na_build_bias(rpb_ref, slab_ref, bias_ref):
    w2 = 2 * GRID_W
    qc = lax.broadcasted_iota(jnp.int32, (GRID_W, w2), 0)
    lane = lax.broadcasted_iota(jnp.int32, (GRID_W, w2), 1)
    left = lane < GRID_W
    kc = jnp.where(left, lane, lane - GRID_W)
    c_start = jnp.clip(qc - NA_WIN_COLS // 2, 0, GRID_W - NA_WIN_COLS)
    col_ok = jnp.logical_and(kc >= c_start, kc < c_start + NA_WIN_COLS)
    shift = w2 - (NA_WIN_COLS - 1)
    for dr in range(NA_NDR):
        row = jnp.broadcast_to(rpb_ref[0, dr:dr + 1, :] * LOG2E, (GRID_W, w2))
        lo = pltpu.roll(row, shift, 1, stride=1, stride_axis=0)
        hi = pltpu.roll(row, (shift + GRID_W) % w2, 1, stride=1, stride_axis=0)
        slab_ref[dr] = jnp.where(col_ok, jnp.where(left, lo, hi), NEG)
    slab_ref[NA_NDR] = jnp.full((GRID_W, w2), NEG, F32)
    for p, pat in enumerate(_na_tile_patterns()):
        for a in range(NA_QROWS):
            for jp in range(NA_KROWS // 2):
                blk = jnp.where(left, slab_ref[int(pat[a, 2 * jp])], slab_ref[int(pat[a, 2 * jp + 1])])
                bias_ref[p, a * GRID_W:(a + 1) * GRID_W, jp * w2:(jp + 1) * w2] = blk


def _na_kernel(q_ref, k_ref, v_ref, z_ref, rpb_ref, o_ref, bias_ref, slab_ref, *, need_ctx):
    @pl.when(pl.program_id(1) == 0)
    def _():
        _na_build_bias(rpb_ref, slab_ref, bias_ref)

    kc = k_ref[0, 0:CTX_LEN, :]
    vc = v_ref[0, 0:CTX_LEN, :]

    def finish(o, l, r0):
        z = z_ref[0, pl.ds(r0, NA_TQ), :].astype(F32)
        o_ref[0, pl.ds(r0, NA_TQ), :] = (o * (1.0 / l) * _silu(z)).astype(o_ref.dtype)

    if need_ctx:
        s = lax.dot_general(q_ref[0, 0:CTX_LEN, :], kc, _NT, preferred_element_type=F32)
        p = jnp.exp2(s - jnp.max(s, axis=-1, keepdims=True))
        finish(jnp.dot(p.astype(BF16), vc, preferred_element_type=F32), jnp.sum(p, axis=-1, keepdims=True), 0)
    else:
        o_ref[0, 0:CTX_LEN, :] = jnp.zeros((CTX_LEN, NA_HEAD_DIM), o_ref.dtype)

    def key_start(t):
        ws = jnp.clip(NA_QROWS * t - NA_WIN_ROWS // 2, 0, NA_ROWS - NA_KROWS)
        return pl.multiple_of(CTX_LEN + ws * GRID_W, GRID_W)

    def scores(t):
        q = q_ref[0, pl.ds(pl.multiple_of(CTX_LEN + t * NA_TQ, NA_TQ), NA_TQ), :]
        pat = jnp.where(t == 0, 0, jnp.where(t == NA_TILES - 1, 2, 1))
        s1 = lax.dot_general(q, k_ref[0, pl.ds(key_start(t), NA_TK), :], _NT,
                             preferred_element_type=F32) + bias_ref[pat]
        return s1, lax.dot_general(q, kc, _NT, preferred_element_type=F32)

    def tile(t, carry):
        s1, s2 = carry
        nxt = scores(jnp.minimum(t + 1, NA_TILES - 1))
        m = jnp.maximum(jnp.max(s1, axis=-1, keepdims=True), jnp.max(s2, axis=-1, keepdims=True))
        p1 = jnp.exp2(s1 - m)
        p2 = jnp.exp2(s2 - m)
        l = jnp.sum(p1, axis=-1, keepdims=True) + jnp.sum(p2, axis=-1, keepdims=True)
        vw = v_ref[0, pl.ds(key_start(t), NA_TK), :]
        o = (jnp.dot(p1.astype(BF16), vw, preferred_element_type=F32)
             + jnp.dot(p2.astype(BF16), vc, preferred_element_type=F32))
        finish(o, l, pl.multiple_of(CTX_LEN + t * NA_TQ, NA_TQ))
        return nxt

    lax.fori_loop(0, NA_TILES, tile, scores(0), unroll=2)


def _na_attention(p, rpb, need_ctx):
    hd = NA_HEAD_DIM
    w2 = 2 * GRID_W
    rpb_pad = jnp.pad(rpb, ((0, 0), (0, NA_NDR + 1 - rpb.shape[1]), (0, w2 - rpb.shape[2])))
    blk = lambda off: pl.BlockSpec((1, LTOT, hd), lambda h, b: (b, 0, off + h))
    return pl.pallas_call(
        functools.partial(_na_kernel, need_ctx=need_ctx),
        out_shape=jax.ShapeDtypeStruct((BATCH, LTOT, D_MODEL), BF16),
        grid=(NA_HEADS, BATCH),
        in_specs=[blk(0), blk(NA_HEADS), blk(2 * NA_HEADS), blk(3 * NA_HEADS),
                  pl.BlockSpec((1, NA_NDR + 1, w2), lambda h, b: (h, 0, 0))],
        out_specs=blk(0),
        scratch_shapes=[pltpu.VMEM((3, NA_TQ, NA_TK), F32),
                        pltpu.VMEM((NA_NDR + 1, GRID_W, w2), F32)],
        compiler_params=_cparams(("parallel", "arbitrary")),
        name="na_attention",
    )(p, p, p, p, rpb_pad)


def _conv_kernel(u_ref, w_ref, b_ref, o_ref):
    u = u_ref[0].astype(F32)
    w = w_ref[...]
    y = w[0:1] * pltpu.roll(u, 1, 0) + w[1:2] * u + w[2:3] * pltpu.roll(u, LTOT - 1, 0) + b_ref[...]
    o_ref[0] = _silu(y).astype(o_ref.dtype)
    for t, has_prev, has_next in ((0, False, True), (CTX_LEN - 1, True, False),
                                  (CTX_LEN, False, True), (LTOT - 1, True, False)):
        yt = w[1:2] * u[t:t + 1] + b_ref[...]
        if has_prev:
            yt = yt + w[0:1] * u[t - 1:t]
        if has_next:
            yt = yt + w[2:3] * u[t + 1:t + 2]
        o_ref[0, t:t + 1, :] = _silu(yt).astype(o_ref.dtype)


def _ssd_conv(p, conv_w, conv_b):
    tc = 256
    off = SSD_D_INNER // tc
    return pl.pallas_call(
        _conv_kernel,
        out_shape=jax.ShapeDtypeStruct((BATCH, LTOT, SSD_CONV_CH), BF16),
        grid=(BATCH, SSD_CONV_CH // tc),
        in_specs=[pl.BlockSpec((1, LTOT, tc), lambda b, j: (b, 0, off + j)),
                  pl.BlockSpec((3, tc), lambda b, j: (0, j)),
                  pl.BlockSpec((1, tc), lambda b, j: (0, j))],
        out_specs=pl.BlockSpec((1, LTOT, tc), lambda b, j: (b, 0, j)),
        compiler_params=_cparams(("parallel", "parallel")),
        name="ssd_conv",
    )(p, conv_w, conv_b.reshape(1, SSD_CONV_CH))


def _bf16_pieces(v):
    hi = v.astype(BF16)
    r1 = v - hi.astype(F32)
    mid = r1.astype(BF16)
    lo = (r1 - mid.astype(F32)).astype(BF16)
    return hi, mid, lo


def _split3(v):
    hi, mid, lo = _bf16_pieces(v)
    lane = lax.broadcasted_iota(jnp.int32, v.shape, 1)
    return jnp.where(lane < SSD_DL, hi, jnp.where(lane < 2 * SSD_DL, mid, lo))


def _ssd_kernel(x_ref, b_ref, c_ref, z_ref, dtc_ref, dtr_ref, pc_ref, pr_ref, dskip_ref,
                y_ref, yacc_ref, s_ref):
    q = SSD_CHUNK
    hp = SSD_HPG
    li = lax.broadcasted_iota(jnp.int32, (q, q), 0)
    si = lax.broadcasted_iota(jnp.int32, (q, q), 1)
    lower = li >= si
    upper = li <= si
    tri_lo = lower.astype(BF16)
    tri_up = upper.astype(BF16)
    tri_lanes = (jnp.concatenate([tri_lo] * 3, axis=1), jnp.concatenate([tri_up] * 3, axis=1))
    tri_rows = (jnp.concatenate([tri_up] * 3, axis=0), jnp.concatenate([tri_lo] * 3, axis=0))
    left = si < SSD_HEAD_DIM

    def expand_matrix(d):
        r = lax.broadcasted_iota(jnp.int32, (3 * SSD_DL, SSD_GW), 0) % SSD_DL
        c = lax.broadcasted_iota(jnp.int32, (3 * SSD_DL, SSD_GW), 1) // SSD_HEAD_DIM
        return (r == d * hp + c).astype(BF16)

    expand = (expand_matrix(0), expand_matrix(1))

    bias_c = pc_ref[0, 0:1, :]
    a_c = -jnp.exp(pc_ref[0, 1:2, :]) * LOG2E
    bias_r = pr_ref[0, :, 0:1]
    a_r = -jnp.exp(pr_ref[0, :, 1:2]) * LOG2E

    s_ref[...] = jnp.zeros_like(s_ref)

    def prep(c, d):
        r0 = pl.multiple_of(c * q, q)
        ex = expand[d]
        dtc = _softplus(dtc_ref[0, 0, pl.ds(r0, q), :] + bias_c)
        acum_c = jnp.dot(tri_lanes[d], jnp.concatenate(_bf16_pieces(dtc * a_c), axis=0),
                         preferred_element_type=F32)
        dtr = _softplus(dtr_ref[0, 0, c] + bias_r)
        acum_r = jnp.dot(jnp.concatenate(_bf16_pieces(dtr * a_r), axis=1), tri_rows[d],
                         preferred_element_type=F32)
        tot_c = acum_c[q - 1:q, :] if d == 0 else acum_c[0:1, :]
        ea_c = jnp.exp2(acum_c)
        dw_c = dtc * jnp.exp2(tot_c - acum_c)
        ea_x = jnp.dot(_split3(ea_c), ex, preferred_element_type=F32)
        dw_x = jnp.dot(_split3(dw_c), ex, preferred_element_type=F32)
        return ea_x, dw_x, acum_c, acum_r, dtr

    def chunk(c, d, prepared):
        ea_x, dw_x, acum_c, acum_r, dtr = prepared
        r0 = pl.multiple_of(c * q, q)
        mask = lower if d == 0 else upper
        xb = x_ref[0, pl.ds(r0, q), :]
        bm = b_ref[0, pl.ds(r0, q), :]
        cm = c_ref[0, pl.ds(r0, q), :]
        etot_x = ea_x[q - 1:q, :] if d == 0 else ea_x[0:1, :]
        cb = jnp.where(mask, lax.dot_general(cm, bm, _NT, preferred_element_type=F32), 0.0)
        bt = bm.astype(F32).T.astype(BF16)
        sprev = s_ref[d]
        yoff = jnp.dot(cm, sprev.astype(BF16), preferred_element_type=F32)
        s_ref[d] = sprev * etot_x + jnp.dot(bt, (xb.astype(F32) * dw_x).astype(BF16),
                                            preferred_element_type=F32)
        zero = jnp.zeros((q, q), BF16)
        for p in range(hp // 2):
            cs = slice(p * q, (p + 1) * q)
            m_pair = []
            for ln in (d * hp + 2 * p, d * hp + 2 * p + 1):
                diff = jnp.minimum(acum_c[:, ln:ln + 1] - acum_r[ln:ln + 1, :], 0.0)
                m_pair.append((cb * jnp.exp2(diff) * dtr[ln:ln + 1, :]).astype(BF16))
            x_p = xb[:, cs]
            rhs = jnp.concatenate([jnp.where(left, x_p, zero), jnp.where(left, zero, x_p)], axis=0)
            yd = jnp.dot(jnp.concatenate(m_pair, axis=1), rhs, preferred_element_type=F32)
            yacc_ref[d, pl.ds(r0, q), cs] = yd + ea_x[:, cs] * yoff[:, cs]

    def bwd_chunk(k):
        return jnp.where(k < SSD_CCHUNK, SSD_CCHUNK - 1 - k, SSD_NCHUNK - 1 + SSD_CCHUNK - k)

    def step(k, carry):
        kn = jnp.minimum(k + 1, SSD_NCHUNK - 1)
        nxt = (prep(kn, 0), prep(bwd_chunk(kn), 1))
        chunk(k, 0, carry[0])
        chunk(bwd_chunk(k), 1, carry[1])
        return nxt

    lax.fori_loop(0, SSD_NCHUNK, step, (prep(0, 0), prep(bwd_chunk(0), 1)))

    y = yacc_ref[0] + yacc_ref[1] + x_ref[0].astype(F32) * dskip_ref[0]
    y_ref[0] = (y * _silu(z_ref[0].astype(F32))).astype(y_ref.dtype)


def _ssd_scan(xbc, p, dt_raw, dt_bias, a_log, d_skip):
    g, hp, dl = SSD_GROUPS, SSD_HPG, SSD_DL
    dt = dt_raw.reshape(BATCH, LTOT, 2, g, hp).transpose(0, 3, 1, 2, 4).reshape(BATCH, g, LTOT, dl)
    dt_col = jnp.tile(dt, (1, 1, 1, 3))
    dt_row = dt.reshape(BATCH, g, SSD_NCHUNK, SSD_CHUNK, dl).transpose(0, 1, 2, 4, 3)
    par = jnp.stack([dt_bias, a_log]).reshape(2, 2, g, hp).transpose(2, 0, 1, 3).reshape(g, 2, dl)
    par_col = jnp.pad(jnp.tile(par, (1, 1, 3)), ((0, 0), (0, 6), (0, 0)))
    par_row = jnp.pad(par.transpose(0, 2, 1), ((0, 0), (0, 0), (0, 126)))
    dskip = jnp.repeat(d_skip, SSD_HEAD_DIM).reshape(g, 1, SSD_GW)
    xoff = SSD_D_INNER // SSD_STATE
    return pl.pallas_call(
        _ssd_kernel,
        out_shape=jax.ShapeDtypeStruct((BATCH, LTOT, SSD_D_INNER), BF16),
        grid=(BATCH, g),
        in_specs=[pl.BlockSpec((1, LTOT, SSD_GW), lambda b, j: (b, 0, j)),
                  pl.BlockSpec((1, LTOT, SSD_STATE), lambda b, j: (b, 0, xoff + j)),
                  pl.BlockSpec((1, LTOT, SSD_STATE), lambda b, j: (b, 0, xoff + g + j)),
                  pl.BlockSpec((1, LTOT, SSD_GW), lambda b, j: (b, 0, j)),
                  pl.BlockSpec((1, 1, LTOT, 3 * dl), lambda b, j: (b, j, 0, 0)),
                  pl.BlockSpec((1, 1, SSD_NCHUNK, dl, SSD_CHUNK), lambda b, j: (b, j, 0, 0, 0)),
                  pl.BlockSpec((1, 8, 3 * dl), lambda b, j: (j, 0, 0)),
                  pl.BlockSpec((1, dl, 128), lambda b, j: (j, 0, 0)),
                  pl.BlockSpec((1, 1, SSD_GW), lambda b, j: (j, 0, 0))],
        out_specs=pl.BlockSpec((1, LTOT, SSD_GW), lambda b, j: (b, 0, j)),
        scratch_shapes=[pltpu.VMEM((2, LTOT, SSD_GW), F32),
                        pltpu.VMEM((2, SSD_STATE, SSD_GW), F32)],
        compiler_params=_cparams(("parallel", "parallel")),
        name="ssd_scan",
    )(xbc, xbc, xbc, p, dt_col, dt_row, par_col, par_row, dskip)


def _sg_kernel(z_ref, u_ref, v_ref, g_ref, b_ref, ws_ref, bs_ref, o_ref):
    v = v_ref[...].astype(F32)
    mu = jnp.mean(v, axis=-1, keepdims=True)
    vc = v - mu
    var = jnp.mean(vc * vc, axis=-1, keepdims=True)
    vn = (vc * lax.rsqrt(var + EPS) * g_ref[...] + b_ref[...]).astype(BF16)
    for g in range(SG_GROUPS):
        cs = slice(g * SG_GW, (g + 1) * SG_GW)
        sv = jnp.dot(ws_ref[g], vn[:, cs], preferred_element_type=F32) + bs_ref[:, g:g + 1]
        o_ref[:, cs] = (u_ref[:, cs].astype(F32) * sv * _silu(z_ref[:, cs].astype(F32))).astype(o_ref.dtype)


def _sg_gate(p, ln_g, ln_b, w_s, b_s_t):
    rows = p.shape[0]
    blk = lambda j: pl.BlockSpec((SG_CHUNK, SG_HALF), lambda i: (i, j))
    return pl.pallas_call(
        _sg_kernel,
        out_shape=jax.ShapeDtypeStruct((rows, SG_HALF), BF16),
        grid=(rows // SG_CHUNK,),
        in_specs=[blk(0), blk(1), blk(2),
                  pl.BlockSpec((1, SG_HALF), lambda i: (0, 0)),
                  pl.BlockSpec((1, SG_HALF), lambda i: (0, 0)),
                  pl.BlockSpec((SG_GROUPS, SG_CHUNK, SG_CHUNK), lambda i: (0, 0, 0)),
                  pl.BlockSpec((SG_CHUNK, SG_GROUPS), lambda i: (0, 0))],
        out_specs=blk(0),
        compiler_params=_cparams(("parallel",)),
        name="sg_gate",
    )(p, p, p, ln_g.reshape(1, SG_HALF), ln_b.reshape(1, SG_HALF), w_s, b_s_t)


def kernel(x, c, ctx, c_ctx, norm_g, ada_w, ada_b, na_w_in, na_rpb, na_w_out,
           ssd_w_in, ssd_conv_w, ssd_conv_b, ssd_dt_bias, ssd_a_log, ssd_d_skip,
           ssd_norm_g, ssd_w_out, sg_w_in, sg_ln_g, sg_ln_b, sg_w_s, sg_b_s, sg_w_out,
           final_norm_g):
    d = D_MODEL
    c_rows = jnp.concatenate([c, c_ctx[None], jnp.zeros((MOD_ROWS - BATCH - 1, d), F32)], axis=0)
    mods = _ada_all(c_rows, ada_w, ada_b)

    h, n = _norm_mod(ctx, x, norm_g[0], mods[0])

    na_w_out_b = _outproj_weights(na_w_out)
    ssd_w_out_b = _outproj_weights(ssd_w_out, ssd_norm_g)
    sg_w_out_b = _outproj_weights(sg_w_out)

    for i in range(DEPTH):
        kind, j = i % N_MIXERS, i // N_MIXERS
        need_ctx = i < DEPTH - 1
        if kind == 0:
            p = _proj(n, na_w_in, j, n=4 * d, out_dtype=BF16, scale_cols=d, scale=NA_HEAD_DIM ** -0.5 * LOG2E)
            y = _na_attention(p.reshape(BATCH, LTOT, 4 * d), na_rpb[j], need_ctx).reshape(ROWS, d)
            w_out = na_w_out_b
        elif kind == 1:
            p = _proj(n, ssd_w_in, j, n=SSD_MAIN, out_dtype=BF16).reshape(BATCH, LTOT, SSD_MAIN)
            dt_raw = _proj(n, ssd_w_in, j, n=2 * SSD_HEADS, out_dtype=F32, tn=2 * SSD_HEADS, col0=SSD_MAIN)
            xbc = _ssd_conv(p, ssd_conv_w[j], ssd_conv_b[j])
            y = _ssd_scan(xbc, p, dt_raw, ssd_dt_bias[j], ssd_a_log[j], ssd_d_skip[j]).reshape(ROWS, SSD_D_INNER)
            w_out = ssd_w_out_b
        else:
            p = _proj(n, sg_w_in, j, n=3 * SG_HALF, out_dtype=BF16, gelu_from_col=SG_HALF)
            y = _sg_gate(p, sg_ln_g[j], sg_ln_b[j], sg_w_s[j].astype(BF16), jnp.transpose(sg_b_s[j]))
            w_out = sg_w_out_b

        rms_a = kind == 1
        if need_ctx:
            h, n = _outproj(y, w_out, j, h, mods, i, norm_g[i + 1], mods[i + 1], final=False, rms_a=rms_a)
        else:
            out = _outproj(y, w_out, j, h, mods, i, final_norm_g, mods[i], final=True, rms_a=rms_a)

    return out.reshape(BATCH, LTOT, d)[:, CTX_LEN:]
```

```python
import functools
import math

import numpy as np
import jax
import jax.numpy as jnp
from jax import lax
from jax.experimental import pallas as pl
from jax.experimental.pallas import tpu as pltpu

D_MODEL = 2048
BATCH = 4
SEQ = 2048
DEPTH = 4
GRID_W = 64
CTX_LEN = 256
N_MIXERS = 3
EPS = 1e-6
LTOT = SEQ + CTX_LEN
ROWS = BATCH * LTOT

NA_HEADS = 16
NA_HEAD_DIM = D_MODEL // NA_HEADS
NA_WIN_ROWS = 8
NA_WIN_COLS = 16
NA_ROWS = SEQ // GRID_W
NA_QROWS = 4
NA_KROWS = NA_QROWS + NA_WIN_ROWS
NA_TQ = NA_QROWS * GRID_W
NA_TK = NA_KROWS * GRID_W
NA_TILES = NA_ROWS // NA_QROWS
NA_NDR = 2 * NA_WIN_ROWS - 1

SSD_D_INNER = 2 * D_MODEL
SSD_HEAD_DIM = 64
SSD_HEADS = SSD_D_INNER // SSD_HEAD_DIM
SSD_GROUPS = 8
SSD_STATE = 128
SSD_CHUNK = 128
SSD_GN = SSD_GROUPS * SSD_STATE
SSD_CONV_CH = SSD_D_INNER + 2 * SSD_GN
SSD_MAIN = SSD_D_INNER + SSD_CONV_CH
SSD_HPG = SSD_HEADS // SSD_GROUPS
SSD_GW = SSD_HPG * SSD_HEAD_DIM
SSD_NCHUNK = LTOT // SSD_CHUNK
SSD_CCHUNK = CTX_LEN // SSD_CHUNK
SSD_DL = 2 * SSD_HPG

SG_HALF = 3 * D_MODEL
SG_GROUPS = 16
SG_CHUNK = 128
SG_GW = SG_HALF // SG_GROUPS

NEG = -1e30
VMEM_LIMIT = 56 * 1024 * 1024
MOD_ROWS = 8

PROJ_TM, PROJ_TN = 1024, 1024
OUT_TM = 768
LOG2E = math.log2(math.e)

F32 = jnp.float32
BF16 = jnp.bfloat16
_NT = (((1,), (1,)), ((), ()))


def _cparams(sem):
    return pltpu.CompilerParams(dimension_semantics=sem, vmem_limit_bytes=VMEM_LIMIT)


def _silu(x):
    return x * (1.0 / (1.0 + jnp.exp2(x * -LOG2E)))


def _gelu_tanh(x):
    c = math.sqrt(2.0 / math.pi)
    return 0.5 * x * (1.0 + jnp.tanh(c * (x + 0.044715 * (x * x * x))))


def _softplus(x):
    return jnp.maximum(x, 0.0) + jnp.log(1.0 + jnp.exp(-jnp.abs(x)))


def _ada_kernel(c_ref, w_ref, b_ref, o_ref):
    a = _silu(c_ref[...]).astype(BF16)
    o_ref[0] = jnp.dot(a, w_ref[0].astype(BF16), preferred_element_type=F32) + b_ref[0]


def _ada_all(c_rows, ada_w, ada_b):
    tn = 1024
    return pl.pallas_call(
        _ada_kernel,
        out_shape=jax.ShapeDtypeStruct((DEPTH, MOD_ROWS, 3 * D_MODEL), F32),
        grid=(DEPTH, 3 * D_MODEL // tn),
        in_specs=[pl.BlockSpec((MOD_ROWS, D_MODEL), lambda l, j: (0, 0)),
                  pl.BlockSpec((1, D_MODEL, tn), lambda l, j: (l, 0, j)),
                  pl.BlockSpec((1, 1, tn), lambda l, j: (l, 0, j))],
        out_specs=pl.BlockSpec((1, MOD_ROWS, tn), lambda l, j: (l, 0, j)),
        compiler_params=_cparams(("arbitrary", "arbitrary")),
        name="ada_mod",
    )(c_rows, ada_w, ada_b.reshape(DEPTH, 1, 3 * D_MODEL))


def _mod_rows(mod_ref, row):
    m = mod_ref[pl.ds(row, 1), :]
    return m[:, :D_MODEL], m[:, D_MODEL:2 * D_MODEL], m[:, 2 * D_MODEL:]


def _norm_mod_kernel(ctx_ref, x_ref, g_ref, mod_ref, h_ref, o_ref):
    b = pl.program_id(0)
    r = pl.program_id(1)
    is_ctx = r == LTOT // CTX_LEN - 1
    shift, scale, _ = _mod_rows(mod_ref, jnp.where(is_ctx, BATCH, b))
    x = jnp.where(is_ctx, ctx_ref[0], x_ref[0])
    h_ref[...] = x
    y = x * lax.rsqrt(jnp.mean(x * x, axis=-1, keepdims=True) + EPS) * g_ref[...]
    o_ref[...] = (y * (1.0 + scale) + shift).astype(o_ref.dtype)


def _norm_mod(ctx, x, g, mod):
    per = LTOT // CTX_LEN
    rows = pl.BlockSpec((CTX_LEN, D_MODEL), lambda b, r: (b * per + r, 0))
    return pl.pallas_call(
        _norm_mod_kernel,
        out_shape=(jax.ShapeDtypeStruct((ROWS, D_MODEL), F32), jax.ShapeDtypeStruct((ROWS, D_MODEL), BF16)),
        grid=(BATCH, per),
        in_specs=[pl.BlockSpec((1, CTX_LEN, D_MODEL), lambda b, r: (b, 0, 0)),
                  pl.BlockSpec((1, CTX_LEN, D_MODEL), lambda b, r: (b, jnp.minimum(r, per - 2), 0)),
                  pl.BlockSpec((1, D_MODEL), lambda b, r: (0, 0)),
                  pl.BlockSpec((MOD_ROWS, 3 * D_MODEL), lambda b, r: (0, 0))],
        out_specs=(rows, rows),
        compiler_params=_cparams(("parallel", "parallel")),
        name="norm_mod",
    )(ctx, x, g.reshape(1, D_MODEL), mod)


def _proj_kernel(a_ref, w_ref, o_ref, wb_ref, *, gelu_from, scale_tiles, scale):
    j = pl.program_id(0)
    i = pl.program_id(1)

    @pl.when(i == 0)
    def _():
        w = w_ref[0]
        if scale_tiles:
            w = w * jnp.where(j < scale_tiles, scale, 1.0)
        wb_ref[...] = w.astype(BF16)

    acc = jnp.dot(a_ref[...], wb_ref[...], preferred_element_type=F32)
    if gelu_from is None:
        o_ref[...] = acc.astype(o_ref.dtype)
    else:
        @pl.when(j >= gelu_from)
        def _():
            o_ref[...] = _gelu_tanh(acc).astype(o_ref.dtype)

        @pl.when(j < gelu_from)
        def _():
            o_ref[...] = acc.astype(o_ref.dtype)


def _proj(a, w, layer, *, n, out_dtype, tn=PROJ_TN, col0=0, gelu_from_col=None, scale_cols=0, scale=1.0):
    m, kdim = a.shape
    off = col0 // tn
    return pl.pallas_call(
        functools.partial(_proj_kernel,
                          gelu_from=None if gelu_from_col is None else gelu_from_col // tn,
                          scale_tiles=scale_cols // tn, scale=scale),
        out_shape=jax.ShapeDtypeStruct((m, n), out_dtype),
        grid=(n // tn, m // PROJ_TM),
        in_specs=[pl.BlockSpec((PROJ_TM, kdim), lambda j, i: (i, 0)),
                  pl.BlockSpec((1, kdim, tn), lambda j, i: (layer, 0, off + j))],
        out_specs=pl.BlockSpec((PROJ_TM, tn), lambda j, i: (i, j)),
        scratch_shapes=[pltpu.VMEM((kdim, tn), BF16)],
        compiler_params=_cparams(("parallel", "arbitrary")),
        name="proj",
    )(a, w)


def _outproj_kernel(a_ref, w_ref, h_ref, gate_ref, ng_ref, nmod_ref, *rest, nj, tn, final, resident, rms_a):
    if final:
        n_ref, hs_ref, ss_ref, ra_ref = rest
        hn_ref = None
    else:
        hn_ref, n_ref, hs_ref, ss_ref, ra_ref = rest
    i = pl.program_id(0)
    j = pl.program_id(1)

    if rms_a:
        @pl.when(j == 0)
        def _():
            kdim = a_ref.shape[1]
            ssq = jnp.zeros((OUT_TM, 1), F32)
            for c0 in range(0, kdim, 512):
                af = a_ref[:, c0:c0 + 512].astype(F32)
                ssq = ssq + jnp.sum(af * af, axis=-1, keepdims=True)
            ra_ref[...] = lax.rsqrt(ssq * (1.0 / kdim) + EPS)
    per = LTOT // OUT_TM
    b = i // per
    last = i % per == per - 1
    top = slice(0, OUT_TM - CTX_LEN)
    bot = slice(OUT_TM - CTX_LEN, OUT_TM)
    if final:
        n_ref = n_ref.at[0]

    def rows(ref, cols):
        lat = ref[pl.ds(b, 1), cols]
        return lat, jnp.where(last, ref[BATCH:BATCH + 1, cols], lat)

    w = w_ref[0, j] if resident else w_ref[0, 0]
    acc = jnp.dot(a_ref[...], w, preferred_element_type=F32)
    if rms_a:
        acc = acc * ra_ref[...]
    g_top, g_bot = rows(gate_ref.at[0], slice(None))
    hn_t = h_ref[top, :] + g_top * acc[top, :]
    hn_b = h_ref[bot, :] + g_bot * acc[bot, :]
    hs_ref[j, top, :] = hn_t
    hs_ref[j, bot, :] = hn_b
    if not final:
        hn_ref[top, :] = hn_t
        hn_ref[bot, :] = hn_b
    sq_t = jnp.sum(hn_t * hn_t, axis=-1, keepdims=True)
    sq_b = jnp.sum(hn_b * hn_b, axis=-1, keepdims=True)

    @pl.when(j == 0)
    def _():
        ss_ref[top, :] = sq_t
        ss_ref[bot, :] = sq_b

    @pl.when(j > 0)
    def _():
        ss_ref[top, :] += sq_t
        ss_ref[bot, :] += sq_b

    @pl.when(j == nj - 1)
    def _():
        inv = lax.rsqrt(ss_ref[...] * (1.0 / D_MODEL) + EPS)
        for jj in range(nj):
            cs = slice(jj * tn, (jj + 1) * tn)
            y = hs_ref[jj] * inv * ng_ref[:, cs]
            if final:
                n_ref[:, cs] = y
            else:
                sh_top, sh_bot = rows(nmod_ref, cs)
                sc_top, sc_bot = rows(nmod_ref, slice(D_MODEL + jj * tn, D_MODEL + (jj + 1) * tn))
                n_ref[top, cs] = (y[top, :] * (1.0 + sc_top) + sh_top).astype(n_ref.dtype)
                n_ref[bot, cs] = (y[bot, :] * (1.0 + sc_bot) + sh_bot).astype(n_ref.dtype)


def _outproj_tn(kdim):
    return 512 if kdim <= 2 * D_MODEL else 256


def _wtile_kernel(w_ref, *rest):
    if len(rest) == 2:
        g_ref, o_ref = rest
        o_ref[0, 0] = (w_ref[0] * g_ref[0]).astype(o_ref.dtype)
    else:
        o_ref, = rest
        o_ref[0, 0] = w_ref[0].astype(o_ref.dtype)


def _outproj_weights(w, row_gain=None):
    nl, kdim, _ = w.shape
    tn = _outproj_tn(kdim)
    in_specs = [pl.BlockSpec((1, kdim, tn), lambda l, j: (l, 0, j))]
    args = [w]
    if row_gain is not None:
        in_specs.append(pl.BlockSpec((1, kdim, 1), lambda l, j: (l, 0, 0)))
        args.append(row_gain.reshape(nl, kdim, 1))
    return pl.pallas_call(
        _wtile_kernel,
        out_shape=jax.ShapeDtypeStruct((nl, D_MODEL // tn, kdim, tn), BF16),
        grid=(nl, D_MODEL // tn),
        in_specs=in_specs,
        out_specs=pl.BlockSpec((1, 1, kdim, tn), lambda l, j: (l, j, 0, 0)),
        compiler_params=_cparams(("parallel", "parallel")),
        name="wtile",
    )(*args)


def _outproj(a, w, layer, h, mods, mod_layer, next_g, next_mod, *, final, rms_a=False):
    m, kdim = a.shape
    tn = _outproj_tn(kdim)
    nj = D_MODEL // tn
    gate_off = 2 * D_MODEL // tn
    resident = kdim <= 2 * D_MODEL
    if resident:
        w_spec = pl.BlockSpec((1, nj, kdim, tn), lambda i, j: (layer, 0, 0, 0), pipeline_mode=pl.Buffered(1))
    else:
        w_spec = pl.BlockSpec((1, 1, kdim, tn), lambda i, j: (layer, j, 0, 0))
    tile = pl.BlockSpec((OUT_TM, tn), lambda i, j: (i, j))
    full = pl.BlockSpec((OUT_TM, D_MODEL), lambda i, j: (i, 0))
    const = lambda i, j: (0, 0)
    if final:
        per = LTOT // OUT_TM
        out_shape = jax.ShapeDtypeStruct((BATCH, SEQ, D_MODEL), F32)
        out_specs = pl.BlockSpec((1, OUT_TM, D_MODEL), lambda i, j: (i // per, i % per, 0))
    else:
        out_shape = (jax.ShapeDtypeStruct((m, D_MODEL), F32), jax.ShapeDtypeStruct((m, D_MODEL), BF16))
        out_specs = (tile, full)
    return pl.pallas_call(
        functools.partial(_outproj_kernel, nj=nj, tn=tn, final=final, resident=resident, rms_a=rms_a),
        out_shape=out_shape,
        grid=(m // OUT_TM, nj),
        in_specs=[pl.BlockSpec((OUT_TM, kdim), lambda i, j: (i, 0)),
                  w_spec,
                  tile,
                  pl.BlockSpec((1, MOD_ROWS, tn), lambda i, j: (mod_layer, 0, gate_off + j)),
                  pl.BlockSpec((1, D_MODEL), const),
                  pl.BlockSpec((MOD_ROWS, 3 * D_MODEL), const)],
        out_specs=out_specs,
        scratch_shapes=[pltpu.VMEM((nj, OUT_TM, tn), F32),
                        pltpu.VMEM((OUT_TM, 1), F32),
                        pltpu.VMEM((OUT_TM, 1), F32)],
        compiler_params=_cparams(("parallel", "arbitrary")),
        name="outproj",
    )(a, w, h, mods, next_g.reshape(1, D_MODEL), next_mod)


def _na_tile_patterns():
    pats = []
    for t in (0, 1, NA_TILES - 1):
        ws = min(max(NA_QROWS * t - NA_WIN_ROWS // 2, 0), NA_ROWS - NA_KROWS)
        pat = np.full((NA_QROWS, NA_KROWS), NA_NDR, dtype=np.int64)
        for a in range(NA_QROWS):
            r = NA_QROWS * t + a
            rs = min(max(r - NA_WIN_ROWS // 2, 0), NA_ROWS - NA_WIN_ROWS)
            for jj in range(NA_KROWS):
                kabs = ws + jj
                if rs <= kabs < rs + NA_WIN_ROWS:
                    pat[a, jj] = kabs - r + NA_WIN_ROWS - 1
        pats.append(pat)
    return pats


def _na_build_bias(rpb_ref, slab_ref, bias_ref):
    w2 = 2 * GRID_W
    qc = lax.broadcasted_iota(jnp.int32, (GRID_W, w2), 0)
    lane = lax.broadcasted_iota(jnp.int32, (GRID_W, w2), 1)
    left = lane < GRID_W
    kc = jnp.where(left, lane, lane - GRID_W)
    c_start = jnp.clip(qc - NA_WIN_COLS // 2, 0, GRID_W - NA_WIN_COLS)
    col_ok = jnp.logical_and(kc >= c_start, kc < c_start + NA_WIN_COLS)
    shift = w2 - (NA_WIN_COLS - 1)
    for dr in range(NA_NDR):
        row = jnp.broadcast_to(rpb_ref[0, dr:dr + 1, :] * LOG2E, (GRID_W, w2))
        lo = pltpu.roll(row, shift, 1, stride=1, stride_axis=0)
        hi = pltpu.roll(row, (shift + GRID_W) % w2, 1, stride=1, stride_axis=0)
        slab_ref[dr] = jnp.where(col_ok, jnp.where(left, lo, hi), NEG)
    slab_ref[NA_NDR] = jnp.full((GRID_W, w2), NEG, F32)
    for p, pat in enumerate(_na_tile_patterns()):
        for a in range(NA_QROWS):
            for jp in range(NA_KROWS // 2):
                blk = jnp.where(left, slab_ref[int(pat[a, 2 * jp])], slab_ref[int(pat[a, 2 * jp + 1])])
                bias_ref[p, a * GRID_W:(a + 1) * GRID_W, jp * w2:(jp + 1) * w2] = blk


def _na_kernel(q_ref, k_ref, v_ref, z_ref, rpb_ref, o_ref, bias_ref, slab_ref, *, need_ctx):
    @pl.when(pl.program_id(1) == 0)
    def _():
        _na_build_bias(rpb_ref, slab_ref, bias_ref)

    kc = k_ref[0, SEQ:LTOT, :]
    vc = v_ref[0, SEQ:LTOT, :]

    def finish(o, l, r0):
        z = z_ref[0, pl.ds(r0, NA_TQ), :].astype(F32)
        o_ref[0, pl.ds(r0, NA_TQ), :] = (o * (1.0 / l) * _silu(z)).astype(o_ref.dtype)

    if need_ctx:
        s = lax.dot_general(q_ref[0, SEQ:LTOT, :], kc, _NT, preferred_element_type=F32)
        p = jnp.exp2(s - jnp.max(s, axis=-1, keepdims=True))
        finish(jnp.dot(p.astype(BF16), vc, preferred_element_type=F32), jnp.sum(p, axis=-1, keepdims=True), SEQ)
    else:
        o_ref[0, SEQ:LTOT, :] = jnp.zeros((CTX_LEN, NA_HEAD_DIM), o_ref.dtype)

    def key_start(t):
        ws = jnp.clip(NA_QROWS * t - NA_WIN_ROWS // 2, 0, NA_ROWS - NA_KROWS)
        return pl.multiple_of(ws * GRID_W, GRID_W)

    def scores(t):
        q = q_ref[0, pl.ds(pl.multiple_of(t * NA_TQ, NA_TQ), NA_TQ), :]
        pat = jnp.where(t == 0, 0, jnp.where(t == NA_TILES - 1, 2, 1))
        s1 = lax.dot_general(q, k_ref[0, pl.ds(key_start(t), NA_TK), :], _NT,
                             preferred_element_type=F32) + bias_ref[pat]
        return s1, lax.dot_general(q, kc, _NT, preferred_element_type=F32)

    def tile(t, carry):
        s1, s2 = carry
        nxt = scores(jnp.minimum(t + 1, NA_TILES - 1))
        m = jnp.maximum(jnp.max(s1, axis=-1, keepdims=True), jnp.max(s2, axis=-1, keepdims=True))
        p1 = jnp.exp2(s1 - m)
        p2 = jnp.exp2(s2 - m)
        l = jnp.sum(p1, axis=-1, keepdims=True) + jnp.sum(p2, axis=-1, keepdims=True)
        vw = v_ref[0, pl.ds(key_start(t), NA_TK), :]
        o = (jnp.dot(p1.astype(BF16), vw, preferred_element_type=F32)
             + jnp.dot(p2.astype(BF16), vc, preferred_element_type=F32))
        finish(o, l, pl.multiple_of(t * NA_TQ, NA_TQ))
        return nxt

    lax.fori_loop(0, NA_TILES, tile, scores(0), unroll=2)


def _na_attention(p, rpb, need_ctx):
    hd = NA_HEAD_DIM
    w2 = 2 * GRID_W
    rpb_pad = jnp.pad(rpb, ((0, 0), (0, NA_NDR + 1 - rpb.shape[1]), (0, w2 - rpb.shape[2])))
    blk = lambda off: pl.BlockSpec((1, LTOT, hd), lambda h, b: (b, 0, off + h))
    return pl.pallas_call(
        functools.partial(_na_kernel, need_ctx=need_ctx),
        out_shape=jax.ShapeDtypeStruct((BATCH, LTOT, D_MODEL), BF16),
        grid=(NA_HEADS, BATCH),
        in_specs=[blk(0), blk(NA_HEADS), blk(2 * NA_HEADS), blk(3 * NA_HEADS),
                  pl.BlockSpec((1, NA_NDR + 1, w2), lambda h, b: (h, 0, 0))],
        out_specs=blk(0),
        scratch_shapes=[pltpu.VMEM((3, NA_TQ, NA_TK), F32),
                        pltpu.VMEM((NA_NDR + 1, GRID_W, w2), F32)],
        compiler_params=_cparams(("parallel", "arbitrary")),
        name="na_attention",
    )(p, p, p, p, rpb_pad)


def _conv_kernel(u_ref, w_ref, b_ref, o_ref):
    u = u_ref[0].astype(F32)
    row = lax.broadcasted_iota(jnp.int32, u.shape, 0)
    seg_first = jnp.logical_or(row == 0, row == SEQ)
    seg_last = jnp.logical_or(row == SEQ - 1, row == LTOT - 1)
    up = jnp.where(seg_first, 0.0, pltpu.roll(u, 1, 0))
    un = jnp.where(seg_last, 0.0, pltpu.roll(u, LTOT - 1, 0))
    w = w_ref[...]
    y = w[0:1] * up + w[1:2] * u + w[2:3] * un + b_ref[...]
    o_ref[0] = _silu(y).astype(o_ref.dtype)


def _ssd_conv(p, conv_w, conv_b):
    tc = 512
    off = SSD_D_INNER // tc
    return pl.pallas_call(
        _conv_kernel,
        out_shape=jax.ShapeDtypeStruct((BATCH, LTOT, SSD_CONV_CH), BF16),
        grid=(BATCH, SSD_CONV_CH // tc),
        in_specs=[pl.BlockSpec((1, LTOT, tc), lambda b, j: (b, 0, off + j)),
                  pl.BlockSpec((3, tc), lambda b, j: (0, j)),
                  pl.BlockSpec((1, tc), lambda b, j: (0, j))],
        out_specs=pl.BlockSpec((1, LTOT, tc), lambda b, j: (b, 0, j)),
        compiler_params=_cparams(("parallel", "parallel")),
        name="ssd_conv",
    )(p, conv_w, conv_b.reshape(1, SSD_CONV_CH))


def _bf16_pieces(v):
    hi = v.astype(BF16)
    r1 = v - hi.astype(F32)
    mid = r1.astype(BF16)
    lo = (r1 - mid.astype(F32)).astype(BF16)
    return hi, mid, lo


def _split3(v):
    hi, mid, lo = _bf16_pieces(v)
    lane = lax.broadcasted_iota(jnp.int32, v.shape, 1)
    return jnp.where(lane < SSD_DL, hi, jnp.where(lane < 2 * SSD_DL, mid, lo))


def _ssd_kernel(x_ref, b_ref, c_ref, z_ref, dtc_ref, dtr_ref, pc_ref, pr_ref, dskip_ref,
                y_ref, yacc_ref, s_ref):
    q = SSD_CHUNK
    hp = SSD_HPG
    li = lax.broadcasted_iota(jnp.int32, (q, q), 0)
    si = lax.broadcasted_iota(jnp.int32, (q, q), 1)
    lower = li >= si
    upper = li <= si
    tri_lo = lower.astype(BF16)
    tri_up = upper.astype(BF16)
    tri_lanes = (jnp.concatenate([tri_lo] * 3, axis=1), jnp.concatenate([tri_up] * 3, axis=1))
    tri_rows = (jnp.concatenate([tri_up] * 3, axis=0), jnp.concatenate([tri_lo] * 3, axis=0))
    left = si < SSD_HEAD_DIM

    def expand_matrix(d):
        r = lax.broadcasted_iota(jnp.int32, (3 * SSD_DL, SSD_GW), 0) % SSD_DL
        c = lax.broadcasted_iota(jnp.int32, (3 * SSD_DL, SSD_GW), 1) // SSD_HEAD_DIM
        return (r == d * hp + c).astype(BF16)

    expand = (expand_matrix(0), expand_matrix(1))

    bias_c = pc_ref[0, 0:1, :]
    a_c = -jnp.exp(pc_ref[0, 1:2, :]) * LOG2E
    bias_r = pr_ref[0, :, 0:1]
    a_r = -jnp.exp(pr_ref[0, :, 1:2]) * LOG2E

    s_ref[...] = jnp.zeros_like(s_ref)

    def prep(c, d):
        r0 = pl.multiple_of(c * q, q)
        ex = expand[d]
        dtc = _softplus(dtc_ref[0, 0, pl.ds(r0, q), :] + bias_c)
        acum_c = jnp.dot(tri_lanes[d], jnp.concatenate(_bf16_pieces(dtc * a_c), axis=0),
                         preferred_element_type=F32)
        dtr = _softplus(dtr_ref[0, 0, c] + bias_r)
        acum_r = jnp.dot(jnp.concatenate(_bf16_pieces(dtr * a_r), axis=1), tri_rows[d],
                         preferred_element_type=F32)
        tot_c = acum_c[q - 1:q, :] if d == 0 else acum_c[0:1, :]
        ea_c = jnp.exp2(acum_c)
        dw_c = dtc * jnp.exp2(tot_c - acum_c)
        ea_x = jnp.dot(_split3(ea_c), ex, preferred_element_type=F32)
        dw_x = jnp.dot(_split3(dw_c), ex, preferred_element_type=F32)
        return ea_x, dw_x, acum_c, acum_r, dtr

    def chunk(c, d, prepared):
        ea_x, dw_x, acum_c, acum_r, dtr = prepared
        r0 = pl.multiple_of(c * q, q)
        mask = lower if d == 0 else upper
        xb = x_ref[0, pl.ds(r0, q), :]
        bm = b_ref[0, pl.ds(r0, q), :]
        cm = c_ref[0, pl.ds(r0, q), :]
        etot_x = ea_x[q - 1:q, :] if d == 0 else ea_x[0:1, :]
        cb = jnp.where(mask, lax.dot_general(cm, bm, _NT, preferred_element_type=F32), 0.0)
        bt = bm.astype(F32).T.astype(BF16)
        sprev = s_ref[d]
        yoff = jnp.dot(cm, sprev.astype(BF16), preferred_element_type=F32)
        s_ref[d] = sprev * etot_x + jnp.dot(bt, (xb.astype(F32) * dw_x).astype(BF16),
                                            preferred_element_type=F32)
        zero = jnp.zeros((q, q), BF16)
        for p in range(hp // 2):
            cs = slice(p * q, (p + 1) * q)
            m_pair = []
            for ln in (d * hp + 2 * p, d * hp + 2 * p + 1):
                diff = jnp.minimum(acum_c[:, ln:ln + 1] - acum_r[ln:ln + 1, :], 0.0)
                m_pair.append((cb * jnp.exp2(diff) * dtr[ln:ln + 1, :]).astype(BF16))
            x_p = xb[:, cs]
            rhs = jnp.concatenate([jnp.where(left, x_p, zero), jnp.where(left, zero, x_p)], axis=0)
            yd = jnp.dot(jnp.concatenate(m_pair, axis=1), rhs, preferred_element_type=F32)
            yacc_ref[d, pl.ds(r0, q), cs] = yd + ea_x[:, cs] * yoff[:, cs]

    lat_chunks = SSD_NCHUNK - SSD_CCHUNK

    def fwd_chunk(k):
        return jnp.where(k < SSD_CCHUNK, lat_chunks + k, k - SSD_CCHUNK)

    def bwd_chunk(k):
        return SSD_NCHUNK - 1 - k

    def step(k, carry):
        kn = jnp.minimum(k + 1, SSD_NCHUNK - 1)
        nxt = (prep(fwd_chunk(kn), 0), prep(bwd_chunk(kn), 1))
        chunk(fwd_chunk(k), 0, carry[0])
        chunk(bwd_chunk(k), 1, carry[1])
        return nxt

    lax.fori_loop(0, SSD_NCHUNK, step, (prep(fwd_chunk(0), 0), prep(bwd_chunk(0), 1)))

    y = yacc_ref[0] + yacc_ref[1] + x_ref[0].astype(F32) * dskip_ref[0]
    y_ref[0] = (y * _silu(z_ref[0].astype(F32))).astype(y_ref.dtype)


def _ssd_scan(xbc, p, dt_raw, dt_bias, a_log, d_skip):
    g, hp, dl = SSD_GROUPS, SSD_HPG, SSD_DL
    dt = dt_raw.reshape(BATCH, LTOT, 2, g, hp).transpose(0, 3, 1, 2, 4).reshape(BATCH, g, LTOT, dl)
    dt_col = jnp.tile(dt, (1, 1, 1, 3))
    dt_row = dt.reshape(BATCH, g, SSD_NCHUNK, SSD_CHUNK, dl).transpose(0, 1, 2, 4, 3)
    par = jnp.stack([dt_bias, a_log]).reshape(2, 2, g, hp).transpose(2, 0, 1, 3).reshape(g, 2, dl)
    par_col = jnp.pad(jnp.tile(par, (1, 1, 3)), ((0, 0), (0, 6), (0, 0)))
    par_row = jnp.pad(par.transpose(0, 2, 1), ((0, 0), (0, 0), (0, 126)))
    dskip = jnp.repeat(d_skip, SSD_HEAD_DIM).reshape(g, 1, SSD_GW)
    xoff = SSD_D_INNER // SSD_STATE
    return pl.pallas_call(
        _ssd_kernel,
        out_shape=jax.ShapeDtypeStruct((BATCH, LTOT, SSD_D_INNER), BF16),
        grid=(BATCH, g),
        in_specs=[pl.BlockSpec((1, LTOT, SSD_GW), lambda b, j: (b, 0, j)),
                  pl.BlockSpec((1, LTOT, SSD_STATE), lambda b, j: (b, 0, xoff + j)),
                  pl.BlockSpec((1, LTOT, SSD_STATE), lambda b, j: (b, 0, xoff + g + j)),
                  pl.BlockSpec((1, LTOT, SSD_GW), lambda b, j: (b, 0, j)),
                  pl.BlockSpec((1, 1, LTOT, 3 * dl), lambda b, j: (b, j, 0, 0)),
                  pl.BlockSpec((1, 1, SSD_NCHUNK, dl, SSD_CHUNK), lambda b, j: (b, j, 0, 0, 0)),
                  pl.BlockSpec((1, 8, 3 * dl), lambda b, j: (j, 0, 0)),
                  pl.BlockSpec((1, dl, 128), lambda b, j: (j, 0, 0)),
                  pl.BlockSpec((1, 1, SSD_GW), lambda b, j: (j, 0, 0))],
        out_specs=pl.BlockSpec((1, LTOT, SSD_GW), lambda b, j: (b, 0, j)),
        scratch_shapes=[pltpu.VMEM((2, LTOT, SSD_GW), F32),
                        pltpu.VMEM((2, SSD_STATE, SSD_GW), F32)],
        compiler_params=_cparams(("parallel", "parallel")),
        name="ssd_scan",
    )(xbc, xbc, xbc, p, dt_col, dt_row, par_col, par_row, dskip)


def _sg_kernel(z_ref, u_ref, v_ref, g_ref, b_ref, ws_ref, bs_ref, o_ref):
    v = v_ref[...].astype(F32)
    mu = jnp.mean(v, axis=-1, keepdims=True)
    vc = v - mu
    var = jnp.mean(vc * vc, axis=-1, keepdims=True)
    vn = (vc * lax.rsqrt(var + EPS) * g_ref[...] + b_ref[...]).astype(BF16)
    for g in range(SG_GROUPS):
        cs = slice(g * SG_GW, (g + 1) * SG_GW)
        sv = jnp.dot(ws_ref[g], vn[:, cs], preferred_element_type=F32) + bs_ref[:, g:g + 1]
        o_ref[:, cs] = (u_ref[:, cs].astype(F32) * sv * _silu(z_ref[:, cs].astype(F32))).astype(o_ref.dtype)


def _sg_gate(p, ln_g, ln_b, w_s, b_s_t):
    rows = p.shape[0]
    blk = lambda j: pl.BlockSpec((SG_CHUNK, SG_HALF), lambda i: (i, j))
    return pl.pallas_call(
        _sg_kernel,
        out_shape=jax.ShapeDtypeStruct((rows, SG_HALF), BF16),
        grid=(rows // SG_CHUNK,),
        in_specs=[blk(0), blk(1), blk(2),
                  pl.BlockSpec((1, SG_HALF), lambda i: (0, 0)),
                  pl.BlockSpec((1, SG_HALF), lambda i: (0, 0)),
                  pl.BlockSpec((SG_GROUPS, SG_CHUNK, SG_CHUNK), lambda i: (0, 0, 0)),
                  pl.BlockSpec((SG_CHUNK, SG_GROUPS), lambda i: (0, 0))],
        out_specs=blk(0),
        compiler_params=_cparams(("parallel",)),
        name="sg_gate",
    )(p, p, p, ln_g.reshape(1, SG_HALF), ln_b.reshape(1, SG_HALF), w_s, b_s_t)


def kernel(x, c, ctx, c_ctx, norm_g, ada_w, ada_b, na_w_in, na_rpb, na_w_out,
           ssd_w_in, ssd_conv_w, ssd_conv_b, ssd_dt_bias, ssd_a_log, ssd_d_skip,
           ssd_norm_g, ssd_w_out, sg_w_in, sg_ln_g, sg_ln_b, sg_w_s, sg_b_s, sg_w_out,
           final_norm_g):
    d = D_MODEL
    c_rows = jnp.concatenate([c, c_ctx[None], jnp.zeros((MOD_ROWS - BATCH - 1, d), F32)], axis=0)
    mods = _ada_all(c_rows, ada_w, ada_b)

    h, n = _norm_mod(ctx, x, norm_g[0], mods[0])

    na_w_out_b = _outproj_weights(na_w_out)
    ssd_w_out_b = _outproj_weights(ssd_w_out, ssd_norm_g)
    sg_w_out_b = _outproj_weights(sg_w_out)

    for i in range(DEPTH):
        kind, j = i % N_MIXERS, i // N_MIXERS
        need_ctx = i < DEPTH - 1
        if kind == 0:
            p = _proj(n, na_w_in, j, n=4 * d, out_dtype=BF16, scale_cols=d, scale=NA_HEAD_DIM ** -0.5 * LOG2E)
            y = _na_attention(p.reshape(BATCH, LTOT, 4 * d), na_rpb[j], need_ctx).reshape(ROWS, d)
            w_out = na_w_out_b
        elif kind == 1:
            p = _proj(n, ssd_w_in, j, n=SSD_MAIN, out_dtype=BF16).reshape(BATCH, LTOT, SSD_MAIN)
            dt_raw = _proj(n, ssd_w_in, j, n=2 * SSD_HEADS, out_dtype=F32, tn=2 * SSD_HEADS, col0=SSD_MAIN)
            xbc = _ssd_conv(p, ssd_conv_w[j], ssd_conv_b[j])
            y = _ssd_scan(xbc, p, dt_raw, ssd_dt_bias[j], ssd_a_log[j], ssd_d_skip[j]).reshape(ROWS, SSD_D_INNER)
            w_out = ssd_w_out_b
        else:
            p = _proj(n, sg_w_in, j, n=3 * SG_HALF, out_dtype=BF16, gelu_from_col=SG_HALF)
            y = _sg_gate(p, sg_ln_g[j], sg_ln_b[j], sg_w_s[j].astype(BF16), jnp.transpose(sg_b_s[j]))
            w_out = sg_w_out_b

        rms_a = kind == 1
        if need_ctx:
            h, n = _outproj(y, w_out, j, h, mods, i, norm_g[i + 1], mods[i + 1], final=False, rms_a=rms_a)
        else:
            out = _outproj(y, w_out, j, h, mods, i, final_norm_g, mods[i], final=True, rms_a=rms_a)

    return out
```

```python
import functools
import math

import numpy as np
import jax
import jax.numpy as jnp
from jax import lax
from jax.experimental import pallas as pl
from jax.experimental.pallas import tpu as pltpu

D_MODEL = 2048
BATCH = 4
SEQ = 2048
DEPTH = 4
GRID_W = 64
CTX_LEN = 256
N_MIXERS = 3
EPS = 1e-6
LTOT = SEQ + CTX_LEN
ROWS = BATCH * LTOT

NA_HEADS = 16
NA_HEAD_DIM = D_MODEL // NA_HEADS
NA_WIN_ROWS = 8
NA_WIN_COLS = 16
NA_ROWS = SEQ // GRID_W
NA_QROWS = 4
NA_KROWS = NA_QROWS + NA_WIN_ROWS
NA_TQ = NA_QROWS * GRID_W
NA_TK = NA_KROWS * GRID_W
NA_TILES = NA_ROWS // NA_QROWS
NA_NDR = 2 * NA_WIN_ROWS - 1

SSD_D_INNER = 2 * D_MODEL
SSD_HEAD_DIM = 64
SSD_HEADS = SSD_D_INNER // SSD_HEAD_DIM
SSD_GROUPS = 8
SSD_STATE = 128
SSD_CHUNK = 128
SSD_GN = SSD_GROUPS * SSD_STATE
SSD_CONV_CH = SSD_D_INNER + 2 * SSD_GN
SSD_MAIN = SSD_D_INNER + SSD_CONV_CH
SSD_HPG = SSD_HEADS // SSD_GROUPS
SSD_GW = SSD_HPG * SSD_HEAD_DIM
SSD_NCHUNK = LTOT // SSD_CHUNK
SSD_CCHUNK = CTX_LEN // SSD_CHUNK
SSD_DL = 2 * SSD_HPG

SG_HALF = 3 * D_MODEL
SG_GROUPS = 16
SG_CHUNK = 128
SG_GW = SG_HALF // SG_GROUPS

NEG = -1e30
VMEM_LIMIT = 56 * 1024 * 1024
MOD_ROWS = 8

PROJ_TM, PROJ_TN = 1536, 1024
OUT_TM = 768
LOG2E = math.log2(math.e)

F32 = jnp.float32
BF16 = jnp.bfloat16
_NT = (((1,), (1,)), ((), ()))


def _cparams(sem):
    return pltpu.CompilerParams(dimension_semantics=sem, vmem_limit_bytes=VMEM_LIMIT)


def _silu(x):
    return x * (1.0 / (1.0 + jnp.exp2(x * -LOG2E)))


def _gelu_tanh(x):
    c = math.sqrt(2.0 / math.pi)
    return 0.5 * x * (1.0 + jnp.tanh(c * (x + 0.044715 * (x * x * x))))


def _softplus(x):
    return jnp.maximum(x, 0.0) + jnp.log(1.0 + jnp.exp(-jnp.abs(x)))


def _ada_kernel(c_ref, w_ref, b_ref, o_ref):
    a = _silu(c_ref[...]).astype(BF16)
    o_ref[0] = jnp.dot(a, w_ref[0].astype(BF16), preferred_element_type=F32) + b_ref[0]


def _ada_all(c_rows, ada_w, ada_b):
    tn = 1024
    return pl.pallas_call(
        _ada_kernel,
        out_shape=jax.ShapeDtypeStruct((DEPTH, MOD_ROWS, 3 * D_MODEL), F32),
        grid=(DEPTH, 3 * D_MODEL // tn),
        in_specs=[pl.BlockSpec((MOD_ROWS, D_MODEL), lambda l, j: (0, 0)),
                  pl.BlockSpec((1, D_MODEL, tn), lambda l, j: (l, 0, j)),
                  pl.BlockSpec((1, 1, tn), lambda l, j: (l, 0, j))],
        out_specs=pl.BlockSpec((1, MOD_ROWS, tn), lambda l, j: (l, 0, j)),
        compiler_params=_cparams(("arbitrary", "arbitrary")),
        name="ada_mod",
    )(c_rows, ada_w, ada_b.reshape(DEPTH, 1, 3 * D_MODEL))


def _mod_rows(mod_ref, row):
    m = mod_ref[pl.ds(row, 1), :]
    return m[:, :D_MODEL], m[:, D_MODEL:2 * D_MODEL], m[:, 2 * D_MODEL:]


def _norm_mod_kernel(ctx_ref, x_ref, g_ref, mod_ref, h_ref, o_ref):
    b = pl.program_id(0)
    r = pl.program_id(1)
    is_ctx = r == LTOT // CTX_LEN - 1
    shift, scale, _ = _mod_rows(mod_ref, jnp.where(is_ctx, BATCH, b))
    x = jnp.where(is_ctx, ctx_ref[0], x_ref[0])
    h_ref[...] = x
    y = x * lax.rsqrt(jnp.mean(x * x, axis=-1, keepdims=True) + EPS) * g_ref[...]
    o_ref[...] = (y * (1.0 + scale) + shift).astype(o_ref.dtype)


def _norm_mod(ctx, x, g, mod):
    per = LTOT // CTX_LEN
    rows = pl.BlockSpec((CTX_LEN, D_MODEL), lambda b, r: (b * per + r, 0))
    return pl.pallas_call(
        _norm_mod_kernel,
        out_shape=(jax.ShapeDtypeStruct((ROWS, D_MODEL), F32), jax.ShapeDtypeStruct((ROWS, D_MODEL), BF16)),
        grid=(BATCH, per),
        in_specs=[pl.BlockSpec((1, CTX_LEN, D_MODEL), lambda b, r: (b, 0, 0)),
                  pl.BlockSpec((1, CTX_LEN, D_MODEL), lambda b, r: (b, jnp.minimum(r, per - 2), 0)),
                  pl.BlockSpec((1, D_MODEL), lambda b, r: (0, 0)),
                  pl.BlockSpec((MOD_ROWS, 3 * D_MODEL), lambda b, r: (0, 0))],
        out_specs=(rows, rows),
        compiler_params=_cparams(("parallel", "parallel")),
        name="norm_mod",
    )(ctx, x, g.reshape(1, D_MODEL), mod)


def _proj_kernel(a_ref, w_ref, o_ref, wb_ref, *, gelu_from, scale_tiles, scale):
    j = pl.program_id(0)
    i = pl.program_id(1)

    @pl.when(i == 0)
    def _():
        w = w_ref[0]
        if scale_tiles:
            w = w * jnp.where(j < scale_tiles, scale, 1.0)
        wb_ref[...] = w.astype(BF16)

    acc = jnp.dot(a_ref[...], wb_ref[...], preferred_element_type=F32)
    if gelu_from is None:
        o_ref[...] = acc.astype(o_ref.dtype)
    else:
        @pl.when(j >= gelu_from)
        def _():
            o_ref[...] = _gelu_tanh(acc).astype(o_ref.dtype)

        @pl.when(j < gelu_from)
        def _():
            o_ref[...] = acc.astype(o_ref.dtype)


def _proj(a, w, layer, *, n, out_dtype, tn=PROJ_TN, col0=0, gelu_from_col=None, scale_cols=0, scale=1.0):
    m, kdim = a.shape
    off = col0 // tn
    return pl.pallas_call(
        functools.partial(_proj_kernel,
                          gelu_from=None if gelu_from_col is None else gelu_from_col // tn,
                          scale_tiles=scale_cols // tn, scale=scale),
        out_shape=jax.ShapeDtypeStruct((m, n), out_dtype),
        grid=(n // tn, m // PROJ_TM),
        in_specs=[pl.BlockSpec((PROJ_TM, kdim), lambda j, i: (i, 0)),
                  pl.BlockSpec((1, kdim, tn), lambda j, i: (layer, 0, off + j))],
        out_specs=pl.BlockSpec((PROJ_TM, tn), lambda j, i: (i, j)),
        scratch_shapes=[pltpu.VMEM((kdim, tn), BF16)],
        compiler_params=_cparams(("parallel", "arbitrary")),
        name="proj",
    )(a, w)


def _outproj_kernel(a_ref, w_ref, h_ref, gate_ref, ng_ref, nmod_ref, *rest, nj, tn, final, resident, rms_a):
    if final:
        n_ref, hs_ref, ss_ref, ra_ref = rest
        hn_ref = None
    else:
        hn_ref, n_ref, hs_ref, ss_ref, ra_ref = rest
    i = pl.program_id(0)
    j = pl.program_id(1)

    if rms_a:
        @pl.when(j == 0)
        def _():
            kdim = a_ref.shape[1]
            ssq = jnp.zeros((OUT_TM, 1), F32)
            for c0 in range(0, kdim, 512):
                af = a_ref[:, c0:c0 + 512].astype(F32)
                ssq = ssq + jnp.sum(af * af, axis=-1, keepdims=True)
            ra_ref[...] = lax.rsqrt(ssq * (1.0 / kdim) + EPS)
    per = LTOT // OUT_TM
    b = i // per
    last = i % per == per - 1
    top = slice(0, OUT_TM - CTX_LEN)
    bot = slice(OUT_TM - CTX_LEN, OUT_TM)
    if final:
        n_ref = n_ref.at[0]

    def rows(ref, cols):
        lat = ref[pl.ds(b, 1), cols]
        return lat, jnp.where(last, ref[BATCH:BATCH + 1, cols], lat)

    w = w_ref[0, j] if resident else w_ref[0, 0]
    acc = jnp.dot(a_ref[...], w, preferred_element_type=F32)
    if rms_a:
        acc = acc * ra_ref[...]
    g_top, g_bot = rows(gate_ref.at[0], slice(None))
    hn_t = h_ref[top, :] + g_top * acc[top, :]
    hn_b = h_ref[bot, :] + g_bot * acc[bot, :]
    hs_ref[j, top, :] = hn_t
    hs_ref[j, bot, :] = hn_b
    if not final:
        hn_ref[top, :] = hn_t
        hn_ref[bot, :] = hn_b
    sq_t = jnp.sum(hn_t * hn_t, axis=-1, keepdims=True)
    sq_b = jnp.sum(hn_b * hn_b, axis=-1, keepdims=True)

    @pl.when(j == 0)
    def _():
        ss_ref[top, :] = sq_t
        ss_ref[bot, :] = sq_b

    @pl.when(j > 0)
    def _():
        ss_ref[top, :] += sq_t
        ss_ref[bot, :] += sq_b

    @pl.when(j == nj - 1)
    def _():
        inv = lax.rsqrt(ss_ref[...] * (1.0 / D_MODEL) + EPS)
        for jj in range(nj):
            cs = slice(jj * tn, (jj + 1) * tn)
            y = hs_ref[jj] * inv
            if final:
                n_ref[:, cs] = y * ng_ref[:, cs]
            else:
                sh_top, sh_bot = rows(nmod_ref, cs)
                sc_top, sc_bot = rows(nmod_ref, slice(D_MODEL + jj * tn, D_MODEL + (jj + 1) * tn))
                g_top = ng_ref[:, cs] * (1.0 + sc_top)
                g_bot = ng_ref[:, cs] * (1.0 + sc_bot)
                n_ref[top, cs] = (y[top, :] * g_top + sh_top).astype(n_ref.dtype)
                n_ref[bot, cs] = (y[bot, :] * g_bot + sh_bot).astype(n_ref.dtype)


def _outproj_tn(kdim):
    return 512 if kdim <= 2 * D_MODEL else 256


def _wtile_kernel(w_ref, *rest):
    if len(rest) == 2:
        g_ref, o_ref = rest
        o_ref[0, 0] = (w_ref[0] * g_ref[0]).astype(o_ref.dtype)
    else:
        o_ref, = rest
        o_ref[0, 0] = w_ref[0].astype(o_ref.dtype)


def _outproj_weights(w, row_gain=None):
    nl, kdim, _ = w.shape
    tn = _outproj_tn(kdim)
    in_specs = [pl.BlockSpec((1, kdim, tn), lambda l, j: (l, 0, j))]
    args = [w]
    if row_gain is not None:
        in_specs.append(pl.BlockSpec((1, kdim, 1), lambda l, j: (l, 0, 0)))
        args.append(row_gain.reshape(nl, kdim, 1))
    return pl.pallas_call(
        _wtile_kernel,
        out_shape=jax.ShapeDtypeStruct((nl, D_MODEL // tn, kdim, tn), BF16),
        grid=(nl, D_MODEL // tn),
        in_specs=in_specs,
        out_specs=pl.BlockSpec((1, 1, kdim, tn), lambda l, j: (l, j, 0, 0)),
        compiler_params=_cparams(("parallel", "parallel")),
        name="wtile",
    )(*args)


def _outproj(a, w, layer, h, mods, mod_layer, next_g, next_mod, *, final, rms_a=False):
    m, kdim = a.shape
    tn = _outproj_tn(kdim)
    nj = D_MODEL // tn
    gate_off = 2 * D_MODEL // tn
    resident = kdim <= 2 * D_MODEL
    if resident:
        w_spec = pl.BlockSpec((1, nj, kdim, tn), lambda i, j: (layer, 0, 0, 0), pipeline_mode=pl.Buffered(1))
    else:
        w_spec = pl.BlockSpec((1, 1, kdim, tn), lambda i, j: (layer, j, 0, 0))
    tile = pl.BlockSpec((OUT_TM, tn), lambda i, j: (i, j))
    full = pl.BlockSpec((OUT_TM, D_MODEL), lambda i, j: (i, 0))
    const = lambda i, j: (0, 0)
    if final:
        per = LTOT // OUT_TM
        out_shape = jax.ShapeDtypeStruct((BATCH, SEQ, D_MODEL), F32)
        out_specs = pl.BlockSpec((1, OUT_TM, D_MODEL), lambda i, j: (i // per, i % per, 0))
    else:
        out_shape = (jax.ShapeDtypeStruct((m, D_MODEL), F32), jax.ShapeDtypeStruct((m, D_MODEL), BF16))
        out_specs = (tile, full)
    return pl.pallas_call(
        functools.partial(_outproj_kernel, nj=nj, tn=tn, final=final, resident=resident, rms_a=rms_a),
        out_shape=out_shape,
        grid=(m // OUT_TM, nj),
        in_specs=[pl.BlockSpec((OUT_TM, kdim), lambda i, j: (i, 0)),
                  w_spec,
                  tile,
                  pl.BlockSpec((1, MOD_ROWS, tn), lambda i, j: (mod_layer, 0, gate_off + j)),
                  pl.BlockSpec((1, D_MODEL), const),
                  pl.BlockSpec((MOD_ROWS, 3 * D_MODEL), const)],
        out_specs=out_specs,
        scratch_shapes=[pltpu.VMEM((nj, OUT_TM, tn), F32),
                        pltpu.VMEM((OUT_TM, 1), F32),
                        pltpu.VMEM((OUT_TM, 1), F32)],
        compiler_params=_cparams(("parallel", "arbitrary")),
        name="outproj",
    )(a, w, h, mods, next_g.reshape(1, D_MODEL), next_mod)


def _na_tile_patterns():
    pats = []
    for t in (0, 1, NA_TILES - 1):
        ws = min(max(NA_QROWS * t - NA_WIN_ROWS // 2, 0), NA_ROWS - NA_KROWS)
        pat = np.full((NA_QROWS, NA_KROWS), NA_NDR, dtype=np.int64)
        for a in range(NA_QROWS):
            r = NA_QROWS * t + a
            rs = min(max(r - NA_WIN_ROWS // 2, 0), NA_ROWS - NA_WIN_ROWS)
            for jj in range(NA_KROWS):
                kabs = ws + jj
                if rs <= kabs < rs + NA_WIN_ROWS:
                    pat[a, jj] = kabs - r + NA_WIN_ROWS - 1
        pats.append(pat)
    return pats


def _na_build_bias(rpb_ref, slab_ref, bias_ref):
    w2 = 2 * GRID_W
    qc = lax.broadcasted_iota(jnp.int32, (GRID_W, w2), 0)
    lane = lax.broadcasted_iota(jnp.int32, (GRID_W, w2), 1)
    left = lane < GRID_W
    kc = jnp.where(left, lane, lane - GRID_W)
    c_start = jnp.clip(qc - NA_WIN_COLS // 2, 0, GRID_W - NA_WIN_COLS)
    col_ok = jnp.logical_and(kc >= c_start, kc < c_start + NA_WIN_COLS)
    shift = w2 - (NA_WIN_COLS - 1)
    for dr in range(NA_NDR):
        row = jnp.broadcast_to(rpb_ref[0, dr:dr + 1, :] * LOG2E, (GRID_W, w2))
        lo = pltpu.roll(row, shift, 1, stride=1, stride_axis=0)
        hi = pltpu.roll(row, (shift + GRID_W) % w2, 1, stride=1, stride_axis=0)
        slab_ref[dr] = jnp.where(col_ok, jnp.where(left, lo, hi), NEG)
    slab_ref[NA_NDR] = jnp.full((GRID_W, w2), NEG, F32)
    for p, pat in enumerate(_na_tile_patterns()):
        for a in range(NA_QROWS):
            for jp in range(NA_KROWS // 2):
                blk = jnp.where(left, slab_ref[int(pat[a, 2 * jp])], slab_ref[int(pat[a, 2 * jp + 1])])
                bias_ref[p, a * GRID_W:(a + 1) * GRID_W, jp * w2:(jp + 1) * w2] = blk


def _na_kernel(q_ref, k_ref, v_ref, z_ref, rpb_ref, o_ref, bias_ref, slab_ref, *, need_ctx):
    @pl.when(pl.program_id(1) == 0)
    def _():
        _na_build_bias(rpb_ref, slab_ref, bias_ref)

    kc = k_ref[0, SEQ:LTOT, :]
    vc = v_ref[0, SEQ:LTOT, :]

    def finish(o, l, r0):
        z = z_ref[0, pl.ds(r0, NA_TQ), :].astype(F32)
        o_ref[0, pl.ds(r0, NA_TQ), :] = (o * (1.0 / l) * _silu(z)).astype(o_ref.dtype)

    if need_ctx:
        s = lax.dot_general(q_ref[0, SEQ:LTOT, :], kc, _NT, preferred_element_type=F32)
        p = jnp.exp2(s - jnp.max(s, axis=-1, keepdims=True))
        finish(jnp.dot(p.astype(BF16), vc, preferred_element_type=F32), jnp.sum(p, axis=-1, keepdims=True), SEQ)
    else:
        o_ref[0, SEQ:LTOT, :] = jnp.zeros((CTX_LEN, NA_HEAD_DIM), o_ref.dtype)

    def key_start(t):
        ws = jnp.clip(NA_QROWS * t - NA_WIN_ROWS // 2, 0, NA_ROWS - NA_KROWS)
        return pl.multiple_of(ws * GRID_W, GRID_W)

    def scores(t):
        q = q_ref[0, pl.ds(pl.multiple_of(t * NA_TQ, NA_TQ), NA_TQ), :]
        pat = jnp.where(t == 0, 0, jnp.where(t == NA_TILES - 1, 2, 1))
        s1 = lax.dot_general(q, k_ref[0, pl.ds(key_start(t), NA_TK), :], _NT,
                             preferred_element_type=F32) + bias_ref[pat]
        return s1, lax.dot_general(q, kc, _NT, preferred_element_type=F32)

    def tile(t, carry, has_next=True):
        s1, s2 = carry
        nxt = scores(t + 1) if has_next else None
        m = jnp.maximum(jnp.max(s1, axis=-1, keepdims=True), jnp.max(s2, axis=-1, keepdims=True))
        p1 = jnp.exp2(s1 - m)
        p2 = jnp.exp2(s2 - m)
        l = jnp.sum(p1, axis=-1, keepdims=True) + jnp.sum(p2, axis=-1, keepdims=True)
        vw = v_ref[0, pl.ds(key_start(t), NA_TK), :]
        o = (jnp.dot(p1.astype(BF16), vw, preferred_element_type=F32)
             + jnp.dot(p2.astype(BF16), vc, preferred_element_type=F32))
        finish(o, l, pl.multiple_of(t * NA_TQ, NA_TQ))
        return nxt

    last = lax.fori_loop(0, NA_TILES - 1, tile, scores(0), unroll=2)
    tile(NA_TILES - 1, last, has_next=False)


def _na_attention(p, rpb, need_ctx):
    hd = NA_HEAD_DIM
    w2 = 2 * GRID_W
    rpb_pad = jnp.pad(rpb, ((0, 0), (0, NA_NDR + 1 - rpb.shape[1]), (0, w2 - rpb.shape[2])))
    blk = lambda off: pl.BlockSpec((1, LTOT, hd), lambda h, b: (b, 0, off + h))
    return pl.pallas_call(
        functools.partial(_na_kernel, need_ctx=need_ctx),
        out_shape=jax.ShapeDtypeStruct((BATCH, LTOT, D_MODEL), BF16),
        grid=(NA_HEADS, BATCH),
        in_specs=[blk(0), blk(NA_HEADS), blk(2 * NA_HEADS), blk(3 * NA_HEADS),
                  pl.BlockSpec((1, NA_NDR + 1, w2), lambda h, b: (h, 0, 0))],
        out_specs=blk(0),
        scratch_shapes=[pltpu.VMEM((3, NA_TQ, NA_TK), F32),
                        pltpu.VMEM((NA_NDR + 1, GRID_W, w2), F32)],
        compiler_params=_cparams(("parallel", "arbitrary")),
        name="na_attention",
    )(p, p, p, p, rpb_pad)


def _conv_kernel(u_ref, w_ref, b_ref, o_ref):
    u = u_ref[0].astype(F32)
    row = lax.broadcasted_iota(jnp.int32, u.shape, 0)
    seg_first = jnp.logical_or(row == 0, row == SEQ)
    seg_last = jnp.logical_or(row == SEQ - 1, row == LTOT - 1)
    up = jnp.where(seg_first, 0.0, pltpu.roll(u, 1, 0))
    un = jnp.where(seg_last, 0.0, pltpu.roll(u, LTOT - 1, 0))
    w = w_ref[...]
    y = w[0:1] * up + w[1:2] * u + w[2:3] * un + b_ref[...]
    o_ref[0] = _silu(y).astype(o_ref.dtype)


def _ssd_conv(p, conv_w, conv_b):
    tc = 512
    off = SSD_D_INNER // tc
    return pl.pallas_call(
        _conv_kernel,
        out_shape=jax.ShapeDtypeStruct((BATCH, LTOT, SSD_CONV_CH), BF16),
        grid=(BATCH, SSD_CONV_CH // tc),
        in_specs=[pl.BlockSpec((1, LTOT, tc), lambda b, j: (b, 0, off + j)),
                  pl.BlockSpec((3, tc), lambda b, j: (0, j)),
                  pl.BlockSpec((1, tc), lambda b, j: (0, j))],
        out_specs=pl.BlockSpec((1, LTOT, tc), lambda b, j: (b, 0, j)),
        compiler_params=_cparams(("parallel", "parallel")),
        name="ssd_conv",
    )(p, conv_w, conv_b.reshape(1, SSD_CONV_CH))


def _bf16_pieces(v):
    hi = v.astype(BF16)
    r1 = v - hi.astype(F32)
    mid = r1.astype(BF16)
    lo = (r1 - mid.astype(F32)).astype(BF16)
    return hi, mid, lo


def _split3(v):
    hi, mid, lo = _bf16_pieces(v)
    lane = lax.broadcasted_iota(jnp.int32, v.shape, 1)
    return jnp.where(lane < SSD_DL, hi, jnp.where(lane < 2 * SSD_DL, mid, lo))


def _ssd_kernel(x_ref, b_ref, c_ref, z_ref, dtc_ref, dtr_ref, pc_ref, pr_ref, dskip_ref,
                y_ref, yacc_ref, s_ref):
    q = SSD_CHUNK
    hp = SSD_HPG
    li = lax.broadcasted_iota(jnp.int32, (q, q), 0)
    si = lax.broadcasted_iota(jnp.int32, (q, q), 1)
    lower = li >= si
    upper = li <= si
    tri_lo = lower.astype(BF16)
    tri_up = upper.astype(BF16)
    tri_lanes = (jnp.concatenate([tri_lo] * 3, axis=1), jnp.concatenate([tri_up] * 3, axis=1))
    tri_rows = (jnp.concatenate([tri_up] * 3, axis=0), jnp.concatenate([tri_lo] * 3, axis=0))
    left = si < SSD_HEAD_DIM

    def expand_matrix(d):
        r = lax.broadcasted_iota(jnp.int32, (3 * SSD_DL, SSD_GW), 0) % SSD_DL
        c = lax.broadcasted_iota(jnp.int32, (3 * SSD_DL, SSD_GW), 1) // SSD_HEAD_DIM
        return (r == d * hp + c).astype(BF16)

    expand = (expand_matrix(0), expand_matrix(1))

    bias_c = pc_ref[0, 0:1, :]
    a_c = -jnp.exp(pc_ref[0, 1:2, :]) * LOG2E
    bias_r = pr_ref[0, :, 0:1]
    a_r = -jnp.exp(pr_ref[0, :, 1:2]) * LOG2E

    s_ref[...] = jnp.zeros_like(s_ref)

    def prep(c, d):
        r0 = pl.multiple_of(c * q, q)
        ex = expand[d]
        dtc = _softplus(dtc_ref[0, 0, pl.ds(r0, q), :] + bias_c)
        acum_c = jnp.dot(tri_lanes[d], jnp.concatenate(_bf16_pieces(dtc * a_c), axis=0),
                         preferred_element_type=F32)
        dtr = _softplus(dtr_ref[0, 0, c] + bias_r)
        acum_r = jnp.dot(jnp.concatenate(_bf16_pieces(dtr * a_r), axis=1), tri_rows[d],
                         preferred_element_type=F32)
        tot_c = acum_c[q - 1:q, :] if d == 0 else acum_c[0:1, :]
        ea_c = jnp.exp2(acum_c)
        dw_c = dtc * jnp.exp2(tot_c - acum_c)
        ea_x = jnp.dot(_split3(ea_c), ex, preferred_element_type=F32)
        dw_x = jnp.dot(_split3(dw_c), ex, preferred_element_type=F32)
        return ea_x, dw_x, acum_c, acum_r, dtr

    def chunk(c, d, prepared):
        ea_x, dw_x, acum_c, acum_r, dtr = prepared
        r0 = pl.multiple_of(c * q, q)
        mask = lower if d == 0 else upper
        xb = x_ref[0, pl.ds(r0, q), :]
        bm = b_ref[0, pl.ds(r0, q), :]
        cm = c_ref[0, pl.ds(r0, q), :]
        etot_x = ea_x[q - 1:q, :] if d == 0 else ea_x[0:1, :]
        cb = jnp.where(mask, lax.dot_general(cm, bm, _NT, preferred_element_type=F32), 0.0)
        bt = bm.astype(F32).T.astype(BF16)
        sprev = s_ref[d]
        yoff = jnp.dot(cm, sprev.astype(BF16), preferred_element_type=F32)
        s_ref[d] = sprev * etot_x + jnp.dot(bt, (xb.astype(F32) * dw_x).astype(BF16),
                                            preferred_element_type=F32)
        zero = jnp.zeros((q, q), BF16)
        for p in range(hp // 2):
            cs = slice(p * q, (p + 1) * q)
            m_pair = []
            for ln in (d * hp + 2 * p, d * hp + 2 * p + 1):
                diff = jnp.minimum(acum_c[:, ln:ln + 1] - acum_r[ln:ln + 1, :], 0.0)
                m_pair.append((cb * jnp.exp2(diff) * dtr[ln:ln + 1, :]).astype(BF16))
            x_p = xb[:, cs]
            rhs = jnp.concatenate([jnp.where(left, x_p, zero), jnp.where(left, zero, x_p)], axis=0)
            yd = jnp.dot(jnp.concatenate(m_pair, axis=1), rhs, preferred_element_type=F32)
            yacc_ref[d, pl.ds(r0, q), cs] = yd + ea_x[:, cs] * yoff[:, cs]

    lat_chunks = SSD_NCHUNK - SSD_CCHUNK

    def fwd_chunk(k):
        return jnp.where(k < SSD_CCHUNK, lat_chunks + k, k - SSD_CCHUNK)

    def bwd_chunk(k):
        return SSD_NCHUNK - 1 - k

    def step(k, carry):
        kn = jnp.minimum(k + 1, SSD_NCHUNK - 1)
        nxt = (prep(fwd_chunk(kn), 0), prep(bwd_chunk(kn), 1))
        chunk(fwd_chunk(k), 0, carry[0])
        chunk(bwd_chunk(k), 1, carry[1])
        return nxt

    lax.fori_loop(0, SSD_NCHUNK, step, (prep(fwd_chunk(0), 0), prep(bwd_chunk(0), 1)))

    y = yacc_ref[0] + yacc_ref[1] + x_ref[0].astype(F32) * dskip_ref[0]
    y_ref[0] = (y * _silu(z_ref[0].astype(F32))).astype(y_ref.dtype)


def _ssd_scan(xbc, p, dt_raw, dt_bias, a_log, d_skip):
    g, hp, dl = SSD_GROUPS, SSD_HPG, SSD_DL
    dt = dt_raw.reshape(BATCH, LTOT, 2, g, hp).transpose(0, 3, 1, 2, 4).reshape(BATCH, g, LTOT, dl)
    dt_col = jnp.tile(dt, (1, 1, 1, 3))
    dt_row = dt.reshape(BATCH, g, SSD_NCHUNK, SSD_CHUNK, dl).transpose(0, 1, 2, 4, 3)
    par = jnp.stack([dt_bias, a_log]).reshape(2, 2, g, hp).transpose(2, 0, 1, 3).reshape(g, 2, dl)
    par_col = jnp.pad(jnp.tile(par, (1, 1, 3)), ((0, 0), (0, 6), (0, 0)))
    par_row = jnp.pad(par.transpose(0, 2, 1), ((0, 0), (0, 0), (0, 126)))
    dskip = jnp.repeat(d_skip, SSD_HEAD_DIM).reshape(g, 1, SSD_GW)
    xoff = SSD_D_INNER // SSD_STATE
    return pl.pallas_call(
        _ssd_kernel,
        out_shape=jax.ShapeDtypeStruct((BATCH, LTOT, SSD_D_INNER), BF16),
        grid=(BATCH, g),
        in_specs=[pl.BlockSpec((1, LTOT, SSD_GW), lambda b, j: (b, 0, j)),
                  pl.BlockSpec((1, LTOT, SSD_STATE), lambda b, j: (b, 0, xoff + j)),
                  pl.BlockSpec((1, LTOT, SSD_STATE), lambda b, j: (b, 0, xoff + g + j)),
                  pl.BlockSpec((1, LTOT, SSD_GW), lambda b, j: (b, 0, j)),
                  pl.BlockSpec((1, 1, LTOT, 3 * dl), lambda b, j: (b, j, 0, 0)),
                  pl.BlockSpec((1, 1, SSD_NCHUNK, dl, SSD_CHUNK), lambda b, j: (b, j, 0, 0, 0)),
                  pl.BlockSpec((1, 8, 3 * dl), lambda b, j: (j, 0, 0)),
                  pl.BlockSpec((1, dl, 128), lambda b, j: (j, 0, 0)),
                  pl.BlockSpec((1, 1, SSD_GW), lambda b, j: (j, 0, 0))],
        out_specs=pl.BlockSpec((1, LTOT, SSD_GW), lambda b, j: (b, 0, j)),
        scratch_shapes=[pltpu.VMEM((2, LTOT, SSD_GW), F32),
                        pltpu.VMEM((2, SSD_STATE, SSD_GW), F32)],
        compiler_params=_cparams(("parallel", "parallel")),
        name="ssd_scan",
    )(xbc, xbc, xbc, p, dt_col, dt_row, par_col, par_row, dskip)


def _sg_kernel(z_ref, u_ref, v_ref, g_ref, b_ref, ws_ref, bs_ref, o_ref):
    v = v_ref[...].astype(F32)
    mu = jnp.mean(v, axis=-1, keepdims=True)
    vc = v - mu
    var = jnp.mean(vc * vc, axis=-1, keepdims=True)
    vn = (vc * lax.rsqrt(var + EPS) * g_ref[...] + b_ref[...]).astype(BF16)
    for g in range(SG_GROUPS):
        cs = slice(g * SG_GW, (g + 1) * SG_GW)
        sv = jnp.dot(ws_ref[g], vn[:, cs], preferred_element_type=F32) + bs_ref[:, g:g + 1]
        o_ref[:, cs] = (u_ref[:, cs].astype(F32) * sv * _silu(z_ref[:, cs].astype(F32))).astype(o_ref.dtype)


def _sg_gate(p, ln_g, ln_b, w_s, b_s_t):
    rows = p.shape[0]
    blk = lambda j: pl.BlockSpec((SG_CHUNK, SG_HALF), lambda i: (i, j))
    return pl.pallas_call(
        _sg_kernel,
        out_shape=jax.ShapeDtypeStruct((rows, SG_HALF), BF16),
        grid=(rows // SG_CHUNK,),
        in_specs=[blk(0), blk(1), blk(2),
                  pl.BlockSpec((1, SG_HALF), lambda i: (0, 0)),
                  pl.BlockSpec((1, SG_HALF), lambda i: (0, 0)),
                  pl.BlockSpec((SG_GROUPS, SG_CHUNK, SG_CHUNK), lambda i: (0, 0, 0)),
                  pl.BlockSpec((SG_CHUNK, SG_GROUPS), lambda i: (0, 0))],
        out_specs=blk(0),
        compiler_params=_cparams(("parallel",)),
        name="sg_gate",
    )(p, p, p, ln_g.reshape(1, SG_HALF), ln_b.reshape(1, SG_HALF), w_s, b_s_t)


def kernel(x, c, ctx, c_ctx, norm_g, ada_w, ada_b, na_w_in, na_rpb, na_w_out,
           ssd_w_in, ssd_conv_w, ssd_conv_b, ssd_dt_bias, ssd_a_log, ssd_d_skip,
           ssd_norm_g, ssd_w_out, sg_w_in, sg_ln_g, sg_ln_b, sg_w_s, sg_b_s, sg_w_out,
           final_norm_g):
    d = D_MODEL
    c_rows = jnp.concatenate([c, c_ctx[None], jnp.zeros((MOD_ROWS - BATCH - 1, d), F32)], axis=0)
    mods = _ada_all(c_rows, ada_w, ada_b)

    h, n = _norm_mod(ctx, x, norm_g[0], mods[0])

    na_w_out_b = _outproj_weights(na_w_out)
    ssd_w_out_b = _outproj_weights(ssd_w_out, ssd_norm_g)
    sg_w_out_b = _outproj_weights(sg_w_out)

    for i in range(DEPTH):
        kind, j = i % N_MIXERS, i // N_MIXERS
        need_ctx = i < DEPTH - 1
        if kind == 0:
            p = _proj(n, na_w_in, j, n=4 * d, out_dtype=BF16, scale_cols=d, scale=NA_HEAD_DIM ** -0.5 * LOG2E)
            y = _na_attention(p.reshape(BATCH, LTOT, 4 * d), na_rpb[j], need_ctx).reshape(ROWS, d)
            w_out = na_w_out_b
        elif kind == 1:
            p = _proj(n, ssd_w_in, j, n=SSD_MAIN, out_dtype=BF16).reshape(BATCH, LTOT, SSD_MAIN)
            dt_raw = _proj(n, ssd_w_in, j, n=2 * SSD_HEADS, out_dtype=F32, tn=2 * SSD_HEADS, col0=SSD_MAIN)
            xbc = _ssd_conv(p, ssd_conv_w[j], ssd_conv_b[j])
            y = _ssd_scan(xbc, p, dt_raw, ssd_dt_bias[j], ssd_a_log[j], ssd_d_skip[j]).reshape(ROWS, SSD_D_INNER)
            w_out = ssd_w_out_b
        else:
            p = _proj(n, sg_w_in, j, n=3 * SG_HALF, out_dtype=BF16, gelu_from_col=SG_HALF)
            y = _sg_gate(p, sg_ln_g[j], sg_ln_b[j], sg_w_s[j].astype(BF16), jnp.transpose(sg_b_s[j]))
            w_out = sg_w_out_b

        rms_a = kind == 1
        if need_ctx:
            h, n = _outproj(y, w_out, j, h, mods, i, norm_g[i + 1], mods[i + 1], final=False, rms_a=rms_a)
        else:
            out = _outproj(y, w_out, j, h, mods, i, final_norm_g, mods[i], final=True, rms_a=rms_a)

    return out
```

```python
import functools
import math

import numpy as np
import jax
import jax.numpy as jnp
from jax import lax
from jax.experimental import pallas as pl
from jax.experimental.pallas import tpu as pltpu

D_MODEL = 2048
BATCH = 4
SEQ = 2048
DEPTH = 4
GRID_W = 64
CTX_LEN = 256
N_MIXERS = 3
EPS = 1e-6
LTOT = SEQ + CTX_LEN
ROWS = BATCH * LTOT

NA_HEADS = 16
NA_HEAD_DIM = D_MODEL // NA_HEADS
NA_WIN_ROWS = 8
NA_WIN_COLS = 16
NA_ROWS = SEQ // GRID_W
NA_QROWS = 4
NA_KROWS = NA_QROWS + NA_WIN_ROWS
NA_TQ = NA_QROWS * GRID_W
NA_TK = NA_KROWS * GRID_W
NA_TILES = NA_ROWS // NA_QROWS
NA_NDR = 2 * NA_WIN_ROWS - 1

SSD_D_INNER = 2 * D_MODEL
SSD_HEAD_DIM = 64
SSD_HEADS = SSD_D_INNER // SSD_HEAD_DIM
SSD_GROUPS = 8
SSD_STATE = 128
SSD_CHUNK = 128
SSD_GN = SSD_GROUPS * SSD_STATE
SSD_CONV_CH = SSD_D_INNER + 2 * SSD_GN
SSD_MAIN = SSD_D_INNER + SSD_CONV_CH
SSD_HPG = SSD_HEADS // SSD_GROUPS
SSD_GW = SSD_HPG * SSD_HEAD_DIM
SSD_NCHUNK = LTOT // SSD_CHUNK
SSD_CCHUNK = CTX_LEN // SSD_CHUNK
SSD_DL = 2 * SSD_HPG

SG_HALF = 3 * D_MODEL
SG_GROUPS = 16
SG_CHUNK = 128
SG_GW = SG_HALF // SG_GROUPS

NEG = -1e30
VMEM_LIMIT = 56 * 1024 * 1024
MOD_ROWS = 8

PROJ_TM, PROJ_TN = 1536, 1024
OUT_TM = 768
LOG2E = math.log2(math.e)

F32 = jnp.float32
BF16 = jnp.bfloat16
_NT = (((1,), (1,)), ((), ()))


def _cparams(sem):
    return pltpu.CompilerParams(dimension_semantics=sem, vmem_limit_bytes=VMEM_LIMIT)


def _silu(x):
    return x * (1.0 / (1.0 + jnp.exp2(x * -LOG2E)))


def _gelu_tanh(x):
    c = math.sqrt(2.0 / math.pi)
    return 0.5 * x * (1.0 + jnp.tanh(c * (x + 0.044715 * (x * x * x))))


def _softplus(x):
    return jnp.maximum(x, 0.0) + jnp.log(1.0 + jnp.exp(-jnp.abs(x)))


def _ada_kernel(c_ref, w_ref, b_ref, o_ref):
    a = _silu(c_ref[...]).astype(BF16)
    o_ref[0] = jnp.dot(a, w_ref[0].astype(BF16), preferred_element_type=F32) + b_ref[0]


def _ada_all(c_rows, ada_w, ada_b):
    tn = 1024
    return pl.pallas_call(
        _ada_kernel,
        out_shape=jax.ShapeDtypeStruct((DEPTH, MOD_ROWS, 3 * D_MODEL), F32),
        grid=(DEPTH, 3 * D_MODEL // tn),
        in_specs=[pl.BlockSpec((MOD_ROWS, D_MODEL), lambda l, j: (0, 0)),
                  pl.BlockSpec((1, D_MODEL, tn), lambda l, j: (l, 0, j)),
                  pl.BlockSpec((1, 1, tn), lambda l, j: (l, 0, j))],
        out_specs=pl.BlockSpec((1, MOD_ROWS, tn), lambda l, j: (l, 0, j)),
        compiler_params=_cparams(("arbitrary", "arbitrary")),
        name="ada_mod",
    )(c_rows, ada_w, ada_b.reshape(DEPTH, 1, 3 * D_MODEL))


def _mod_rows(mod_ref, row):
    m = mod_ref[pl.ds(row, 1), :]
    return m[:, :D_MODEL], m[:, D_MODEL:2 * D_MODEL], m[:, 2 * D_MODEL:]


def _norm_mod_kernel(ctx_ref, x_ref, g_ref, mod_ref, h_ref, o_ref):
    b = pl.program_id(0)
    r = pl.program_id(1)
    is_ctx = r == LTOT // CTX_LEN - 1
    shift, scale, _ = _mod_rows(mod_ref, jnp.where(is_ctx, BATCH, b))
    x = jnp.where(is_ctx, ctx_ref[0], x_ref[0])
    h_ref[...] = x
    y = x * lax.rsqrt(jnp.mean(x * x, axis=-1, keepdims=True) + EPS) * g_ref[...]
    o_ref[...] = (y * (1.0 + scale) + shift).astype(o_ref.dtype)


def _norm_mod(ctx, x, g, mod):
    per = LTOT // CTX_LEN
    rows = pl.BlockSpec((CTX_LEN, D_MODEL), lambda b, r: (b * per + r, 0))
    return pl.pallas_call(
        _norm_mod_kernel,
        out_shape=(jax.ShapeDtypeStruct((ROWS, D_MODEL), F32), jax.ShapeDtypeStruct((ROWS, D_MODEL), BF16)),
        grid=(BATCH, per),
        in_specs=[pl.BlockSpec((1, CTX_LEN, D_MODEL), lambda b, r: (b, 0, 0)),
                  pl.BlockSpec((1, CTX_LEN, D_MODEL), lambda b, r: (b, jnp.minimum(r, per - 2), 0)),
                  pl.BlockSpec((1, D_MODEL), lambda b, r: (0, 0)),
                  pl.BlockSpec((MOD_ROWS, 3 * D_MODEL), lambda b, r: (0, 0))],
        out_specs=(rows, rows),
        compiler_params=_cparams(("parallel", "parallel")),
        name="norm_mod",
    )(ctx, x, g.reshape(1, D_MODEL), mod)


def _proj_kernel(a_ref, w_ref, o_ref, wb_ref, *, gelu_from, scale_tiles, scale):
    j = pl.program_id(0)
    i = pl.program_id(1)

    @pl.when(i == 0)
    def _():
        w = w_ref[0]
        if scale_tiles:
            w = w * jnp.where(j < scale_tiles, scale, 1.0)
        wb_ref[...] = w.astype(BF16)

    acc = jnp.dot(a_ref[...], wb_ref[...], preferred_element_type=F32)
    if gelu_from is None:
        o_ref[...] = acc.astype(o_ref.dtype)
    else:
        @pl.when(j >= gelu_from)
        def _():
            o_ref[...] = _gelu_tanh(acc).astype(o_ref.dtype)

        @pl.when(j < gelu_from)
        def _():
            o_ref[...] = acc.astype(o_ref.dtype)


def _proj(a, w, layer, *, n, out_dtype, tn=PROJ_TN, col0=0, gelu_from_col=None, scale_cols=0, scale=1.0):
    m, kdim = a.shape
    off = col0 // tn
    return pl.pallas_call(
        functools.partial(_proj_kernel,
                          gelu_from=None if gelu_from_col is None else gelu_from_col // tn,
                          scale_tiles=scale_cols // tn, scale=scale),
        out_shape=jax.ShapeDtypeStruct((m, n), out_dtype),
        grid=(n // tn, m // PROJ_TM),
        in_specs=[pl.BlockSpec((PROJ_TM, kdim), lambda j, i: (i, 0)),
                  pl.BlockSpec((1, kdim, tn), lambda j, i: (layer, 0, off + j))],
        out_specs=pl.BlockSpec((PROJ_TM, tn), lambda j, i: (i, j)),
        scratch_shapes=[pltpu.VMEM((kdim, tn), BF16)],
        compiler_params=_cparams(("parallel", "arbitrary")),
        name="proj",
    )(a, w)


def _outproj_kernel(a_ref, w_ref, h_ref, gate_ref, ng_ref, nmod_ref, *rest, nj, tn, final, resident, rms_a):
    if final:
        n_ref, hs_ref, ss_ref, ra_ref = rest
        hn_ref = None
    else:
        hn_ref, n_ref, hs_ref, ss_ref, ra_ref = rest
    i = pl.program_id(0)
    j = pl.program_id(1)

    if rms_a:
        @pl.when(j == 0)
        def _():
            kdim = a_ref.shape[1]
            ssq = jnp.zeros((OUT_TM, 1), F32)
            for c0 in range(0, kdim, 512):
                af = a_ref[:, c0:c0 + 512].astype(F32)
                ssq = ssq + jnp.sum(af * af, axis=-1, keepdims=True)
            ra_ref[...] = lax.rsqrt(ssq * (1.0 / kdim) + EPS)
    per = LTOT // OUT_TM
    b = i // per
    last = i % per == per - 1
    top = slice(0, OUT_TM - CTX_LEN)
    bot = slice(OUT_TM - CTX_LEN, OUT_TM)
    if final:
        n_ref = n_ref.at[0]

    def rows(ref, cols):
        lat = ref[pl.ds(b, 1), cols]
        return lat, jnp.where(last, ref[BATCH:BATCH + 1, cols], lat)

    w = w_ref[0, j] if resident else w_ref[0, 0]
    acc = jnp.dot(a_ref[...], w, preferred_element_type=F32)
    if rms_a:
        acc = acc * ra_ref[...]
    g_top, g_bot = rows(gate_ref.at[0], slice(None))
    hn_t = h_ref[top, :] + g_top * acc[top, :]
    hn_b = h_ref[bot, :] + g_bot * acc[bot, :]
    hs_ref[j, top, :] = hn_t
    hs_ref[j, bot, :] = hn_b
    if not final:
        hn_ref[top, :] = hn_t
        hn_ref[bot, :] = hn_b
    sq_t = jnp.sum(hn_t * hn_t, axis=-1, keepdims=True)
    sq_b = jnp.sum(hn_b * hn_b, axis=-1, keepdims=True)

    @pl.when(j == 0)
    def _():
        ss_ref[top, :] = sq_t
        ss_ref[bot, :] = sq_b

    @pl.when(j > 0)
    def _():
        ss_ref[top, :] += sq_t
        ss_ref[bot, :] += sq_b

    @pl.when(j == nj - 1)
    def _():
        inv = lax.rsqrt(ss_ref[...] * (1.0 / D_MODEL) + EPS)
        for jj in range(nj):
            cs = slice(jj * tn, (jj + 1) * tn)
            y = hs_ref[jj] * inv
            if final:
                n_ref[:, cs] = y * ng_ref[:, cs]
            else:
                sh_top, sh_bot = rows(nmod_ref, cs)
                sc_top, sc_bot = rows(nmod_ref, slice(D_MODEL + jj * tn, D_MODEL + (jj + 1) * tn))
                g_top = ng_ref[:, cs] * (1.0 + sc_top)
                g_bot = ng_ref[:, cs] * (1.0 + sc_bot)
                n_ref[top, cs] = (y[top, :] * g_top + sh_top).astype(n_ref.dtype)
                n_ref[bot, cs] = (y[bot, :] * g_bot + sh_bot).astype(n_ref.dtype)


def _outproj_tn(kdim):
    return 512 if kdim <= 2 * D_MODEL else 256


def _wtile_kernel(w_ref, *rest):
    if len(rest) == 2:
        g_ref, o_ref = rest
        o_ref[0, 0] = (w_ref[0] * g_ref[0]).astype(o_ref.dtype)
    else:
        o_ref, = rest
        o_ref[0, 0] = w_ref[0].astype(o_ref.dtype)


def _outproj_weights(w, row_gain=None):
    nl, kdim, _ = w.shape
    tn = _outproj_tn(kdim)
    in_specs = [pl.BlockSpec((1, kdim, tn), lambda l, j: (l, 0, j))]
    args = [w]
    if row_gain is not None:
        in_specs.append(pl.BlockSpec((1, kdim, 1), lambda l, j: (l, 0, 0)))
        args.append(row_gain.reshape(nl, kdim, 1))
    return pl.pallas_call(
        _wtile_kernel,
        out_shape=jax.ShapeDtypeStruct((nl, D_MODEL // tn, kdim, tn), BF16),
        grid=(nl, D_MODEL // tn),
        in_specs=in_specs,
        out_specs=pl.BlockSpec((1, 1, kdim, tn), lambda l, j: (l, j, 0, 0)),
        compiler_params=_cparams(("parallel", "parallel")),
        name="wtile",
    )(*args)


def _outproj(a, w, layer, h, mods, mod_layer, next_g, next_mod, *, final, rms_a=False):
    m, kdim = a.shape
    tn = _outproj_tn(kdim)
    nj = D_MODEL // tn
    gate_off = 2 * D_MODEL // tn
    resident = kdim <= 2 * D_MODEL
    if resident:
        w_spec = pl.BlockSpec((1, nj, kdim, tn), lambda i, j: (layer, 0, 0, 0), pipeline_mode=pl.Buffered(1))
    else:
        w_spec = pl.BlockSpec((1, 1, kdim, tn), lambda i, j: (layer, j, 0, 0))
    tile = pl.BlockSpec((OUT_TM, tn), lambda i, j: (i, j))
    full = pl.BlockSpec((OUT_TM, D_MODEL), lambda i, j: (i, 0))
    const = lambda i, j: (0, 0)
    if final:
        per = LTOT // OUT_TM
        out_shape = jax.ShapeDtypeStruct((BATCH, SEQ, D_MODEL), F32)
        out_specs = pl.BlockSpec((1, OUT_TM, D_MODEL), lambda i, j: (i // per, i % per, 0))
    else:
        out_shape = (jax.ShapeDtypeStruct((m, D_MODEL), F32), jax.ShapeDtypeStruct((m, D_MODEL), BF16))
        out_specs = (tile, full)
    return pl.pallas_call(
        functools.partial(_outproj_kernel, nj=nj, tn=tn, final=final, resident=resident, rms_a=rms_a),
        out_shape=out_shape,
        grid=(m // OUT_TM, nj),
        in_specs=[pl.BlockSpec((OUT_TM, kdim), lambda i, j: (i, 0)),
                  w_spec,
                  tile,
                  pl.BlockSpec((1, MOD_ROWS, tn), lambda i, j: (mod_layer, 0, gate_off + j)),
                  pl.BlockSpec((1, D_MODEL), const),
                  pl.BlockSpec((MOD_ROWS, 3 * D_MODEL), const)],
        out_specs=out_specs,
        scratch_shapes=[pltpu.VMEM((nj, OUT_TM, tn), F32),
                        pltpu.VMEM((OUT_TM, 1), F32),
                        pltpu.VMEM((OUT_TM, 1), F32)],
        compiler_params=_cparams(("parallel", "arbitrary")),
        name="outproj",
    )(a, w, h, mods, next_g.reshape(1, D_MODEL), next_mod)


def _na_tile_patterns():
    pats = []
    for t in (0, 1, NA_TILES - 1):
        ws = min(max(NA_QROWS * t - NA_WIN_ROWS // 2, 0), NA_ROWS - NA_KROWS)
        pat = np.full((NA_QROWS, NA_KROWS), NA_NDR, dtype=np.int64)
        for a in range(NA_QROWS):
            r = NA_QROWS * t + a
            rs = min(max(r - NA_WIN_ROWS // 2, 0), NA_ROWS - NA_WIN_ROWS)
            for jj in range(NA_KROWS):
                kabs = ws + jj
                if rs <= kabs < rs + NA_WIN_ROWS:
                    pat[a, jj] = kabs - r + NA_WIN_ROWS - 1
        pats.append(pat)
    return pats


def _na_build_bias(rpb_ref, slab_ref, bias_ref):
    w2 = 2 * GRID_W
    qc = lax.broadcasted_iota(jnp.int32, (GRID_W, w2), 0)
    lane = lax.broadcasted_iota(jnp.int32, (GRID_W, w2), 1)
    left = lane < GRID_W
    kc = jnp.where(left, lane, lane - GRID_W)
    c_start = jnp.clip(qc - NA_WIN_COLS // 2, 0, GRID_W - NA_WIN_COLS)
    col_ok = jnp.logical_and(kc >= c_start, kc < c_start + NA_WIN_COLS)
    shift = w2 - (NA_WIN_COLS - 1)
    for dr in range(NA_NDR):
        row = jnp.broadcast_to(rpb_ref[0, dr:dr + 1, :] * LOG2E, (GRID_W, w2))
        lo = pltpu.roll(row, shift, 1, stride=1, stride_axis=0)
        hi = pltpu.roll(row, (shift + GRID_W) % w2, 1, stride=1, stride_axis=0)
        slab_ref[dr] = jnp.where(col_ok, jnp.where(left, lo, hi), NEG)
    slab_ref[NA_NDR] = jnp.full((GRID_W, w2), NEG, F32)
    for p, pat in enumerate(_na_tile_patterns()):
        for a in range(NA_QROWS):
            for jp in range(NA_KROWS // 2):
                blk = jnp.where(left, slab_ref[int(pat[a, 2 * jp])], slab_ref[int(pat[a, 2 * jp + 1])])
                bias_ref[p, a * GRID_W:(a + 1) * GRID_W, jp * w2:(jp + 1) * w2] = blk


def _na_kernel(q_ref, k_ref, v_ref, z_ref, rpb_ref, o_ref, bias_ref, slab_ref, *, need_ctx):
    @pl.when(pl.program_id(1) == 0)
    def _():
        _na_build_bias(rpb_ref, slab_ref, bias_ref)

    kc = k_ref[0, SEQ:LTOT, :]
    vc = v_ref[0, SEQ:LTOT, :]

    def finish(o, l, r0):
        z = z_ref[0, pl.ds(r0, NA_TQ), :].astype(F32)
        o_ref[0, pl.ds(r0, NA_TQ), :] = (o * (1.0 / l) * _silu(z)).astype(o_ref.dtype)

    if need_ctx:
        s = lax.dot_general(q_ref[0, SEQ:LTOT, :], kc, _NT, preferred_element_type=F32)
        p = jnp.exp2(s - jnp.max(s, axis=-1, keepdims=True))
        finish(jnp.dot(p.astype(BF16), vc, preferred_element_type=F32), jnp.sum(p, axis=-1, keepdims=True), SEQ)
    else:
        o_ref[0, SEQ:LTOT, :] = jnp.zeros((CTX_LEN, NA_HEAD_DIM), o_ref.dtype)

    def key_start(t):
        ws = jnp.clip(NA_QROWS * t - NA_WIN_ROWS // 2, 0, NA_ROWS - NA_KROWS)
        return pl.multiple_of(ws * GRID_W, GRID_W)

    def scores(t):
        q = q_ref[0, pl.ds(pl.multiple_of(t * NA_TQ, NA_TQ), NA_TQ), :]
        pat = jnp.where(t == 0, 0, jnp.where(t == NA_TILES - 1, 2, 1))
        s1 = lax.dot_general(q, k_ref[0, pl.ds(key_start(t), NA_TK), :], _NT,
                             preferred_element_type=F32) + bias_ref[pat]
        return s1, lax.dot_general(q, kc, _NT, preferred_element_type=F32)

    def tile(t, carry, has_next=True):
        s1, s2 = carry
        nxt = scores(t + 1) if has_next else None
        m = jnp.maximum(jnp.max(s1, axis=-1, keepdims=True), jnp.max(s2, axis=-1, keepdims=True))
        p1 = jnp.exp2(s1 - m)
        p2 = jnp.exp2(s2 - m)
        l = jnp.sum(p1, axis=-1, keepdims=True) + jnp.sum(p2, axis=-1, keepdims=True)
        vw = v_ref[0, pl.ds(key_start(t), NA_TK), :]
        o = (jnp.dot(p1.astype(BF16), vw, preferred_element_type=F32)
             + jnp.dot(p2.astype(BF16), vc, preferred_element_type=F32))
        finish(o, l, pl.multiple_of(t * NA_TQ, NA_TQ))
        return nxt

    last = lax.fori_loop(0, NA_TILES - 1, tile, scores(0), unroll=2)
    tile(NA_TILES - 1, last, has_next=False)


def _na_attention(p, rpb, need_ctx):
    hd = NA_HEAD_DIM
    w2 = 2 * GRID_W
    rpb_pad = jnp.pad(rpb, ((0, 0), (0, NA_NDR + 1 - rpb.shape[1]), (0, w2 - rpb.shape[2])))
    blk = lambda off: pl.BlockSpec((1, LTOT, hd), lambda h, b: (b, 0, off + h))
    return pl.pallas_call(
        functools.partial(_na_kernel, need_ctx=need_ctx),
        out_shape=jax.ShapeDtypeStruct((BATCH, LTOT, D_MODEL), BF16),
        grid=(NA_HEADS, BATCH),
        in_specs=[blk(0), blk(NA_HEADS), blk(2 * NA_HEADS), blk(3 * NA_HEADS),
                  pl.BlockSpec((1, NA_NDR + 1, w2), lambda h, b: (h, 0, 0))],
        out_specs=blk(0),
        scratch_shapes=[pltpu.VMEM((3, NA_TQ, NA_TK), F32),
                        pltpu.VMEM((NA_NDR + 1, GRID_W, w2), F32)],
        compiler_params=_cparams(("parallel", "arbitrary")),
        name="na_attention",
    )(p, p, p, p, rpb_pad)


def _conv_kernel(u_ref, w_ref, b_ref, o_ref):
    u = u_ref[0].astype(F32)
    row = lax.broadcasted_iota(jnp.int32, u.shape, 0)
    seg_first = jnp.logical_or(row == 0, row == SEQ)
    seg_last = jnp.logical_or(row == SEQ - 1, row == LTOT - 1)
    up = jnp.where(seg_first, 0.0, pltpu.roll(u, 1, 0))
    un = jnp.where(seg_last, 0.0, pltpu.roll(u, LTOT - 1, 0))
    w = w_ref[...]
    y = w[0:1] * up + w[1:2] * u + w[2:3] * un + b_ref[...]
    o_ref[0] = _silu(y).astype(o_ref.dtype)


def _ssd_conv(p, conv_w, conv_b):
    tc = 512
    off = SSD_D_INNER // tc
    return pl.pallas_call(
        _conv_kernel,
        out_shape=jax.ShapeDtypeStruct((BATCH, LTOT, SSD_CONV_CH), BF16),
        grid=(BATCH, SSD_CONV_CH // tc),
        in_specs=[pl.BlockSpec((1, LTOT, tc), lambda b, j: (b, 0, off + j)),
                  pl.BlockSpec((3, tc), lambda b, j: (0, j)),
                  pl.BlockSpec((1, tc), lambda b, j: (0, j))],
        out_specs=pl.BlockSpec((1, LTOT, tc), lambda b, j: (b, 0, j)),
        compiler_params=_cparams(("parallel", "parallel")),
        name="ssd_conv",
    )(p, conv_w, conv_b.reshape(1, SSD_CONV_CH))


def _bf16_pieces(v):
    hi = v.astype(BF16)
    r1 = v - hi.astype(F32)
    mid = r1.astype(BF16)
    lo = (r1 - mid.astype(F32)).astype(BF16)
    return hi, mid, lo


def _split3(v):
    hi, mid, lo = _bf16_pieces(v)
    lane = lax.broadcasted_iota(jnp.int32, v.shape, 1)
    return jnp.where(lane < SSD_DL, hi, jnp.where(lane < 2 * SSD_DL, mid, lo))


def _ssd_kernel(x_ref, b_ref, c_ref, z_ref, dtc_ref, dtr_ref, pc_ref, pr_ref, dskip_ref,
                y_ref, yacc_ref, s_ref):
    q = SSD_CHUNK
    hp = SSD_HPG
    li = lax.broadcasted_iota(jnp.int32, (q, q), 0)
    si = lax.broadcasted_iota(jnp.int32, (q, q), 1)
    lower = li >= si
    upper = li <= si
    tri_lo = lower.astype(BF16)
    tri_up = upper.astype(BF16)
    tri_lanes = (jnp.concatenate([tri_lo] * 3, axis=1), jnp.concatenate([tri_up] * 3, axis=1))
    tri_rows = (jnp.concatenate([tri_up] * 3, axis=0), jnp.concatenate([tri_lo] * 3, axis=0))
    left = si < SSD_HEAD_DIM

    def expand_matrix(d):
        r = lax.broadcasted_iota(jnp.int32, (3 * SSD_DL, SSD_GW), 0) % SSD_DL
        c = lax.broadcasted_iota(jnp.int32, (3 * SSD_DL, SSD_GW), 1) // SSD_HEAD_DIM
        return (r == d * hp + c).astype(BF16)

    expand = (expand_matrix(0), expand_matrix(1))

    bias_c = pc_ref[0, 0:1, :]
    a_c = -jnp.exp(pc_ref[0, 1:2, :]) * LOG2E
    bias_r = pr_ref[0, :, 0:1]
    a_r = -jnp.exp(pr_ref[0, :, 1:2]) * LOG2E

    s_ref[...] = jnp.zeros_like(s_ref)

    def prep(c, d):
        r0 = pl.multiple_of(c * q, q)
        dtc = _softplus(dtc_ref[0, 0, pl.ds(r0, q), :] + bias_c)
        acum_c = jnp.dot(tri_lanes[d], jnp.concatenate(_bf16_pieces(dtc * a_c), axis=0),
                         preferred_element_type=F32)
        dtr = _softplus(dtr_ref[0, 0, c] + bias_r)
        acum_r = jnp.dot(jnp.concatenate(_bf16_pieces(dtr * a_r), axis=1), tri_rows[d],
                         preferred_element_type=F32)
        tot_c = acum_c[q - 1:q, :] if d == 0 else acum_c[0:1, :]
        ea_c = jnp.exp2(acum_c)
        dw_c = dtc * jnp.exp2(tot_c - acum_c)
        ldt = jnp.log2(dtr)
        return _split3(ea_c), _split3(dw_c), acum_c, acum_r - ldt, ldt

    def chunk(c, d, prepared):
        ea_s, dw_s, acum_c, acum_r, ldt = prepared
        ea_x = jnp.dot(ea_s, expand[d], preferred_element_type=F32)
        dw_x = jnp.dot(dw_s, expand[d], preferred_element_type=F32)
        r0 = pl.multiple_of(c * q, q)
        mask = lower if d == 0 else upper
        xb = x_ref[0, pl.ds(r0, q), :]
        bm = b_ref[0, pl.ds(r0, q), :]
        cm = c_ref[0, pl.ds(r0, q), :]
        etot_x = ea_x[q - 1:q, :] if d == 0 else ea_x[0:1, :]
        cb = jnp.where(mask, lax.dot_general(cm, bm, _NT, preferred_element_type=F32), 0.0)
        bt = bm.astype(F32).T.astype(BF16)
        sprev = s_ref[d]
        yoff = jnp.dot(cm, sprev.astype(BF16), preferred_element_type=F32)
        s_ref[d] = sprev * etot_x + jnp.dot(bt, (xb.astype(F32) * dw_x).astype(BF16),
                                            preferred_element_type=F32)
        zero = jnp.zeros((q, q), BF16)
        for p in range(hp // 2):
            cs = slice(p * q, (p + 1) * q)
            m_pair = []
            for ln in (d * hp + 2 * p, d * hp + 2 * p + 1):
                e = jnp.minimum(acum_c[:, ln:ln + 1] - acum_r[ln:ln + 1, :], ldt[ln:ln + 1, :])
                m_pair.append((cb * jnp.exp2(e)).astype(BF16))
            x_p = xb[:, cs]
            rhs = jnp.concatenate([jnp.where(left, x_p, zero), jnp.where(left, zero, x_p)], axis=0)
            yd = jnp.dot(jnp.concatenate(m_pair, axis=1), rhs, preferred_element_type=F32)
            yacc_ref[d, pl.ds(r0, q), cs] = yd + ea_x[:, cs] * yoff[:, cs]

    lat_chunks = SSD_NCHUNK - SSD_CCHUNK

    def fwd_chunk(k):
        return jnp.where(k < SSD_CCHUNK, lat_chunks + k, k - SSD_CCHUNK)

    def bwd_chunk(k):
        return SSD_NCHUNK - 1 - k

    def step(k, carry):
        kn = jnp.minimum(k + 1, SSD_NCHUNK - 1)
        nxt = (prep(fwd_chunk(kn), 0), prep(bwd_chunk(kn), 1))
        chunk(fwd_chunk(k), 0, carry[0])
        chunk(bwd_chunk(k), 1, carry[1])
        return nxt

    lax.fori_loop(0, SSD_NCHUNK, step, (prep(fwd_chunk(0), 0), prep(bwd_chunk(0), 1)))

    y = yacc_ref[0] + yacc_ref[1] + x_ref[0].astype(F32) * dskip_ref[0]
    y_ref[0] = (y * _silu(z_ref[0].astype(F32))).astype(y_ref.dtype)


def _ssd_scan(xbc, p, dt_raw, dt_bias, a_log, d_skip):
    g, hp, dl = SSD_GROUPS, SSD_HPG, SSD_DL
    dt = dt_raw.reshape(BATCH, LTOT, 2, g, hp).transpose(0, 3, 1, 2, 4).reshape(BATCH, g, LTOT, dl)
    dt_col = jnp.tile(dt, (1, 1, 1, 3))
    dt_row = dt.reshape(BATCH, g, SSD_NCHUNK, SSD_CHUNK, dl).transpose(0, 1, 2, 4, 3)
    par = jnp.stack([dt_bias, a_log]).reshape(2, 2, g, hp).transpose(2, 0, 1, 3).reshape(g, 2, dl)
    par_col = jnp.pad(jnp.tile(par, (1, 1, 3)), ((0, 0), (0, 6), (0, 0)))
    par_row = jnp.pad(par.transpose(0, 2, 1), ((0, 0), (0, 0), (0, 126)))
    dskip = jnp.repeat(d_skip, SSD_HEAD_DIM).reshape(g, 1, SSD_GW)
    xoff = SSD_D_INNER // SSD_STATE
    return pl.pallas_call(
        _ssd_kernel,
        out_shape=jax.ShapeDtypeStruct((BATCH, LTOT, SSD_D_INNER), BF16),
        grid=(BATCH, g),
        in_specs=[pl.BlockSpec((1, LTOT, SSD_GW), lambda b, j: (b, 0, j)),
                  pl.BlockSpec((1, LTOT, SSD_STATE), lambda b, j: (b, 0, xoff + j)),
                  pl.BlockSpec((1, LTOT, SSD_STATE), lambda b, j: (b, 0, xoff + g + j)),
                  pl.BlockSpec((1, LTOT, SSD_GW), lambda b, j: (b, 0, j)),
                  pl.BlockSpec((1, 1, LTOT, 3 * dl), lambda b, j: (b, j, 0, 0)),
                  pl.BlockSpec((1, 1, SSD_NCHUNK, dl, SSD_CHUNK), lambda b, j: (b, j, 0, 0, 0)),
                  pl.BlockSpec((1, 8, 3 * dl), lambda b, j: (j, 0, 0)),
                  pl.BlockSpec((1, dl, 128), lambda b, j: (j, 0, 0)),
                  pl.BlockSpec((1, 1, SSD_GW), lambda b, j: (j, 0, 0))],
        out_specs=pl.BlockSpec((1, LTOT, SSD_GW), lambda b, j: (b, 0, j)),
        scratch_shapes=[pltpu.VMEM((2, LTOT, SSD_GW), F32),
                        pltpu.VMEM((2, SSD_STATE, SSD_GW), F32)],
        compiler_params=_cparams(("parallel", "parallel")),
        name="ssd_scan",
    )(xbc, xbc, xbc, p, dt_col, dt_row, par_col, par_row, dskip)


def _sg_kernel(z_ref, u_ref, v_ref, g_ref, b_ref, ws_ref, bs_ref, o_ref):
    v = v_ref[...].astype(F32)
    mu = jnp.mean(v, axis=-1, keepdims=True)
    vc = v - mu
    var = jnp.mean(vc * vc, axis=-1, keepdims=True)
    vn = (vc * lax.rsqrt(var + EPS) * g_ref[...] + b_ref[...]).astype(BF16)
    for g in range(SG_GROUPS):
        cs = slice(g * SG_GW, (g + 1) * SG_GW)
        sv = jnp.dot(ws_ref[g], vn[:, cs], preferred_element_type=F32) + bs_ref[:, g:g + 1]
        o_ref[:, cs] = (u_ref[:, cs].astype(F32) * sv * _silu(z_ref[:, cs].astype(F32))).astype(o_ref.dtype)


def _sg_gate(p, ln_g, ln_b, w_s, b_s_t):
    rows = p.shape[0]
    blk = lambda j: pl.BlockSpec((SG_CHUNK, SG_HALF), lambda i: (i, j))
    return pl.pallas_call(
        _sg_kernel,
        out_shape=jax.ShapeDtypeStruct((rows, SG_HALF), BF16),
        grid=(rows // SG_CHUNK,),
        in_specs=[blk(0), blk(1), blk(2),
                  pl.BlockSpec((1, SG_HALF), lambda i: (0, 0)),
                  pl.BlockSpec((1, SG_HALF), lambda i: (0, 0)),
                  pl.BlockSpec((SG_GROUPS, SG_CHUNK, SG_CHUNK), lambda i: (0, 0, 0)),
                  pl.BlockSpec((SG_CHUNK, SG_GROUPS), lambda i: (0, 0))],
        out_specs=blk(0),
        compiler_params=_cparams(("parallel",)),
        name="sg_gate",
    )(p, p, p, ln_g.reshape(1, SG_HALF), ln_b.reshape(1, SG_HALF), w_s, b_s_t)


def kernel(x, c, ctx, c_ctx, norm_g, ada_w, ada_b, na_w_in, na_rpb, na_w_out,
           ssd_w_in, ssd_conv_w, ssd_conv_b, ssd_dt_bias, ssd_a_log, ssd_d_skip,
           ssd_norm_g, ssd_w_out, sg_w_in, sg_ln_g, sg_ln_b, sg_w_s, sg_b_s, sg_w_out,
           final_norm_g):
    d = D_MODEL
    c_rows = jnp.concatenate([c, c_ctx[None], jnp.zeros((MOD_ROWS - BATCH - 1, d), F32)], axis=0)
    mods = _ada_all(c_rows, ada_w, ada_b)

    h, n = _norm_mod(ctx, x, norm_g[0], mods[0])

    na_w_out_b = _outproj_weights(na_w_out)
    ssd_w_out_b = _outproj_weights(ssd_w_out, ssd_norm_g)
    sg_w_out_b = _outproj_weights(sg_w_out)

    for i in range(DEPTH):
        kind, j = i % N_MIXERS, i // N_MIXERS
        need_ctx = i < DEPTH - 1
        if kind == 0:
            p = _proj(n, na_w_in, j, n=4 * d, out_dtype=BF16, scale_cols=d, scale=NA_HEAD_DIM ** -0.5 * LOG2E)
            y = _na_attention(p.reshape(BATCH, LTOT, 4 * d), na_rpb[j], need_ctx).reshape(ROWS, d)
            w_out = na_w_out_b
        elif kind == 1:
            p = _proj(n, ssd_w_in, j, n=SSD_MAIN, out_dtype=BF16).reshape(BATCH, LTOT, SSD_MAIN)
            dt_raw = _proj(n, ssd_w_in, j, n=2 * SSD_HEADS, out_dtype=F32, tn=2 * SSD_HEADS, col0=SSD_MAIN)
            xbc = _ssd_conv(p, ssd_conv_w[j], ssd_conv_b[j])
            y = _ssd_scan(xbc, p, dt_raw, ssd_dt_bias[j], ssd_a_log[j], ssd_d_skip[j]).reshape(ROWS, SSD_D_INNER)
            w_out = ssd_w_out_b
        else:
            p = _proj(n, sg_w_in, j, n=3 * SG_HALF, out_dtype=BF16, gelu_from_col=SG_HALF)
            y = _sg_gate(p, sg_ln_g[j], sg_ln_b[j], sg_w_s[j].astype(BF16), jnp.transpose(sg_b_s[j]))
            w_out = sg_w_out_b

        rms_a = kind == 1
        if need_ctx:
            h, n = _outproj(y, w_out, j, h, mods, i, norm_g[i + 1], mods[i + 1], final=False, rms_a=rms_a)
        else:
            out = _outproj(y, w_out, j, h, mods, i, final_norm_g, mods[i], final=True, rms_a=rms_a)

    return out
```

```python
import functools
import math

import numpy as np
import jax
import jax.numpy as jnp
from jax import lax
from jax.experimental import pallas as pl
from jax.experimental.pallas import tpu as pltpu

D_MODEL = 2048
BATCH = 4
SEQ = 2048
DEPTH = 4
GRID_W = 64
CTX_LEN = 256
N_MIXERS = 3
EPS = 1e-6
LTOT = SEQ + CTX_LEN
ROWS = BATCH * LTOT

NA_HEADS = 16
NA_HEAD_DIM = D_MODEL // NA_HEADS
NA_WIN_ROWS = 8
NA_WIN_COLS = 16
NA_ROWS = SEQ // GRID_W
NA_QROWS = 4
NA_KROWS = NA_QROWS + NA_WIN_ROWS
NA_TQ = NA_QROWS * GRID_W
NA_TK = NA_KROWS * GRID_W
NA_TILES = NA_ROWS // NA_QROWS
NA_NDR = 2 * NA_WIN_ROWS - 1

SSD_D_INNER = 2 * D_MODEL
SSD_HEAD_DIM = 64
SSD_HEADS = SSD_D_INNER // SSD_HEAD_DIM
SSD_GROUPS = 8
SSD_STATE = 128
SSD_CHUNK = 128
SSD_GN = SSD_GROUPS * SSD_STATE
SSD_CONV_CH = SSD_D_INNER + 2 * SSD_GN
SSD_MAIN = SSD_D_INNER + SSD_CONV_CH
SSD_HPG = SSD_HEADS // SSD_GROUPS
SSD_GW = SSD_HPG * SSD_HEAD_DIM
SSD_NCHUNK = LTOT // SSD_CHUNK
SSD_CCHUNK = CTX_LEN // SSD_CHUNK
SSD_DL = 2 * SSD_HPG

SG_HALF = 3 * D_MODEL
SG_GROUPS = 16
SG_CHUNK = 128
SG_GW = SG_HALF // SG_GROUPS

NEG = -1e30
VMEM_LIMIT = 56 * 1024 * 1024
MOD_ROWS = 8

PROJ_TM, PROJ_TN = 1536, 1024
OUT_TM = 768
LOG2E = math.log2(math.e)

F32 = jnp.float32
BF16 = jnp.bfloat16
_NT = (((1,), (1,)), ((), ()))


def _cparams(sem):
    return pltpu.CompilerParams(dimension_semantics=sem, vmem_limit_bytes=VMEM_LIMIT)


def _silu(x):
    return x * (1.0 / (1.0 + jnp.exp2(x * -LOG2E)))


def _gelu_tanh(x):
    c = math.sqrt(2.0 / math.pi)
    return 0.5 * x * (1.0 + jnp.tanh(c * (x + 0.044715 * (x * x * x))))


def _softplus(x):
    return jnp.maximum(x, 0.0) + jnp.log(1.0 + jnp.exp(-jnp.abs(x)))


def _ada_kernel(c_ref, w_ref, b_ref, o_ref):
    a = _silu(c_ref[...]).astype(BF16)
    o_ref[0] = jnp.dot(a, w_ref[0].astype(BF16), preferred_element_type=F32) + b_ref[0]


def _ada_all(c_rows, ada_w, ada_b):
    tn = 1024
    return pl.pallas_call(
        _ada_kernel,
        out_shape=jax.ShapeDtypeStruct((DEPTH, MOD_ROWS, 3 * D_MODEL), F32),
        grid=(DEPTH, 3 * D_MODEL // tn),
        in_specs=[pl.BlockSpec((MOD_ROWS, D_MODEL), lambda l, j: (0, 0)),
                  pl.BlockSpec((1, D_MODEL, tn), lambda l, j: (l, 0, j)),
                  pl.BlockSpec((1, 1, tn), lambda l, j: (l, 0, j))],
        out_specs=pl.BlockSpec((1, MOD_ROWS, tn), lambda l, j: (l, 0, j)),
        compiler_params=_cparams(("arbitrary", "arbitrary")),
        name="ada_mod",
    )(c_rows, ada_w, ada_b.reshape(DEPTH, 1, 3 * D_MODEL))


def _mod_rows(mod_ref, row):
    m = mod_ref[pl.ds(row, 1), :]
    return m[:, :D_MODEL], m[:, D_MODEL:2 * D_MODEL], m[:, 2 * D_MODEL:]


def _norm_mod_kernel(ctx_ref, x_ref, g_ref, mod_ref, h_ref, o_ref):
    b = pl.program_id(0)
    r = pl.program_id(1)
    is_ctx = r == LTOT // CTX_LEN - 1
    shift, scale, _ = _mod_rows(mod_ref, jnp.where(is_ctx, BATCH, b))
    x = jnp.where(is_ctx, ctx_ref[0], x_ref[0])
    h_ref[...] = x
    y = x * lax.rsqrt(jnp.mean(x * x, axis=-1, keepdims=True) + EPS) * g_ref[...]
    o_ref[...] = (y * (1.0 + scale) + shift).astype(o_ref.dtype)


def _norm_mod(ctx, x, g, mod):
    per = LTOT // CTX_LEN
    rows = pl.BlockSpec((CTX_LEN, D_MODEL), lambda b, r: (b * per + r, 0))
    return pl.pallas_call(
        _norm_mod_kernel,
        out_shape=(jax.ShapeDtypeStruct((ROWS, D_MODEL), F32), jax.ShapeDtypeStruct((ROWS, D_MODEL), BF16)),
        grid=(BATCH, per),
        in_specs=[pl.BlockSpec((1, CTX_LEN, D_MODEL), lambda b, r: (b, 0, 0)),
                  pl.BlockSpec((1, CTX_LEN, D_MODEL), lambda b, r: (b, jnp.minimum(r, per - 2), 0)),
                  pl.BlockSpec((1, D_MODEL), lambda b, r: (0, 0)),
                  pl.BlockSpec((MOD_ROWS, 3 * D_MODEL), lambda b, r: (0, 0))],
        out_specs=(rows, rows),
        compiler_params=_cparams(("parallel", "parallel")),
        name="norm_mod",
    )(ctx, x, g.reshape(1, D_MODEL), mod)


def _proj_kernel(a_ref, w_ref, o_ref, wb_ref, *, gelu_from, scale_tiles, scale):
    j = pl.program_id(0)
    i = pl.program_id(1)

    @pl.when(i == 0)
    def _():
        w = w_ref[0]
        if scale_tiles:
            w = w * jnp.where(j < scale_tiles, scale, 1.0)
        wb_ref[...] = w.astype(BF16)

    acc = jnp.dot(a_ref[...], wb_ref[...], preferred_element_type=F32)
    if gelu_from is None:
        o_ref[...] = acc.astype(o_ref.dtype)
    else:
        @pl.when(j >= gelu_from)
        def _():
            o_ref[...] = _gelu_tanh(acc).astype(o_ref.dtype)

        @pl.when(j < gelu_from)
        def _():
            o_ref[...] = acc.astype(o_ref.dtype)


def _proj(a, w, layer, *, n, out_dtype, tn=PROJ_TN, col0=0, gelu_from_col=None, scale_cols=0, scale=1.0):
    m, kdim = a.shape
    off = col0 // tn
    return pl.pallas_call(
        functools.partial(_proj_kernel,
                          gelu_from=None if gelu_from_col is None else gelu_from_col // tn,
                          scale_tiles=scale_cols // tn, scale=scale),
        out_shape=jax.ShapeDtypeStruct((m, n), out_dtype),
        grid=(n // tn, m // PROJ_TM),
        in_specs=[pl.BlockSpec((PROJ_TM, kdim), lambda j, i: (i, 0)),
                  pl.BlockSpec((1, kdim, tn), lambda j, i: (layer, 0, off + j))],
        out_specs=pl.BlockSpec((PROJ_TM, tn), lambda j, i: (i, j)),
        scratch_shapes=[pltpu.VMEM((kdim, tn), BF16)],
        compiler_params=_cparams(("parallel", "arbitrary")),
        name="proj",
    )(a, w)


def _outproj_kernel(a_ref, w_ref, h_ref, gate_ref, ng_ref, nmod_ref, *rest, nj, tn, final, resident, rms_a):
    if final:
        n_ref, hs_ref, ss_ref, ra_ref = rest
        hn_ref = None
    else:
        hn_ref, n_ref, hs_ref, ss_ref, ra_ref = rest
    i = pl.program_id(0)
    j = pl.program_id(1)

    if rms_a:
        @pl.when(j == 0)
        def _():
            kdim = a_ref.shape[1]
            ssq = jnp.zeros((OUT_TM, 1), F32)
            for c0 in range(0, kdim, 512):
                af = a_ref[:, c0:c0 + 512].astype(F32)
                ssq = ssq + jnp.sum(af * af, axis=-1, keepdims=True)
            ra_ref[...] = lax.rsqrt(ssq * (1.0 / kdim) + EPS)
    per = LTOT // OUT_TM
    b = i // per
    last = i % per == per - 1
    top = slice(0, OUT_TM - CTX_LEN)
    bot = slice(OUT_TM - CTX_LEN, OUT_TM)
    if final:
        n_ref = n_ref.at[0]

    def rows(ref, cols):
        lat = ref[pl.ds(b, 1), cols]
        return lat, jnp.where(last, ref[BATCH:BATCH + 1, cols], lat)

    w = w_ref[0, j] if resident else w_ref[0, 0]
    acc = jnp.dot(a_ref[...], w, preferred_element_type=F32)
    if rms_a:
        acc = acc * ra_ref[...]
    g_top, g_bot = rows(gate_ref.at[0], slice(None))
    hn_t = h_ref[top, :] + g_top * acc[top, :]
    hn_b = h_ref[bot, :] + g_bot * acc[bot, :]
    hs_ref[j, top, :] = hn_t
    hs_ref[j, bot, :] = hn_b
    if not final:
        hn_ref[top, :] = hn_t
        hn_ref[bot, :] = hn_b
    sq_t = jnp.sum(hn_t * hn_t, axis=-1, keepdims=True)
    sq_b = jnp.sum(hn_b * hn_b, axis=-1, keepdims=True)

    @pl.when(j == 0)
    def _():
        ss_ref[top, :] = sq_t
        ss_ref[bot, :] = sq_b

    @pl.when(j > 0)
    def _():
        ss_ref[top, :] += sq_t
        ss_ref[bot, :] += sq_b

    @pl.when(j == nj - 1)
    def _():
        inv = lax.rsqrt(ss_ref[...] * (1.0 / D_MODEL) + EPS)
        for jj in range(nj):
            cs = slice(jj * tn, (jj + 1) * tn)
            y = hs_ref[jj] * inv
            if final:
                n_ref[:, cs] = y * ng_ref[:, cs]
            else:
                sh_top, sh_bot = rows(nmod_ref, cs)
                sc_top, sc_bot = rows(nmod_ref, slice(D_MODEL + jj * tn, D_MODEL + (jj + 1) * tn))
                g_top = ng_ref[:, cs] * (1.0 + sc_top)
                g_bot = ng_ref[:, cs] * (1.0 + sc_bot)
                n_ref[top, cs] = (y[top, :] * g_top + sh_top).astype(n_ref.dtype)
                n_ref[bot, cs] = (y[bot, :] * g_bot + sh_bot).astype(n_ref.dtype)


def _outproj_rows_kernel(a_ref, w_ref, h_ref, gate_ref, ng_ref, nmod_ref, *rest, nj, tn, final):
    if final:
        n_ref, keep_ref = rest
        n_ref = n_ref.at[0]
    else:
        keep_ref, n_ref = rest
    i = pl.program_id(0)
    per = LTOT // OUT_TM
    b = i // per
    last = i % per == per - 1
    top = slice(0, OUT_TM - CTX_LEN)
    bot = slice(OUT_TM - CTX_LEN, OUT_TM)

    def rows(ref, cols):
        lat = ref[pl.ds(b, 1), cols]
        return lat, jnp.where(last, ref[BATCH:BATCH + 1, cols], lat)

    a = a_ref[...]
    ssq_t = jnp.zeros((OUT_TM - CTX_LEN, 1), F32)
    ssq_b = jnp.zeros((CTX_LEN, 1), F32)
    for jj in range(nj):
        cs = slice(jj * tn, (jj + 1) * tn)
        acc = jnp.dot(a, w_ref[0, jj], preferred_element_type=F32)
        g_top, g_bot = rows(gate_ref.at[0], cs)
        hn_t = h_ref[top, cs] + g_top * acc[top, :]
        hn_b = h_ref[bot, cs] + g_bot * acc[bot, :]
        keep_ref[top, cs] = hn_t
        keep_ref[bot, cs] = hn_b
        ssq_t = ssq_t + jnp.sum(hn_t * hn_t, axis=-1, keepdims=True)
        ssq_b = ssq_b + jnp.sum(hn_b * hn_b, axis=-1, keepdims=True)
    inv_t = lax.rsqrt(ssq_t * (1.0 / D_MODEL) + EPS)
    inv_b = lax.rsqrt(ssq_b * (1.0 / D_MODEL) + EPS)
    for jj in range(nj):
        cs = slice(jj * tn, (jj + 1) * tn)
        if final:
            n_ref[top, cs] = keep_ref[top, cs] * inv_t * ng_ref[:, cs]
            n_ref[bot, cs] = keep_ref[bot, cs] * inv_b * ng_ref[:, cs]
        else:
            sh_top, sh_bot = rows(nmod_ref, cs)
            sc_top, sc_bot = rows(nmod_ref, slice(D_MODEL + jj * tn, D_MODEL + (jj + 1) * tn))
            n_ref[top, cs] = (keep_ref[top, cs] * inv_t * (ng_ref[:, cs] * (1.0 + sc_top)) + sh_top).astype(n_ref.dtype)
            n_ref[bot, cs] = (keep_ref[bot, cs] * inv_b * (ng_ref[:, cs] * (1.0 + sc_bot)) + sh_bot).astype(n_ref.dtype)


def _outproj_tn(kdim):
    return 512 if kdim <= 2 * D_MODEL else 256


def _wtile_kernel(w_ref, *rest):
    if len(rest) == 2:
        g_ref, o_ref = rest
        o_ref[0, 0] = (w_ref[0] * g_ref[0]).astype(o_ref.dtype)
    else:
        o_ref, = rest
        o_ref[0, 0] = w_ref[0].astype(o_ref.dtype)


def _outproj_weights(w, row_gain=None):
    nl, kdim, _ = w.shape
    tn = _outproj_tn(kdim)
    in_specs = [pl.BlockSpec((1, kdim, tn), lambda l, j: (l, 0, j))]
    args = [w]
    if row_gain is not None:
        in_specs.append(pl.BlockSpec((1, kdim, 1), lambda l, j: (l, 0, 0)))
        args.append(row_gain.reshape(nl, kdim, 1))
    return pl.pallas_call(
        _wtile_kernel,
        out_shape=jax.ShapeDtypeStruct((nl, D_MODEL // tn, kdim, tn), BF16),
        grid=(nl, D_MODEL // tn),
        in_specs=in_specs,
        out_specs=pl.BlockSpec((1, 1, kdim, tn), lambda l, j: (l, j, 0, 0)),
        compiler_params=_cparams(("parallel", "parallel")),
        name="wtile",
    )(*args)


def _outproj(a, w, layer, h, mods, mod_layer, next_g, next_mod, *, final, rms_a=False):
    m, kdim = a.shape
    tn = _outproj_tn(kdim)
    nj = D_MODEL // tn
    gate_off = 2 * D_MODEL // tn
    resident = kdim <= 2 * D_MODEL
    if resident:
        w_spec = pl.BlockSpec((1, nj, kdim, tn), lambda i, j: (layer, 0, 0, 0), pipeline_mode=pl.Buffered(1))
    else:
        w_spec = pl.BlockSpec((1, 1, kdim, tn), lambda i, j: (layer, j, 0, 0))
    if kdim <= D_MODEL and not rms_a:
        per = LTOT // OUT_TM
        row = pl.BlockSpec((OUT_TM, D_MODEL), lambda i: (i, 0))
        if final:
            out_shape = jax.ShapeDtypeStruct((BATCH, SEQ, D_MODEL), F32)
            out_specs = pl.BlockSpec((1, OUT_TM, D_MODEL), lambda i: (i // per, i % per, 0))
            scratch = [pltpu.VMEM((OUT_TM, D_MODEL), F32)]
        else:
            out_shape = (jax.ShapeDtypeStruct((m, D_MODEL), F32), jax.ShapeDtypeStruct((m, D_MODEL), BF16))
            out_specs = (row, row)
            scratch = []
        return pl.pallas_call(
            functools.partial(_outproj_rows_kernel, nj=nj, tn=tn, final=final),
            out_shape=out_shape,
            grid=(m // OUT_TM,),
            in_specs=[pl.BlockSpec((OUT_TM, kdim), lambda i: (i, 0)),
                      pl.BlockSpec((1, nj, kdim, tn), lambda i: (layer, 0, 0, 0), pipeline_mode=pl.Buffered(1)),
                      row,
                      pl.BlockSpec((1, MOD_ROWS, D_MODEL), lambda i: (mod_layer, 0, 2)),
                      pl.BlockSpec((1, D_MODEL), lambda i: (0, 0)),
                      pl.BlockSpec((MOD_ROWS, 3 * D_MODEL), lambda i: (0, 0))],
            out_specs=out_specs,
            scratch_shapes=scratch,
            compiler_params=_cparams(("parallel",)),
            name="outproj_rows",
        )(a, w, h, mods, next_g.reshape(1, D_MODEL), next_mod)
    tile = pl.BlockSpec((OUT_TM, tn), lambda i, j: (i, j))
    full = pl.BlockSpec((OUT_TM, D_MODEL), lambda i, j: (i, 0))
    const = lambda i, j: (0, 0)
    if final:
        per = LTOT // OUT_TM
        out_shape = jax.ShapeDtypeStruct((BATCH, SEQ, D_MODEL), F32)
        out_specs = pl.BlockSpec((1, OUT_TM, D_MODEL), lambda i, j: (i // per, i % per, 0))
    else:
        out_shape = (jax.ShapeDtypeStruct((m, D_MODEL), F32), jax.ShapeDtypeStruct((m, D_MODEL), BF16))
        out_specs = (tile, full)
    return pl.pallas_call(
        functools.partial(_outproj_kernel, nj=nj, tn=tn, final=final, resident=resident, rms_a=rms_a),
        out_shape=out_shape,
        grid=(m // OUT_TM, nj),
        in_specs=[pl.BlockSpec((OUT_TM, kdim), lambda i, j: (i, 0)),
                  w_spec,
                  tile,
                  pl.BlockSpec((1, MOD_ROWS, tn), lambda i, j: (mod_layer, 0, gate_off + j)),
                  pl.BlockSpec((1, D_MODEL), const),
                  pl.BlockSpec((MOD_ROWS, 3 * D_MODEL), const)],
        out_specs=out_specs,
        scratch_shapes=[pltpu.VMEM((nj, OUT_TM, tn), F32),
                        pltpu.VMEM((OUT_TM, 1), F32),
                        pltpu.VMEM((OUT_TM, 1), F32)],
        compiler_params=_cparams(("parallel", "arbitrary")),
        name="outproj",
    )(a, w, h, mods, next_g.reshape(1, D_MODEL), next_mod)


def _na_tile_patterns():
    pats = []
    for t in (0, 1, NA_TILES - 1):
        ws = min(max(NA_QROWS * t - NA_WIN_ROWS // 2, 0), NA_ROWS - NA_KROWS)
        pat = np.full((NA_QROWS, NA_KROWS), NA_NDR, dtype=np.int64)
        for a in range(NA_QROWS):
            r = NA_QROWS * t + a
            rs = min(max(r - NA_WIN_ROWS // 2, 0), NA_ROWS - NA_WIN_ROWS)
            for jj in range(NA_KROWS):
                kabs = ws + jj
                if rs <= kabs < rs + NA_WIN_ROWS:
                    pat[a, jj] = kabs - r + NA_WIN_ROWS - 1
        pats.append(pat)
    return pats


def _na_build_bias(rpb_ref, slab_ref, bias_ref):
    w2 = 2 * GRID_W
    qc = lax.broadcasted_iota(jnp.int32, (GRID_W, w2), 0)
    lane = lax.broadcasted_iota(jnp.int32, (GRID_W, w2), 1)
    left = lane < GRID_W
    kc = jnp.where(left, lane, lane - GRID_W)
    c_start = jnp.clip(qc - NA_WIN_COLS // 2, 0, GRID_W - NA_WIN_COLS)
    col_ok = jnp.logical_and(kc >= c_start, kc < c_start + NA_WIN_COLS)
    shift = w2 - (NA_WIN_COLS - 1)
    for dr in range(NA_NDR):
        row = jnp.broadcast_to(rpb_ref[0, dr:dr + 1, :] * LOG2E, (GRID_W, w2))
        lo = pltpu.roll(row, shift, 1, stride=1, stride_axis=0)
        hi = pltpu.roll(row, (shift + GRID_W) % w2, 1, stride=1, stride_axis=0)
        slab_ref[dr] = jnp.where(col_ok, jnp.where(left, lo, hi), NEG)
    slab_ref[NA_NDR] = jnp.full((GRID_W, w2), NEG, F32)
    for p, pat in enumerate(_na_tile_patterns()):
        for a in range(NA_QROWS):
            for jp in range(NA_KROWS // 2):
                blk = jnp.where(left, slab_ref[int(pat[a, 2 * jp])], slab_ref[int(pat[a, 2 * jp + 1])])
                bias_ref[p, a * GRID_W:(a + 1) * GRID_W, jp * w2:(jp + 1) * w2] = blk


def _na_kernel(q_ref, k_ref, v_ref, z_ref, rpb_ref, o_ref, bias_ref, slab_ref, *, need_ctx):
    @pl.when(pl.program_id(1) == 0)
    def _():
        _na_build_bias(rpb_ref, slab_ref, bias_ref)

    kc = k_ref[0, SEQ:LTOT, :]
    vc = v_ref[0, SEQ:LTOT, :]

    def finish(o, l, r0):
        z = z_ref[0, pl.ds(r0, NA_TQ), :].astype(F32)
        o_ref[0, pl.ds(r0, NA_TQ), :] = (o * (1.0 / l) * _silu(z)).astype(o_ref.dtype)

    if need_ctx:
        s = lax.dot_general(q_ref[0, SEQ:LTOT, :], kc, _NT, preferred_element_type=F32)
        p = jnp.exp2(s - jnp.max(s, axis=-1, keepdims=True))
        finish(jnp.dot(p.astype(BF16), vc, preferred_element_type=F32), jnp.sum(p, axis=-1, keepdims=True), SEQ)
    else:
        o_ref[0, SEQ:LTOT, :] = jnp.zeros((CTX_LEN, NA_HEAD_DIM), o_ref.dtype)

    def key_start(t):
        ws = jnp.clip(NA_QROWS * t - NA_WIN_ROWS // 2, 0, NA_ROWS - NA_KROWS)
        return pl.multiple_of(ws * GRID_W, GRID_W)

    def scores(t):
        q = q_ref[0, pl.ds(pl.multiple_of(t * NA_TQ, NA_TQ), NA_TQ), :]
        pat = jnp.where(t == 0, 0, jnp.where(t == NA_TILES - 1, 2, 1))
        s1 = lax.dot_general(q, k_ref[0, pl.ds(key_start(t), NA_TK), :], _NT,
                             preferred_element_type=F32) + bias_ref[pat]
        return s1, lax.dot_general(q, kc, _NT, preferred_element_type=F32)

    def tile(t, carry, has_next=True):
        s1, s2 = carry
        nxt = scores(t + 1) if has_next else None
        m = jnp.maximum(jnp.max(s1, axis=-1, keepdims=True), jnp.max(s2, axis=-1, keepdims=True))
        p1 = jnp.exp2(s1 - m)
        p2 = jnp.exp2(s2 - m)
        l = jnp.sum(p1, axis=-1, keepdims=True) + jnp.sum(p2, axis=-1, keepdims=True)
        vw = v_ref[0, pl.ds(key_start(t), NA_TK), :]
        o = (jnp.dot(p1.astype(BF16), vw, preferred_element_type=F32)
             + jnp.dot(p2.astype(BF16), vc, preferred_element_type=F32))
        finish(o, l, pl.multiple_of(t * NA_TQ, NA_TQ))
        return nxt

    last = lax.fori_loop(0, NA_TILES - 1, tile, scores(0), unroll=2)
    tile(NA_TILES - 1, last, has_next=False)


def _na_attention(p, rpb, need_ctx):
    hd = NA_HEAD_DIM
    w2 = 2 * GRID_W
    rpb_pad = jnp.pad(rpb, ((0, 0), (0, NA_NDR + 1 - rpb.shape[1]), (0, w2 - rpb.shape[2])))
    blk = lambda off: pl.BlockSpec((1, LTOT, hd), lambda h, b: (b, 0, off + h))
    return pl.pallas_call(
        functools.partial(_na_kernel, need_ctx=need_ctx),
        out_shape=jax.ShapeDtypeStruct((BATCH, LTOT, D_MODEL), BF16),
        grid=(NA_HEADS, BATCH),
        in_specs=[blk(0), blk(NA_HEADS), blk(2 * NA_HEADS), blk(3 * NA_HEADS),
                  pl.BlockSpec((1, NA_NDR + 1, w2), lambda h, b: (h, 0, 0))],
        out_specs=blk(0),
        scratch_shapes=[pltpu.VMEM((3, NA_TQ, NA_TK), F32),
                        pltpu.VMEM((NA_NDR + 1, GRID_W, w2), F32)],
        compiler_params=_cparams(("parallel", "arbitrary")),
        name="na_attention",
    )(p, p, p, p, rpb_pad)


def _conv_kernel(u_ref, w_ref, b_ref, o_ref):
    u = u_ref[0].astype(F32)
    row = lax.broadcasted_iota(jnp.int32, u.shape, 0)
    seg_first = jnp.logical_or(row == 0, row == SEQ)
    seg_last = jnp.logical_or(row == SEQ - 1, row == LTOT - 1)
    up = jnp.where(seg_first, 0.0, pltpu.roll(u, 1, 0))
    un = jnp.where(seg_last, 0.0, pltpu.roll(u, LTOT - 1, 0))
    w = w_ref[...]
    y = w[0:1] * up + w[1:2] * u + w[2:3] * un + b_ref[...]
    o_ref[0] = _silu(y).astype(o_ref.dtype)


def _ssd_conv(p, conv_w, conv_b):
    tc = 512
    off = SSD_D_INNER // tc
    return pl.pallas_call(
        _conv_kernel,
        out_shape=jax.ShapeDtypeStruct((BATCH, LTOT, SSD_CONV_CH), BF16),
        grid=(BATCH, SSD_CONV_CH // tc),
        in_specs=[pl.BlockSpec((1, LTOT, tc), lambda b, j: (b, 0, off + j)),
                  pl.BlockSpec((3, tc), lambda b, j: (0, j)),
                  pl.BlockSpec((1, tc), lambda b, j: (0, j))],
        out_specs=pl.BlockSpec((1, LTOT, tc), lambda b, j: (b, 0, j)),
        compiler_params=_cparams(("parallel", "parallel")),
        name="ssd_conv",
    )(p, conv_w, conv_b.reshape(1, SSD_CONV_CH))


def _bf16_pieces(v):
    hi = v.astype(BF16)
    r1 = v - hi.astype(F32)
    mid = r1.astype(BF16)
    lo = (r1 - mid.astype(F32)).astype(BF16)
    return hi, mid, lo


def _split3(v):
    hi, mid, lo = _bf16_pieces(v)
    lane = lax.broadcasted_iota(jnp.int32, v.shape, 1)
    return jnp.where(lane < SSD_DL, hi, jnp.where(lane < 2 * SSD_DL, mid, lo))


def _ssd_kernel(x_ref, b_ref, c_ref, z_ref, dtc_ref, dtr_ref, pc_ref, pr_ref, dskip_ref,
                y_ref, yacc_ref, s_ref):
    q = SSD_CHUNK
    hp = SSD_HPG
    li = lax.broadcasted_iota(jnp.int32, (q, q), 0)
    si = lax.broadcasted_iota(jnp.int32, (q, q), 1)
    lower = li >= si
    upper = li <= si
    tri_lo = lower.astype(BF16)
    tri_up = upper.astype(BF16)
    tri_lanes = (jnp.concatenate([tri_lo] * 3, axis=1), jnp.concatenate([tri_up] * 3, axis=1))
    tri_rows = (jnp.concatenate([tri_up] * 3, axis=0), jnp.concatenate([tri_lo] * 3, axis=0))
    left = si < SSD_HEAD_DIM

    def expand_matrix(d):
        r = lax.broadcasted_iota(jnp.int32, (3 * SSD_DL, SSD_GW), 0) % SSD_DL
        c = lax.broadcasted_iota(jnp.int32, (3 * SSD_DL, SSD_GW), 1) // SSD_HEAD_DIM
        return (r == d * hp + c).astype(BF16)

    expand = (expand_matrix(0), expand_matrix(1))

    bias_c = pc_ref[0, 0:1, :]
    a_c = -jnp.exp(pc_ref[0, 1:2, :]) * LOG2E
    bias_r = pr_ref[0, :, 0:1]
    a_r = -jnp.exp(pr_ref[0, :, 1:2]) * LOG2E

    s_ref[...] = jnp.zeros_like(s_ref)

    def prep(c, d):
        r0 = pl.multiple_of(c * q, q)
        dtc = _softplus(dtc_ref[0, 0, pl.ds(r0, q), :] + bias_c)
        acum_c = jnp.dot(tri_lanes[d], jnp.concatenate(_bf16_pieces(dtc * a_c), axis=0),
                         preferred_element_type=F32)
        dtr = _softplus(dtr_ref[0, 0, c] + bias_r)
        acum_r = jnp.dot(jnp.concatenate(_bf16_pieces(dtr * a_r), axis=1), tri_rows[d],
                         preferred_element_type=F32)
        tot_c = acum_c[q - 1:q, :] if d == 0 else acum_c[0:1, :]
        ea_c = jnp.exp2(acum_c)
        dw_c = dtc * jnp.exp2(tot_c - acum_c)
        ldt = jnp.log2(dtr)
        return _split3(ea_c), _split3(dw_c), acum_c, acum_r - ldt, ldt

    def chunk(c, d, prepared):
        ea_s, dw_s, acum_c, acum_r, ldt = prepared
        ea_x = jnp.dot(ea_s, expand[d], preferred_element_type=F32)
        dw_x = jnp.dot(dw_s, expand[d], preferred_element_type=F32)
        r0 = pl.multiple_of(c * q, q)
        mask = lower if d == 0 else upper
        xb = x_ref[0, pl.ds(r0, q), :]
        bm = b_ref[0, pl.ds(r0, q), :]
        cm = c_ref[0, pl.ds(r0, q), :]
        etot_x = ea_x[q - 1:q, :] if d == 0 else ea_x[0:1, :]
        cb = jnp.where(mask, lax.dot_general(cm, bm, _NT, preferred_element_type=F32), 0.0)
        bt = bm.astype(F32).T.astype(BF16)
        sprev = s_ref[d]
        yoff = jnp.dot(cm, sprev.astype(BF16), preferred_element_type=F32)
        s_ref[d] = sprev * etot_x + jnp.dot(bt, xb * dw_x.astype(BF16), preferred_element_type=F32)
        zero = jnp.zeros((q, q), BF16)
        for p in range(hp // 2):
            cs = slice(p * q, (p + 1) * q)
            m_pair = []
            for ln in (d * hp + 2 * p, d * hp + 2 * p + 1):
                e = jnp.minimum(acum_c[:, ln:ln + 1] - acum_r[ln:ln + 1, :], ldt[ln:ln + 1, :])
                m_pair.append((cb * jnp.exp2(e)).astype(BF16))
            x_p = xb[:, cs]
            rhs = jnp.concatenate([jnp.where(left, x_p, zero), jnp.where(left, zero, x_p)], axis=0)
            yd = jnp.dot(jnp.concatenate(m_pair, axis=1), rhs, preferred_element_type=F32)
            yacc_ref[d, pl.ds(r0, q), cs] = yd + ea_x[:, cs] * yoff[:, cs]

    lat_chunks = SSD_NCHUNK - SSD_CCHUNK

    def fwd_chunk(k):
        return jnp.where(k < SSD_CCHUNK, lat_chunks + k, k - SSD_CCHUNK)

    def bwd_chunk(k):
        return SSD_NCHUNK - 1 - k

    def step(k, carry):
        kn = jnp.minimum(k + 1, SSD_NCHUNK - 1)
        nxt = (prep(fwd_chunk(kn), 0), prep(bwd_chunk(kn), 1))
        chunk(fwd_chunk(k), 0, carry[0])
        chunk(bwd_chunk(k), 1, carry[1])
        return nxt

    lax.fori_loop(0, SSD_NCHUNK, step, (prep(fwd_chunk(0), 0), prep(bwd_chunk(0), 1)))

    y = yacc_ref[0] + yacc_ref[1] + x_ref[0].astype(F32) * dskip_ref[0]
    y_ref[0] = (y * _silu(z_ref[0].astype(F32))).astype(y_ref.dtype)


def _ssd_scan(xbc, p, dt_raw, dt_bias, a_log, d_skip):
    g, hp, dl = SSD_GROUPS, SSD_HPG, SSD_DL
    dt = dt_raw.reshape(BATCH, LTOT, 2, g, hp).transpose(0, 3, 1, 2, 4).reshape(BATCH, g, LTOT, dl)
    dt_col = jnp.tile(dt, (1, 1, 1, 3))
    dt_row = dt.reshape(BATCH, g, SSD_NCHUNK, SSD_CHUNK, dl).transpose(0, 1, 2, 4, 3)
    par = jnp.stack([dt_bias, a_log]).reshape(2, 2, g, hp).transpose(2, 0, 1, 3).reshape(g, 2, dl)
    par_col = jnp.pad(jnp.tile(par, (1, 1, 3)), ((0, 0), (0, 6), (0, 0)))
    par_row = jnp.pad(par.transpose(0, 2, 1), ((0, 0), (0, 0), (0, 126)))
    dskip = jnp.repeat(d_skip, SSD_HEAD_DIM).reshape(g, 1, SSD_GW)
    xoff = SSD_D_INNER // SSD_STATE
    return pl.pallas_call(
        _ssd_kernel,
        out_shape=jax.ShapeDtypeStruct((BATCH, LTOT, SSD_D_INNER), BF16),
        grid=(BATCH, g),
        in_specs=[pl.BlockSpec((1, LTOT, SSD_GW), lambda b, j: (b, 0, j)),
                  pl.BlockSpec((1, LTOT, SSD_STATE), lambda b, j: (b, 0, xoff + j)),
                  pl.BlockSpec((1, LTOT, SSD_STATE), lambda b, j: (b, 0, xoff + g + j)),
                  pl.BlockSpec((1, LTOT, SSD_GW), lambda b, j: (b, 0, j)),
                  pl.BlockSpec((1, 1, LTOT, 3 * dl), lambda b, j: (b, j, 0, 0)),
                  pl.BlockSpec((1, 1, SSD_NCHUNK, dl, SSD_CHUNK), lambda b, j: (b, j, 0, 0, 0)),
                  pl.BlockSpec((1, 8, 3 * dl), lambda b, j: (j, 0, 0)),
                  pl.BlockSpec((1, dl, 128), lambda b, j: (j, 0, 0)),
                  pl.BlockSpec((1, 1, SSD_GW), lambda b, j: (j, 0, 0))],
        out_specs=pl.BlockSpec((1, LTOT, SSD_GW), lambda b, j: (b, 0, j)),
        scratch_shapes=[pltpu.VMEM((2, LTOT, SSD_GW), F32),
                        pltpu.VMEM((2, SSD_STATE, SSD_GW), F32)],
        compiler_params=_cparams(("parallel", "parallel")),
        name="ssd_scan",
    )(xbc, xbc, xbc, p, dt_col, dt_row, par_col, par_row, dskip)


def _sg_kernel(z_ref, u_ref, v_ref, g_ref, b_ref, ws_ref, bs_ref, o_ref):
    v = v_ref[...].astype(F32)
    mu = jnp.mean(v, axis=-1, keepdims=True)
    vc = v - mu
    var = jnp.mean(vc * vc, axis=-1, keepdims=True)
    vn = (vc * lax.rsqrt(var + EPS) * g_ref[...] + b_ref[...]).astype(BF16)
    for g in range(SG_GROUPS):
        cs = slice(g * SG_GW, (g + 1) * SG_GW)
        sv = jnp.dot(ws_ref[g], vn[:, cs], preferred_element_type=F32) + bs_ref[:, g:g + 1]
        o_ref[:, cs] = (u_ref[:, cs].astype(F32) * sv * _silu(z_ref[:, cs].astype(F32))).astype(o_ref.dtype)


def _sg_gate(p, ln_g, ln_b, w_s, b_s_t):
    rows = p.shape[0]
    blk = lambda j: pl.BlockSpec((SG_CHUNK, SG_HALF), lambda i: (i, j))
    return pl.pallas_call(
        _sg_kernel,
        out_shape=jax.ShapeDtypeStruct((rows, SG_HALF), BF16),
        grid=(rows // SG_CHUNK,),
        in_specs=[blk(0), blk(1), blk(2),
                  pl.BlockSpec((1, SG_HALF), lambda i: (0, 0)),
                  pl.BlockSpec((1, SG_HALF), lambda i: (0, 0)),
                  pl.BlockSpec((SG_GROUPS, SG_CHUNK, SG_CHUNK), lambda i: (0, 0, 0)),
                  pl.BlockSpec((SG_CHUNK, SG_GROUPS), lambda i: (0, 0))],
        out_specs=blk(0),
        compiler_params=_cparams(("parallel",)),
        name="sg_gate",
    )(p, p, p, ln_g.reshape(1, SG_HALF), ln_b.reshape(1, SG_HALF), w_s, b_s_t)


def kernel(x, c, ctx, c_ctx, norm_g, ada_w, ada_b, na_w_in, na_rpb, na_w_out,
           ssd_w_in, ssd_conv_w, ssd_conv_b, ssd_dt_bias, ssd_a_log, ssd_d_skip,
           ssd_norm_g, ssd_w_out, sg_w_in, sg_ln_g, sg_ln_b, sg_w_s, sg_b_s, sg_w_out,
           final_norm_g):
    d = D_MODEL
    c_rows = jnp.concatenate([c, c_ctx[None], jnp.zeros((MOD_ROWS - BATCH - 1, d), F32)], axis=0)
    mods = _ada_all(c_rows, ada_w, ada_b)

    h, n = _norm_mod(ctx, x, norm_g[0], mods[0])

    na_w_out_b = _outproj_weights(na_w_out)
    ssd_w_out_b = _outproj_weights(ssd_w_out, ssd_norm_g)
    sg_w_out_b = _outproj_weights(sg_w_out)

    for i in range(DEPTH):
        kind, j = i % N_MIXERS, i // N_MIXERS
        need_ctx = i < DEPTH - 1
        if kind == 0:
            p = _proj(n, na_w_in, j, n=4 * d, out_dtype=BF16, scale_cols=d, scale=NA_HEAD_DIM ** -0.5 * LOG2E)
            y = _na_attention(p.reshape(BATCH, LTOT, 4 * d), na_rpb[j], need_ctx).reshape(ROWS, d)
            w_out = na_w_out_b
        elif kind == 1:
            p = _proj(n, ssd_w_in, j, n=SSD_MAIN, out_dtype=BF16).reshape(BATCH, LTOT, SSD_MAIN)
            dt_raw = _proj(n, ssd_w_in, j, n=2 * SSD_HEADS, out_dtype=F32, tn=2 * SSD_HEADS, col0=SSD_MAIN)
            xbc = _ssd_conv(p, ssd_conv_w[j], ssd_conv_b[j])
            y = _ssd_scan(xbc, p, dt_raw, ssd_dt_bias[j], ssd_a_log[j], ssd_d_skip[j]).reshape(ROWS, SSD_D_INNER)
            w_out = ssd_w_out_b
        else:
            p = _proj(n, sg_w_in, j, n=3 * SG_HALF, out_dtype=BF16, gelu_from_col=SG_HALF)
            y = _sg_gate(p, sg_ln_g[j], sg_ln_b[j], sg_w_s[j].astype(BF16), jnp.transpose(sg_b_s[j]))
            w_out = sg_w_out_b

        rms_a = kind == 1
        if need_ctx:
            h, n = _outproj(y, w_out, j, h, mods, i, norm_g[i + 1], mods[i + 1], final=False, rms_a=rms_a)
        else:
            out = _outproj(y, w_out, j, h, mods, i, final_norm_g, mods[i], final=True, rms_a=rms_a)

    return out
```

```python
import functools
import math

import numpy as np
import jax
import jax.numpy as jnp
from jax import lax
from jax.experimental import pallas as pl
from jax.experimental.pallas import tpu as pltpu

D_MODEL = 2048
BATCH = 4
SEQ = 2048
DEPTH = 4
GRID_W = 64
CTX_LEN = 256
N_MIXERS = 3
EPS = 1e-6
LTOT = SEQ + CTX_LEN
ROWS = BATCH * LTOT

NA_HEADS = 16
NA_HEAD_DIM = D_MODEL // NA_HEADS
NA_WIN_ROWS = 8
NA_WIN_COLS = 16
NA_ROWS = SEQ // GRID_W
NA_QROWS = 4
NA_KROWS = NA_QROWS + NA_WIN_ROWS
NA_TQ = NA_QROWS * GRID_W
NA_TK = NA_KROWS * GRID_W
NA_TILES = NA_ROWS // NA_QROWS
NA_NDR = 2 * NA_WIN_ROWS - 1

SSD_D_INNER = 2 * D_MODEL
SSD_HEAD_DIM = 64
SSD_HEADS = SSD_D_INNER // SSD_HEAD_DIM
SSD_GROUPS = 8
SSD_STATE = 128
SSD_CHUNK = 128
SSD_GN = SSD_GROUPS * SSD_STATE
SSD_CONV_CH = SSD_D_INNER + 2 * SSD_GN
SSD_MAIN = SSD_D_INNER + SSD_CONV_CH
SSD_HPG = SSD_HEADS // SSD_GROUPS
SSD_GW = SSD_HPG * SSD_HEAD_DIM
SSD_NCHUNK = LTOT // SSD_CHUNK
SSD_CCHUNK = CTX_LEN // SSD_CHUNK
SSD_DL = 2 * SSD_HPG

SG_HALF = 3 * D_MODEL
SG_GROUPS = 16
SG_CHUNK = 128
SG_GW = SG_HALF // SG_GROUPS

NEG = -1e30
VMEM_LIMIT = 56 * 1024 * 1024
MOD_ROWS = 8

PROJ_TM, PROJ_TN = 1536, 1024
OUT_TM = 768
LOG2E = math.log2(math.e)

F32 = jnp.float32
BF16 = jnp.bfloat16
_NT = (((1,), (1,)), ((), ()))


def _cparams(sem):
    return pltpu.CompilerParams(dimension_semantics=sem, vmem_limit_bytes=VMEM_LIMIT)


def _silu(x):
    return x * (1.0 / (1.0 + jnp.exp2(x * -LOG2E)))


def _gelu_tanh(x):
    c = math.sqrt(2.0 / math.pi)
    return 0.5 * x * (1.0 + jnp.tanh(c * (x + 0.044715 * (x * x * x))))


def _softplus(x):
    return jnp.maximum(x, 0.0) + jnp.log(1.0 + jnp.exp(-jnp.abs(x)))


def _ada_kernel(c_ref, w_ref, b_ref, o_ref):
    a = _silu(c_ref[...]).astype(BF16)
    o_ref[0] = jnp.dot(a, w_ref[0].astype(BF16), preferred_element_type=F32) + b_ref[0]


def _ada_all(c_rows, ada_w, ada_b):
    tn = 1024
    return pl.pallas_call(
        _ada_kernel,
        out_shape=jax.ShapeDtypeStruct((DEPTH, MOD_ROWS, 3 * D_MODEL), F32),
        grid=(DEPTH, 3 * D_MODEL // tn),
        in_specs=[pl.BlockSpec((MOD_ROWS, D_MODEL), lambda l, j: (0, 0)),
                  pl.BlockSpec((1, D_MODEL, tn), lambda l, j: (l, 0, j)),
                  pl.BlockSpec((1, 1, tn), lambda l, j: (l, 0, j))],
        out_specs=pl.BlockSpec((1, MOD_ROWS, tn), lambda l, j: (l, 0, j)),
        compiler_params=_cparams(("arbitrary", "arbitrary")),
        name="ada_mod",
    )(c_rows, ada_w, ada_b.reshape(DEPTH, 1, 3 * D_MODEL))


def _mod_rows(mod_ref, row):
    m = mod_ref[pl.ds(row, 1), :]
    return m[:, :D_MODEL], m[:, D_MODEL:2 * D_MODEL], m[:, 2 * D_MODEL:]


def _norm_mod_kernel(ctx_ref, x_ref, g_ref, mod_ref, h_ref, o_ref):
    b = pl.program_id(0)
    r = pl.program_id(1)
    is_ctx = r == LTOT // CTX_LEN - 1
    shift, scale, _ = _mod_rows(mod_ref, jnp.where(is_ctx, BATCH, b))
    x = jnp.where(is_ctx, ctx_ref[0], x_ref[0])
    h_ref[...] = x
    y = x * lax.rsqrt(jnp.mean(x * x, axis=-1, keepdims=True) + EPS) * g_ref[...]
    o_ref[...] = (y * (1.0 + scale) + shift).astype(o_ref.dtype)


def _norm_mod(ctx, x, g, mod):
    per = LTOT // CTX_LEN
    rows = pl.BlockSpec((CTX_LEN, D_MODEL), lambda b, r: (b * per + r, 0))
    return pl.pallas_call(
        _norm_mod_kernel,
        out_shape=(jax.ShapeDtypeStruct((ROWS, D_MODEL), F32), jax.ShapeDtypeStruct((ROWS, D_MODEL), BF16)),
        grid=(BATCH, per),
        in_specs=[pl.BlockSpec((1, CTX_LEN, D_MODEL), lambda b, r: (b, 0, 0)),
                  pl.BlockSpec((1, CTX_LEN, D_MODEL), lambda b, r: (b, jnp.minimum(r, per - 2), 0)),
                  pl.BlockSpec((1, D_MODEL), lambda b, r: (0, 0)),
                  pl.BlockSpec((MOD_ROWS, 3 * D_MODEL), lambda b, r: (0, 0))],
        out_specs=(rows, rows),
        compiler_params=_cparams(("parallel", "parallel")),
        name="norm_mod",
    )(ctx, x, g.reshape(1, D_MODEL), mod)


def _proj_kernel(a_ref, w_ref, o_ref, wb_ref, *, gelu_from, scale_tiles, scale):
    j = pl.program_id(0)
    i = pl.program_id(1)

    @pl.when(i == 0)
    def _():
        w = w_ref[0]
        if scale_tiles:
            w = w * jnp.where(j < scale_tiles, scale, 1.0)
        wb_ref[...] = w.astype(BF16)

    acc = jnp.dot(a_ref[...], wb_ref[...], preferred_element_type=F32)
    if gelu_from is None:
        o_ref[...] = acc.astype(o_ref.dtype)
    else:
        @pl.when(j >= gelu_from)
        def _():
            o_ref[...] = _gelu_tanh(acc).astype(o_ref.dtype)

        @pl.when(j < gelu_from)
        def _():
            o_ref[...] = acc.astype(o_ref.dtype)


def _proj(a, w, layer, *, n, out_dtype, tn=PROJ_TN, col0=0, gelu_from_col=None, scale_cols=0, scale=1.0):
    m, kdim = a.shape
    off = col0 // tn
    return pl.pallas_call(
        functools.partial(_proj_kernel,
                          gelu_from=None if gelu_from_col is None else gelu_from_col // tn,
                          scale_tiles=scale_cols // tn, scale=scale),
        out_shape=jax.ShapeDtypeStruct((m, n), out_dtype),
        grid=(n // tn, m // PROJ_TM),
        in_specs=[pl.BlockSpec((PROJ_TM, kdim), lambda j, i: (i, 0)),
                  pl.BlockSpec((1, kdim, tn), lambda j, i: (layer, 0, off + j))],
        out_specs=pl.BlockSpec((PROJ_TM, tn), lambda j, i: (i, j)),
        scratch_shapes=[pltpu.VMEM((kdim, tn), BF16)],
        compiler_params=_cparams(("parallel", "arbitrary")),
        name="proj",
    )(a, w)


def _outproj_kernel(a_ref, w_ref, h_ref, gate_ref, ng_ref, nmod_ref, *rest, nj, tn, final, resident, rms_a):
    if final:
        n_ref, hs_ref, ss_ref, ra_ref = rest
        hn_ref = None
    else:
        hn_ref, n_ref, hs_ref, ss_ref, ra_ref = rest
    i = pl.program_id(0)
    j = pl.program_id(1)

    if rms_a:
        @pl.when(j == 0)
        def _():
            kdim = a_ref.shape[1]
            ssq = jnp.zeros((OUT_TM, 1), F32)
            for c0 in range(0, kdim, 512):
                af = a_ref[:, c0:c0 + 512].astype(F32)
                ssq = ssq + jnp.sum(af * af, axis=-1, keepdims=True)
            ra_ref[...] = lax.rsqrt(ssq * (1.0 / kdim) + EPS)
    per = LTOT // OUT_TM
    b = i // per
    last = i % per == per - 1
    top = slice(0, OUT_TM - CTX_LEN)
    bot = slice(OUT_TM - CTX_LEN, OUT_TM)
    if final:
        n_ref = n_ref.at[0]

    def rows(ref, cols):
        lat = ref[pl.ds(b, 1), cols]
        return lat, jnp.where(last, ref[BATCH:BATCH + 1, cols], lat)

    w = w_ref[0, j] if resident else w_ref[0, 0]
    acc = jnp.dot(a_ref[...], w, preferred_element_type=F32)
    if rms_a:
        acc = acc * ra_ref[...]
    g_top, g_bot = rows(gate_ref.at[0], slice(None))
    hn_t = h_ref[top, :] + g_top * acc[top, :]
    hn_b = h_ref[bot, :] + g_bot * acc[bot, :]
    hs_ref[j, top, :] = hn_t
    hs_ref[j, bot, :] = hn_b
    if not final:
        hn_ref[top, :] = hn_t
        hn_ref[bot, :] = hn_b
    sq_t = jnp.sum(hn_t * hn_t, axis=-1, keepdims=True)
    sq_b = jnp.sum(hn_b * hn_b, axis=-1, keepdims=True)

    @pl.when(j == 0)
    def _():
        ss_ref[top, :] = sq_t
        ss_ref[bot, :] = sq_b

    @pl.when(j > 0)
    def _():
        ss_ref[top, :] += sq_t
        ss_ref[bot, :] += sq_b

    @pl.when(j == nj - 1)
    def _():
        inv = lax.rsqrt(ss_ref[...] * (1.0 / D_MODEL) + EPS)
        for jj in range(nj):
            cs = slice(jj * tn, (jj + 1) * tn)
            y = hs_ref[jj] * inv
            if final:
                n_ref[:, cs] = y * ng_ref[:, cs]
            else:
                sh_top, sh_bot = rows(nmod_ref, cs)
                sc_top, sc_bot = rows(nmod_ref, slice(D_MODEL + jj * tn, D_MODEL + (jj + 1) * tn))
                g_top = ng_ref[:, cs] * (1.0 + sc_top)
                g_bot = ng_ref[:, cs] * (1.0 + sc_bot)
                n_ref[top, cs] = (y[top, :] * g_top + sh_top).astype(n_ref.dtype)
                n_ref[bot, cs] = (y[bot, :] * g_bot + sh_bot).astype(n_ref.dtype)


def _outproj_rows_kernel(a_ref, w_ref, h_ref, gate_ref, ng_ref, nmod_ref, *rest, nj, tn, final):
    if final:
        n_ref, keep_ref = rest
        n_ref = n_ref.at[0]
    else:
        keep_ref, n_ref = rest
    i = pl.program_id(0)
    per = LTOT // OUT_TM
    b = i // per
    last = i % per == per - 1
    top = slice(0, OUT_TM - CTX_LEN)
    bot = slice(OUT_TM - CTX_LEN, OUT_TM)

    def rows(ref, cols):
        lat = ref[pl.ds(b, 1), cols]
        return lat, jnp.where(last, ref[BATCH:BATCH + 1, cols], lat)

    a = a_ref[...]
    ssq_t = jnp.zeros((OUT_TM - CTX_LEN, 1), F32)
    ssq_b = jnp.zeros((CTX_LEN, 1), F32)
    for jj in range(nj):
        cs = slice(jj * tn, (jj + 1) * tn)
        acc = jnp.dot(a, w_ref[0, jj], preferred_element_type=F32)
        g_top, g_bot = rows(gate_ref.at[0], cs)
        hn_t = h_ref[top, cs] + g_top * acc[top, :]
        hn_b = h_ref[bot, cs] + g_bot * acc[bot, :]
        keep_ref[top, cs] = hn_t
        keep_ref[bot, cs] = hn_b
        ssq_t = ssq_t + jnp.sum(hn_t * hn_t, axis=-1, keepdims=True)
        ssq_b = ssq_b + jnp.sum(hn_b * hn_b, axis=-1, keepdims=True)
    inv_t = lax.rsqrt(ssq_t * (1.0 / D_MODEL) + EPS)
    inv_b = lax.rsqrt(ssq_b * (1.0 / D_MODEL) + EPS)
    for jj in range(nj):
        cs = slice(jj * tn, (jj + 1) * tn)
        if final:
            n_ref[top, cs] = keep_ref[top, cs] * inv_t * ng_ref[:, cs]
            n_ref[bot, cs] = keep_ref[bot, cs] * inv_b * ng_ref[:, cs]
        else:
            sh_top, sh_bot = rows(nmod_ref, cs)
            sc_top, sc_bot = rows(nmod_ref, slice(D_MODEL + jj * tn, D_MODEL + (jj + 1) * tn))
            n_ref[top, cs] = (keep_ref[top, cs] * inv_t * (ng_ref[:, cs] * (1.0 + sc_top)) + sh_top).astype(n_ref.dtype)
            n_ref[bot, cs] = (keep_ref[bot, cs] * inv_b * (ng_ref[:, cs] * (1.0 + sc_bot)) + sh_bot).astype(n_ref.dtype)


def _outproj_tn(kdim):
    return 512


def _wtile_kernel(w_ref, *rest):
    if len(rest) == 2:
        g_ref, o_ref = rest
        o_ref[0, 0] = (w_ref[0] * g_ref[0]).astype(o_ref.dtype)
    else:
        o_ref, = rest
        o_ref[0, 0] = w_ref[0].astype(o_ref.dtype)


def _outproj_weights(w, row_gain=None):
    nl, kdim, _ = w.shape
    tn = _outproj_tn(kdim)
    in_specs = [pl.BlockSpec((1, kdim, tn), lambda l, j: (l, 0, j))]
    args = [w]
    if row_gain is not None:
        in_specs.append(pl.BlockSpec((1, kdim, 1), lambda l, j: (l, 0, 0)))
        args.append(row_gain.reshape(nl, kdim, 1))
    return pl.pallas_call(
        _wtile_kernel,
        out_shape=jax.ShapeDtypeStruct((nl, D_MODEL // tn, kdim, tn), BF16),
        grid=(nl, D_MODEL // tn),
        in_specs=in_specs,
        out_specs=pl.BlockSpec((1, 1, kdim, tn), lambda l, j: (l, j, 0, 0)),
        compiler_params=_cparams(("parallel", "parallel")),
        name="wtile",
    )(*args)


def _outproj(a, w, layer, h, mods, mod_layer, next_g, next_mod, *, final, rms_a=False):
    m, kdim = a.shape
    tn = _outproj_tn(kdim)
    nj = D_MODEL // tn
    gate_off = 2 * D_MODEL // tn
    resident = kdim <= 2 * D_MODEL
    if resident:
        w_spec = pl.BlockSpec((1, nj, kdim, tn), lambda i, j: (layer, 0, 0, 0), pipeline_mode=pl.Buffered(1))
    else:
        w_spec = pl.BlockSpec((1, 1, kdim, tn), lambda i, j: (layer, j, 0, 0))
    if kdim <= D_MODEL and not rms_a:
        per = LTOT // OUT_TM
        row = pl.BlockSpec((OUT_TM, D_MODEL), lambda i: (i, 0))
        if final:
            out_shape = jax.ShapeDtypeStruct((BATCH, SEQ, D_MODEL), F32)
            out_specs = pl.BlockSpec((1, OUT_TM, D_MODEL), lambda i: (i // per, i % per, 0))
            scratch = [pltpu.VMEM((OUT_TM, D_MODEL), F32)]
        else:
            out_shape = (jax.ShapeDtypeStruct((m, D_MODEL), F32), jax.ShapeDtypeStruct((m, D_MODEL), BF16))
            out_specs = (row, row)
            scratch = []
        return pl.pallas_call(
            functools.partial(_outproj_rows_kernel, nj=nj, tn=tn, final=final),
            out_shape=out_shape,
            grid=(m // OUT_TM,),
            in_specs=[pl.BlockSpec((OUT_TM, kdim), lambda i: (i, 0)),
                      pl.BlockSpec((1, nj, kdim, tn), lambda i: (layer, 0, 0, 0), pipeline_mode=pl.Buffered(1)),
                      row,
                      pl.BlockSpec((1, MOD_ROWS, D_MODEL), lambda i: (mod_layer, 0, 2)),
                      pl.BlockSpec((1, D_MODEL), lambda i: (0, 0)),
                      pl.BlockSpec((MOD_ROWS, 3 * D_MODEL), lambda i: (0, 0))],
            out_specs=out_specs,
            scratch_shapes=scratch,
            compiler_params=_cparams(("parallel",)),
            name="outproj_rows",
        )(a, w, h, mods, next_g.reshape(1, D_MODEL), next_mod)
    tile = pl.BlockSpec((OUT_TM, tn), lambda i, j: (i, j))
    full = pl.BlockSpec((OUT_TM, D_MODEL), lambda i, j: (i, 0))
    const = lambda i, j: (0, 0)
    if final:
        per = LTOT // OUT_TM
        out_shape = jax.ShapeDtypeStruct((BATCH, SEQ, D_MODEL), F32)
        out_specs = pl.BlockSpec((1, OUT_TM, D_MODEL), lambda i, j: (i // per, i % per, 0))
    else:
        out_shape = (jax.ShapeDtypeStruct((m, D_MODEL), F32), jax.ShapeDtypeStruct((m, D_MODEL), BF16))
        out_specs = (tile, full)
    return pl.pallas_call(
        functools.partial(_outproj_kernel, nj=nj, tn=tn, final=final, resident=resident, rms_a=rms_a),
        out_shape=out_shape,
        grid=(m // OUT_TM, nj),
        in_specs=[pl.BlockSpec((OUT_TM, kdim), lambda i, j: (i, 0)),
                  w_spec,
                  tile,
                  pl.BlockSpec((1, MOD_ROWS, tn), lambda i, j: (mod_layer, 0, gate_off + j)),
                  pl.BlockSpec((1, D_MODEL), const),
                  pl.BlockSpec((MOD_ROWS, 3 * D_MODEL), const)],
        out_specs=out_specs,
        scratch_shapes=[pltpu.VMEM((nj, OUT_TM, tn), F32),
                        pltpu.VMEM((OUT_TM, 1), F32),
                        pltpu.VMEM((OUT_TM, 1), F32)],
        compiler_params=_cparams(("parallel", "arbitrary")),
        name="outproj",
    )(a, w, h, mods, next_g.reshape(1, D_MODEL), next_mod)


def _na_tile_patterns():
    pats = []
    for t in (0, 1, NA_TILES - 1):
        ws = min(max(NA_QROWS * t - NA_WIN_ROWS // 2, 0), NA_ROWS - NA_KROWS)
        pat = np.full((NA_QROWS, NA_KROWS), NA_NDR, dtype=np.int64)
        for a in range(NA_QROWS):
            r = NA_QROWS * t + a
            rs = min(max(r - NA_WIN_ROWS // 2, 0), NA_ROWS - NA_WIN_ROWS)
            for jj in range(NA_KROWS):
                kabs = ws + jj
                if rs <= kabs < rs + NA_WIN_ROWS:
                    pat[a, jj] = kabs - r + NA_WIN_ROWS - 1
        pats.append(pat)
    return pats


def _na_build_bias(rpb_ref, slab_ref, bias_ref):
    w2 = 2 * GRID_W
    qc = lax.broadcasted_iota(jnp.int32, (GRID_W, w2), 0)
    lane = lax.broadcasted_iota(jnp.int32, (GRID_W, w2), 1)
    left = lane < GRID_W
    kc = jnp.where(left, lane, lane - GRID_W)
    c_start = jnp.clip(qc - NA_WIN_COLS // 2, 0, GRID_W - NA_WIN_COLS)
    col_ok = jnp.logical_and(kc >= c_start, kc < c_start + NA_WIN_COLS)
    shift = w2 - (NA_WIN_COLS - 1)
    for dr in range(NA_NDR):
        row = jnp.broadcast_to(rpb_ref[0, dr:dr + 1, :] * LOG2E, (GRID_W, w2))
        lo = pltpu.roll(row, shift, 1, stride=1, stride_axis=0)
        hi = pltpu.roll(row, (shift + GRID_W) % w2, 1, stride=1, stride_axis=0)
        slab_ref[dr] = jnp.where(col_ok, jnp.where(left, lo, hi), NEG)
    slab_ref[NA_NDR] = jnp.full((GRID_W, w2), NEG, F32)
    for p, pat in enumerate(_na_tile_patterns()):
        for a in range(NA_QROWS):
            for jp in range(NA_KROWS // 2):
                blk = jnp.where(left, slab_ref[int(pat[a, 2 * jp])], slab_ref[int(pat[a, 2 * jp + 1])])
                bias_ref[p, a * GRID_W:(a + 1) * GRID_W, jp * w2:(jp + 1) * w2] = blk


def _na_kernel(q_ref, k_ref, v_ref, z_ref, rpb_ref, o_ref, bias_ref, slab_ref, *, need_ctx):
    @pl.when(pl.program_id(1) == 0)
    def _():
        _na_build_bias(rpb_ref, slab_ref, bias_ref)

    kc = k_ref[0, SEQ:LTOT, :]
    vc = v_ref[0, SEQ:LTOT, :]

    def finish(o, l, r0):
        z = z_ref[0, pl.ds(r0, NA_TQ), :].astype(F32)
        o_ref[0, pl.ds(r0, NA_TQ), :] = (o * (1.0 / l) * _silu(z)).astype(o_ref.dtype)

    if need_ctx:
        s = lax.dot_general(q_ref[0, SEQ:LTOT, :], kc, _NT, preferred_element_type=F32)
        p = jnp.exp2(s - jnp.max(s, axis=-1, keepdims=True))
        finish(jnp.dot(p.astype(BF16), vc, preferred_element_type=F32), jnp.sum(p, axis=-1, keepdims=True), SEQ)
    else:
        o_ref[0, SEQ:LTOT, :] = jnp.zeros((CTX_LEN, NA_HEAD_DIM), o_ref.dtype)

    def key_start(t):
        ws = jnp.clip(NA_QROWS * t - NA_WIN_ROWS // 2, 0, NA_ROWS - NA_KROWS)
        return pl.multiple_of(ws * GRID_W, GRID_W)

    def scores(t):
        q = q_ref[0, pl.ds(pl.multiple_of(t * NA_TQ, NA_TQ), NA_TQ), :]
        pat = jnp.where(t == 0, 0, jnp.where(t == NA_TILES - 1, 2, 1))
        s1 = lax.dot_general(q, k_ref[0, pl.ds(key_start(t), NA_TK), :], _NT,
                             preferred_element_type=F32) + bias_ref[pat]
        return s1, lax.dot_general(q, kc, _NT, preferred_element_type=F32)

    def tile(t, carry, has_next=True):
        s1, s2 = carry
        nxt = scores(t + 1) if has_next else None
        m = jnp.maximum(jnp.max(s1, axis=-1, keepdims=True), jnp.max(s2, axis=-1, keepdims=True))
        p1 = jnp.exp2(s1 - m)
        p2 = jnp.exp2(s2 - m)
        l = jnp.sum(p1, axis=-1, keepdims=True) + jnp.sum(p2, axis=-1, keepdims=True)
        vw = v_ref[0, pl.ds(key_start(t), NA_TK), :]
        o = (jnp.dot(p1.astype(BF16), vw, preferred_element_type=F32)
             + jnp.dot(p2.astype(BF16), vc, preferred_element_type=F32))
        finish(o, l, pl.multiple_of(t * NA_TQ, NA_TQ))
        return nxt

    last = lax.fori_loop(0, NA_TILES - 1, tile, scores(0), unroll=2)
    tile(NA_TILES - 1, last, has_next=False)


def _na_attention(p, rpb, need_ctx):
    hd = NA_HEAD_DIM
    w2 = 2 * GRID_W
    rpb_pad = jnp.pad(rpb, ((0, 0), (0, NA_NDR + 1 - rpb.shape[1]), (0, w2 - rpb.shape[2])))
    blk = lambda off: pl.BlockSpec((1, LTOT, hd), lambda h, b: (b, 0, off + h))
    return pl.pallas_call(
        functools.partial(_na_kernel, need_ctx=need_ctx),
        out_shape=jax.ShapeDtypeStruct((BATCH, LTOT, D_MODEL), BF16),
        grid=(NA_HEADS, BATCH),
        in_specs=[blk(0), blk(NA_HEADS), blk(2 * NA_HEADS), blk(3 * NA_HEADS),
                  pl.BlockSpec((1, NA_NDR + 1, w2), lambda h, b: (h, 0, 0))],
        out_specs=blk(0),
        scratch_shapes=[pltpu.VMEM((3, NA_TQ, NA_TK), F32),
                        pltpu.VMEM((NA_NDR + 1, GRID_W, w2), F32)],
        compiler_params=_cparams(("parallel", "arbitrary")),
        name="na_attention",
    )(p, p, p, p, rpb_pad)


def _conv_kernel(u_ref, w_ref, b_ref, o_ref):
    u = u_ref[0].astype(F32)
    row = lax.broadcasted_iota(jnp.int32, u.shape, 0)
    seg_first = jnp.logical_or(row == 0, row == SEQ)
    seg_last = jnp.logical_or(row == SEQ - 1, row == LTOT - 1)
    up = jnp.where(seg_first, 0.0, pltpu.roll(u, 1, 0))
    un = jnp.where(seg_last, 0.0, pltpu.roll(u, LTOT - 1, 0))
    w = w_ref[...]
    y = w[0:1] * up + w[1:2] * u + w[2:3] * un + b_ref[...]
    o_ref[0] = _silu(y).astype(o_ref.dtype)


def _ssd_conv(p, conv_w, conv_b):
    tc = 512
    off = SSD_D_INNER // tc
    return pl.pallas_call(
        _conv_kernel,
        out_shape=jax.ShapeDtypeStruct((BATCH, LTOT, SSD_CONV_CH), BF16),
        grid=(BATCH, SSD_CONV_CH // tc),
        in_specs=[pl.BlockSpec((1, LTOT, tc), lambda b, j: (b, 0, off + j)),
                  pl.BlockSpec((3, tc), lambda b, j: (0, j)),
                  pl.BlockSpec((1, tc), lambda b, j: (0, j))],
        out_specs=pl.BlockSpec((1, LTOT, tc), lambda b, j: (b, 0, j)),
        compiler_params=_cparams(("parallel", "parallel")),
        name="ssd_conv",
    )(p, conv_w, conv_b.reshape(1, SSD_CONV_CH))


def _bf16_pieces(v):
    hi = v.astype(BF16)
    r1 = v - hi.astype(F32)
    mid = r1.astype(BF16)
    lo = (r1 - mid.astype(F32)).astype(BF16)
    return hi, mid, lo


def _split3(v):
    hi, mid, lo = _bf16_pieces(v)
    lane = lax.broadcasted_iota(jnp.int32, v.shape, 1)
    return jnp.where(lane < SSD_DL, hi, jnp.where(lane < 2 * SSD_DL, mid, lo))


def _ssd_kernel(x_ref, b_ref, c_ref, z_ref, dtc_ref, dtr_ref, pc_ref, pr_ref, dskip_ref,
                y_ref, yacc_ref, s_ref):
    q = SSD_CHUNK
    hp = SSD_HPG
    li = lax.broadcasted_iota(jnp.int32, (q, q), 0)
    si = lax.broadcasted_iota(jnp.int32, (q, q), 1)
    lower = li >= si
    upper = li <= si
    tri_lo = lower.astype(BF16)
    tri_up = upper.astype(BF16)
    tri_lanes = (jnp.concatenate([tri_lo] * 3, axis=1), jnp.concatenate([tri_up] * 3, axis=1))
    tri_rows = (jnp.concatenate([tri_up] * 3, axis=0), jnp.concatenate([tri_lo] * 3, axis=0))
    left = si < SSD_HEAD_DIM

    def expand_matrix(d):
        r = lax.broadcasted_iota(jnp.int32, (3 * SSD_DL, SSD_GW), 0) % SSD_DL
        c = lax.broadcasted_iota(jnp.int32, (3 * SSD_DL, SSD_GW), 1) // SSD_HEAD_DIM
        return (r == d * hp + c).astype(BF16)

    expand = (expand_matrix(0), expand_matrix(1))

    bias_c = pc_ref[0, 0:1, :]
    a_c = -jnp.exp(pc_ref[0, 1:2, :]) * LOG2E
    bias_r = pr_ref[0, :, 0:1]
    a_r = -jnp.exp(pr_ref[0, :, 1:2]) * LOG2E

    s_ref[...] = jnp.zeros_like(s_ref)

    def prep(c, d):
        r0 = pl.multiple_of(c * q, q)
        dtc = _softplus(dtc_ref[0, 0, pl.ds(r0, q), :] + bias_c)
        acum_c = jnp.dot(tri_lanes[d], jnp.concatenate(_bf16_pieces(dtc * a_c), axis=0),
                         preferred_element_type=F32)
        dtr = _softplus(dtr_ref[0, 0, c] + bias_r)
        acum_r = jnp.dot(jnp.concatenate(_bf16_pieces(dtr * a_r), axis=1), tri_rows[d],
                         preferred_element_type=F32)
        tot_c = acum_c[q - 1:q, :] if d == 0 else acum_c[0:1, :]
        ea_c = jnp.exp2(acum_c)
        dw_c = dtc * jnp.exp2(tot_c - acum_c)
        ldt = jnp.log2(dtr)
        return _split3(ea_c), _split3(dw_c), acum_c, acum_r - ldt, ldt

    def chunk(c, d, prepared):
        ea_s, dw_s, acum_c, acum_r, ldt = prepared
        ea_x = jnp.dot(ea_s, expand[d], preferred_element_type=F32)
        dw_x = jnp.dot(dw_s, expand[d], preferred_element_type=F32)
        r0 = pl.multiple_of(c * q, q)
        mask = lower if d == 0 else upper
        xb = x_ref[0, pl.ds(r0, q), :]
        bm = b_ref[0, pl.ds(r0, q), :]
        cm = c_ref[0, pl.ds(r0, q), :]
        etot_x = ea_x[q - 1:q, :] if d == 0 else ea_x[0:1, :]
        cb = jnp.where(mask, lax.dot_general(cm, bm, _NT, preferred_element_type=F32), 0.0)
        bt = bm.astype(F32).T.astype(BF16)
        sprev = s_ref[d]
        yoff = jnp.dot(cm, sprev.astype(BF16), preferred_element_type=F32)
        s_ref[d] = sprev * etot_x + jnp.dot(bt, xb * dw_x.astype(BF16), preferred_element_type=F32)
        zero = jnp.zeros((q, q), BF16)
        for p in range(hp // 2):
            cs = slice(p * q, (p + 1) * q)
            m_pair = []
            for ln in (d * hp + 2 * p, d * hp + 2 * p + 1):
                e = jnp.minimum(acum_c[:, ln:ln + 1] - acum_r[ln:ln + 1, :], ldt[ln:ln + 1, :])
                m_pair.append((cb * jnp.exp2(e)).astype(BF16))
            x_p = xb[:, cs]
            rhs = jnp.concatenate([jnp.where(left, x_p, zero), jnp.where(left, zero, x_p)], axis=0)
            yd = jnp.dot(jnp.concatenate(m_pair, axis=1), rhs, preferred_element_type=F32)
            yacc_ref[d, pl.ds(r0, q), cs] = yd + ea_x[:, cs] * yoff[:, cs]

    lat_chunks = SSD_NCHUNK - SSD_CCHUNK

    def fwd_chunk(k):
        return jnp.where(k < SSD_CCHUNK, lat_chunks + k, k - SSD_CCHUNK)

    def bwd_chunk(k):
        return SSD_NCHUNK - 1 - k

    def step(k, carry):
        kn = jnp.minimum(k + 1, SSD_NCHUNK - 1)
        nxt = (prep(fwd_chunk(kn), 0), prep(bwd_chunk(kn), 1))
        chunk(fwd_chunk(k), 0, carry[0])
        chunk(bwd_chunk(k), 1, carry[1])
        return nxt

    lax.fori_loop(0, SSD_NCHUNK, step, (prep(fwd_chunk(0), 0), prep(bwd_chunk(0), 1)))

    y = yacc_ref[0] + yacc_ref[1] + x_ref[0].astype(F32) * dskip_ref[0]
    y_ref[0] = (y * _silu(z_ref[0].astype(F32))).astype(y_ref.dtype)


def _ssd_scan(xbc, p, dt_raw, dt_bias, a_log, d_skip):
    g, hp, dl = SSD_GROUPS, SSD_HPG, SSD_DL
    dt = dt_raw.reshape(BATCH, LTOT, 2, g, hp).transpose(0, 3, 1, 2, 4).reshape(BATCH, g, LTOT, dl)
    dt_col = jnp.tile(dt, (1, 1, 1, 3))
    dt_row = dt.reshape(BATCH, g, SSD_NCHUNK, SSD_CHUNK, dl).transpose(0, 1, 2, 4, 3)
    par = jnp.stack([dt_bias, a_log]).reshape(2, 2, g, hp).transpose(2, 0, 1, 3).reshape(g, 2, dl)
    par_col = jnp.pad(jnp.tile(par, (1, 1, 3)), ((0, 0), (0, 6), (0, 0)))
    par_row = jnp.pad(par.transpose(0, 2, 1), ((0, 0), (0, 0), (0, 126)))
    dskip = jnp.repeat(d_skip, SSD_HEAD_DIM).reshape(g, 1, SSD_GW)
    xoff = SSD_D_INNER // SSD_STATE
    return pl.pallas_call(
        _ssd_kernel,
        out_shape=jax.ShapeDtypeStruct((BATCH, LTOT, SSD_D_INNER), BF16),
        grid=(BATCH, g),
        in_specs=[pl.BlockSpec((1, LTOT, SSD_GW), lambda b, j: (b, 0, j)),
                  pl.BlockSpec((1, LTOT, SSD_STATE), lambda b, j: (b, 0, xoff + j)),
                  pl.BlockSpec((1, LTOT, SSD_STATE), lambda b, j: (b, 0, xoff + g + j)),
                  pl.BlockSpec((1, LTOT, SSD_GW), lambda b, j: (b, 0, j)),
                  pl.BlockSpec((1, 1, LTOT, 3 * dl), lambda b, j: (b, j, 0, 0)),
                  pl.BlockSpec((1, 1, SSD_NCHUNK, dl, SSD_CHUNK), lambda b, j: (b, j, 0, 0, 0)),
                  pl.BlockSpec((1, 8, 3 * dl), lambda b, j: (j, 0, 0)),
                  pl.BlockSpec((1, dl, 128), lambda b, j: (j, 0, 0)),
                  pl.BlockSpec((1, 1, SSD_GW), lambda b, j: (j, 0, 0))],
        out_specs=pl.BlockSpec((1, LTOT, SSD_GW), lambda b, j: (b, 0, j)),
        scratch_shapes=[pltpu.VMEM((2, LTOT, SSD_GW), F32),
                        pltpu.VMEM((2, SSD_STATE, SSD_GW), F32)],
        compiler_params=_cparams(("parallel", "parallel")),
        name="ssd_scan",
    )(xbc, xbc, xbc, p, dt_col, dt_row, par_col, par_row, dskip)


def _sg_kernel(z_ref, u_ref, v_ref, g_ref, b_ref, ws_ref, bs_ref, o_ref):
    v = v_ref[...].astype(F32)
    mu = jnp.mean(v, axis=-1, keepdims=True)
    vc = v - mu
    var = jnp.mean(vc * vc, axis=-1, keepdims=True)
    vn = (vc * lax.rsqrt(var + EPS) * g_ref[...] + b_ref[...]).astype(BF16)
    for g in range(SG_GROUPS):
        cs = slice(g * SG_GW, (g + 1) * SG_GW)
        sv = jnp.dot(ws_ref[g], vn[:, cs], preferred_element_type=F32) + bs_ref[:, g:g + 1]
        o_ref[:, cs] = (u_ref[:, cs].astype(F32) * sv * _silu(z_ref[:, cs].astype(F32))).astype(o_ref.dtype)


def _sg_gate(p, ln_g, ln_b, w_s, b_s_t):
    rows = p.shape[0]
    blk = lambda j: pl.BlockSpec((SG_CHUNK, SG_HALF), lambda i: (i, j))
    return pl.pallas_call(
        _sg_kernel,
        out_shape=jax.ShapeDtypeStruct((rows, SG_HALF), BF16),
        grid=(rows // SG_CHUNK,),
        in_specs=[blk(0), blk(1), blk(2),
                  pl.BlockSpec((1, SG_HALF), lambda i: (0, 0)),
                  pl.BlockSpec((1, SG_HALF), lambda i: (0, 0)),
                  pl.BlockSpec((SG_GROUPS, SG_CHUNK, SG_CHUNK), lambda i: (0, 0, 0)),
                  pl.BlockSpec((SG_CHUNK, SG_GROUPS), lambda i: (0, 0))],
        out_specs=blk(0),
        compiler_params=_cparams(("parallel",)),
        name="sg_gate",
    )(p, p, p, ln_g.reshape(1, SG_HALF), ln_b.reshape(1, SG_HALF), w_s, b_s_t)


def kernel(x, c, ctx, c_ctx, norm_g, ada_w, ada_b, na_w_in, na_rpb, na_w_out,
           ssd_w_in, ssd_conv_w, ssd_conv_b, ssd_dt_bias, ssd_a_log, ssd_d_skip,
           ssd_norm_g, ssd_w_out, sg_w_in, sg_ln_g, sg_ln_b, sg_w_s, sg_b_s, sg_w_out,
           final_norm_g):
    d = D_MODEL
    c_rows = jnp.concatenate([c, c_ctx[None], jnp.zeros((MOD_ROWS - BATCH - 1, d), F32)], axis=0)
    mods = _ada_all(c_rows, ada_w, ada_b)

    h, n = _norm_mod(ctx, x, norm_g[0], mods[0])

    na_w_out_b = _outproj_weights(na_w_out)
    ssd_w_out_b = _outproj_weights(ssd_w_out, ssd_norm_g)
    sg_w_out_b = _outproj_weights(sg_w_out)

    for i in range(DEPTH):
        kind, j = i % N_MIXERS, i // N_MIXERS
        need_ctx = i < DEPTH - 1
        if kind == 0:
            p = _proj(n, na_w_in, j, n=4 * d, out_dtype=BF16, scale_cols=d, scale=NA_HEAD_DIM ** -0.5 * LOG2E)
            y = _na_attention(p.reshape(BATCH, LTOT, 4 * d), na_rpb[j], need_ctx).reshape(ROWS, d)
            w_out = na_w_out_b
        elif kind == 1:
            p = _proj(n, ssd_w_in, j, n=SSD_MAIN, out_dtype=BF16).reshape(BATCH, LTOT, SSD_MAIN)
            dt_raw = _proj(n, ssd_w_in, j, n=2 * SSD_HEADS, out_dtype=F32, tn=2 * SSD_HEADS, col0=SSD_MAIN)
            xbc = _ssd_conv(p, ssd_conv_w[j], ssd_conv_b[j])
            y = _ssd_scan(xbc, p, dt_raw, ssd_dt_bias[j], ssd_a_log[j], ssd_d_skip[j]).reshape(ROWS, SSD_D_INNER)
            w_out = ssd_w_out_b
        else:
            p = _proj(n, sg_w_in, j, n=3 * SG_HALF, out_dtype=BF16, gelu_from_col=SG_HALF)
            y = _sg_gate(p, sg_ln_g[j], sg_ln_b[j], sg_w_s[j].astype(BF16), jnp.transpose(sg_b_s[j]))
            w_out = sg_w_out_b

        rms_a = kind == 1
        if need_ctx:
            h, n = _outproj(y, w_out, j, h, mods, i, norm_g[i + 1], mods[i + 1], final=False, rms_a=rms_a)
        else:
            out = _outproj(y, w_out, j, h, mods, i, final_norm_g, mods[i], final=True, rms_a=rms_a)

    return out
```

```python
import functools
import math

import numpy as np
import jax
import jax.numpy as jnp
from jax import lax
from jax.experimental import pallas as pl
from jax.experimental.pallas import tpu as pltpu

D_MODEL = 2048
BATCH = 4
SEQ = 2048
DEPTH = 4
GRID_W = 64
CTX_LEN = 256
N_MIXERS = 3
EPS = 1e-6
LTOT = SEQ + CTX_LEN
ROWS = BATCH * LTOT

NA_HEADS = 16
NA_HEAD_DIM = D_MODEL // NA_HEADS
NA_WIN_ROWS = 8
NA_WIN_COLS = 16
NA_ROWS = SEQ // GRID_W
NA_QROWS = 4
NA_KROWS = NA_QROWS + NA_WIN_ROWS
NA_TQ = NA_QROWS * GRID_W
NA_TK = NA_KROWS * GRID_W
NA_TILES = NA_ROWS // NA_QROWS
NA_NDR = 2 * NA_WIN_ROWS - 1

SSD_D_INNER = 2 * D_MODEL
SSD_HEAD_DIM = 64
SSD_HEADS = SSD_D_INNER // SSD_HEAD_DIM
SSD_GROUPS = 8
SSD_STATE = 128
SSD_CHUNK = 128
SSD_GN = SSD_GROUPS * SSD_STATE
SSD_CONV_CH = SSD_D_INNER + 2 * SSD_GN
SSD_MAIN = SSD_D_INNER + SSD_CONV_CH
SSD_HPG = SSD_HEADS // SSD_GROUPS
SSD_GW = SSD_HPG * SSD_HEAD_DIM
SSD_NCHUNK = LTOT // SSD_CHUNK
SSD_CCHUNK = CTX_LEN // SSD_CHUNK
SSD_DL = 2 * SSD_HPG

SG_HALF = 3 * D_MODEL
SG_GROUPS = 16
SG_CHUNK = 128
SG_GW = SG_HALF // SG_GROUPS

NEG = -1e30
VMEM_LIMIT = 56 * 1024 * 1024
MOD_ROWS = 8

PROJ_TM, PROJ_TN = 1536, 1024
OUT_TM = 768
LOG2E = math.log2(math.e)

F32 = jnp.float32
BF16 = jnp.bfloat16
_NT = (((1,), (1,)), ((), ()))


def _cparams(sem):
    return pltpu.CompilerParams(dimension_semantics=sem, vmem_limit_bytes=VMEM_LIMIT)


def _silu(x):
    return x * (1.0 / (1.0 + jnp.exp2(x * -LOG2E)))


def _gelu_tanh(x):
    c = math.sqrt(2.0 / math.pi)
    return 0.5 * x * (1.0 + jnp.tanh(c * (x + 0.044715 * (x * x * x))))


def _softplus(x):
    return jnp.maximum(x, 0.0) + jnp.log(1.0 + jnp.exp(-jnp.abs(x)))


def _ada_kernel(c_ref, w_ref, b_ref, o_ref):
    a = _silu(c_ref[...]).astype(BF16)
    o_ref[0] = jnp.dot(a, w_ref[0].astype(BF16), preferred_element_type=F32) + b_ref[0]


def _ada_all(c_rows, ada_w, ada_b):
    tn = 1024
    return pl.pallas_call(
        _ada_kernel,
        out_shape=jax.ShapeDtypeStruct((DEPTH, MOD_ROWS, 3 * D_MODEL), F32),
        grid=(DEPTH, 3 * D_MODEL // tn),
        in_specs=[pl.BlockSpec((MOD_ROWS, D_MODEL), lambda l, j: (0, 0)),
                  pl.BlockSpec((1, D_MODEL, tn), lambda l, j: (l, 0, j)),
                  pl.BlockSpec((1, 1, tn), lambda l, j: (l, 0, j))],
        out_specs=pl.BlockSpec((1, MOD_ROWS, tn), lambda l, j: (l, 0, j)),
        compiler_params=_cparams(("arbitrary", "arbitrary")),
        name="ada_mod",
    )(c_rows, ada_w, ada_b.reshape(DEPTH, 1, 3 * D_MODEL))


def _mod_rows(mod_ref, row):
    m = mod_ref[pl.ds(row, 1), :]
    return m[:, :D_MODEL], m[:, D_MODEL:2 * D_MODEL], m[:, 2 * D_MODEL:]


def _norm_mod_kernel(ctx_ref, x_ref, g_ref, mod_ref, h_ref, o_ref):
    b = pl.program_id(0)
    r = pl.program_id(1)
    is_ctx = r == LTOT // CTX_LEN - 1
    shift, scale, _ = _mod_rows(mod_ref, jnp.where(is_ctx, BATCH, b))
    x = jnp.where(is_ctx, ctx_ref[0], x_ref[0])
    h_ref[...] = x
    y = x * lax.rsqrt(jnp.mean(x * x, axis=-1, keepdims=True) + EPS) * g_ref[...]
    o_ref[...] = (y * (1.0 + scale) + shift).astype(o_ref.dtype)


def _norm_mod(ctx, x, g, mod):
    per = LTOT // CTX_LEN
    rows = pl.BlockSpec((CTX_LEN, D_MODEL), lambda b, r: (b * per + r, 0))
    return pl.pallas_call(
        _norm_mod_kernel,
        out_shape=(jax.ShapeDtypeStruct((ROWS, D_MODEL), F32), jax.ShapeDtypeStruct((ROWS, D_MODEL), BF16)),
        grid=(BATCH, per),
        in_specs=[pl.BlockSpec((1, CTX_LEN, D_MODEL), lambda b, r: (b, 0, 0)),
                  pl.BlockSpec((1, CTX_LEN, D_MODEL), lambda b, r: (b, jnp.minimum(r, per - 2), 0)),
                  pl.BlockSpec((1, D_MODEL), lambda b, r: (0, 0)),
                  pl.BlockSpec((MOD_ROWS, 3 * D_MODEL), lambda b, r: (0, 0))],
        out_specs=(rows, rows),
        compiler_params=_cparams(("parallel", "parallel")),
        name="norm_mod",
    )(ctx, x, g.reshape(1, D_MODEL), mod)


def _proj_kernel(a_ref, w_ref, o_ref, wb_ref, *, gelu_from, scale_tiles, scale):
    j = pl.program_id(0)
    i = pl.program_id(1)

    @pl.when(i == 0)
    def _():
        w = w_ref[0]
        if scale_tiles:
            w = w * jnp.where(j < scale_tiles, scale, 1.0)
        wb_ref[...] = w.astype(BF16)

    acc = jnp.dot(a_ref[...], wb_ref[...], preferred_element_type=F32)
    if gelu_from is None:
        o_ref[...] = acc.astype(o_ref.dtype)
    else:
        @pl.when(j >= gelu_from)
        def _():
            o_ref[...] = _gelu_tanh(acc).astype(o_ref.dtype)

        @pl.when(j < gelu_from)
        def _():
            o_ref[...] = acc.astype(o_ref.dtype)


def _proj(a, w, layer, *, n, out_dtype, tn=PROJ_TN, col0=0, gelu_from_col=None, scale_cols=0, scale=1.0):
    m, kdim = a.shape
    off = col0 // tn
    return pl.pallas_call(
        functools.partial(_proj_kernel,
                          gelu_from=None if gelu_from_col is None else gelu_from_col // tn,
                          scale_tiles=scale_cols // tn, scale=scale),
        out_shape=jax.ShapeDtypeStruct((m, n), out_dtype),
        grid=(n // tn, m // PROJ_TM),
        in_specs=[pl.BlockSpec((PROJ_TM, kdim), lambda j, i: (i, 0)),
                  pl.BlockSpec((1, kdim, tn), lambda j, i: (layer, 0, off + j))],
        out_specs=pl.BlockSpec((PROJ_TM, tn), lambda j, i: (i, j)),
        scratch_shapes=[pltpu.VMEM((kdim, tn), BF16)],
        compiler_params=_cparams(("parallel", "arbitrary")),
        name="proj",
    )(a, w)


def _outproj_kernel(a_ref, w_ref, h_ref, gate_ref, ng_ref, nmod_ref, *rest, nj, tn, final, resident, rms_a):
    if final:
        n_ref, hs_ref, ss_ref, ra_ref = rest
        hn_ref = None
    else:
        hn_ref, n_ref, hs_ref, ss_ref, ra_ref = rest
    i = pl.program_id(0)
    j = pl.program_id(1)

    if rms_a:
        @pl.when(j == 0)
        def _():
            kdim = a_ref.shape[1]
            ssq = jnp.zeros((OUT_TM, 1), F32)
            for c0 in range(0, kdim, 512):
                af = a_ref[:, c0:c0 + 512].astype(F32)
                ssq = ssq + jnp.sum(af * af, axis=-1, keepdims=True)
            ra_ref[...] = lax.rsqrt(ssq * (1.0 / kdim) + EPS)
    per = LTOT // OUT_TM
    b = i // per
    last = i % per == per - 1
    top = slice(0, OUT_TM - CTX_LEN)
    bot = slice(OUT_TM - CTX_LEN, OUT_TM)
    if final:
        n_ref = n_ref.at[0]

    def rows(ref, cols):
        lat = ref[pl.ds(b, 1), cols]
        return lat, jnp.where(last, ref[BATCH:BATCH + 1, cols], lat)

    w = w_ref[0, j] if resident else w_ref[0, 0]
    acc = jnp.dot(a_ref[...], w, preferred_element_type=F32)
    if rms_a:
        acc = acc * ra_ref[...]
    g_top, g_bot = rows(gate_ref.at[0], slice(None))
    hn_t = h_ref[top, :] + g_top * acc[top, :]
    hn_b = h_ref[bot, :] + g_bot * acc[bot, :]
    hs_ref[j, top, :] = hn_t
    hs_ref[j, bot, :] = hn_b
    if not final:
        hn_ref[top, :] = hn_t
        hn_ref[bot, :] = hn_b
    sq_t = jnp.sum(hn_t * hn_t, axis=-1, keepdims=True)
    sq_b = jnp.sum(hn_b * hn_b, axis=-1, keepdims=True)

    @pl.when(j == 0)
    def _():
        ss_ref[top, :] = sq_t
        ss_ref[bot, :] = sq_b

    @pl.when(j > 0)
    def _():
        ss_ref[top, :] += sq_t
        ss_ref[bot, :] += sq_b

    @pl.when(j == nj - 1)
    def _():
        inv = lax.rsqrt(ss_ref[...] * (1.0 / D_MODEL) + EPS)
        for jj in range(nj):
            cs = slice(jj * tn, (jj + 1) * tn)
            y = hs_ref[jj] * inv
            if final:
                n_ref[:, cs] = y * ng_ref[:, cs]
            else:
                sh_top, sh_bot = rows(nmod_ref, cs)
                sc_top, sc_bot = rows(nmod_ref, slice(D_MODEL + jj * tn, D_MODEL + (jj + 1) * tn))
                g_top = ng_ref[:, cs] * (1.0 + sc_top)
                g_bot = ng_ref[:, cs] * (1.0 + sc_bot)
                n_ref[top, cs] = (y[top, :] * g_top + sh_top).astype(n_ref.dtype)
                n_ref[bot, cs] = (y[bot, :] * g_bot + sh_bot).astype(n_ref.dtype)


def _outproj_rows_kernel(a_ref, w_ref, h_ref, gate_ref, ng_ref, nmod_ref, *rest, nj, tn, final):
    if final:
        n_ref, keep_ref = rest
        n_ref = n_ref.at[0]
    else:
        keep_ref, n_ref = rest
    i = pl.program_id(0)
    per = LTOT // OUT_TM
    b = i // per
    last = i % per == per - 1
    top = slice(0, OUT_TM - CTX_LEN)
    bot = slice(OUT_TM - CTX_LEN, OUT_TM)

    def rows(ref, cols):
        lat = ref[pl.ds(b, 1), cols]
        return lat, jnp.where(last, ref[BATCH:BATCH + 1, cols], lat)

    a = a_ref[...]
    ssq_t = jnp.zeros((OUT_TM - CTX_LEN, 1), F32)
    ssq_b = jnp.zeros((CTX_LEN, 1), F32)
    for jj in range(nj):
        cs = slice(jj * tn, (jj + 1) * tn)
        acc = jnp.dot(a, w_ref[0, jj], preferred_element_type=F32)
        g_top, g_bot = rows(gate_ref.at[0], cs)
        hn_t = h_ref[top, cs] + g_top * acc[top, :]
        hn_b = h_ref[bot, cs] + g_bot * acc[bot, :]
        keep_ref[top, cs] = hn_t
        keep_ref[bot, cs] = hn_b
        ssq_t = ssq_t + jnp.sum(hn_t * hn_t, axis=-1, keepdims=True)
        ssq_b = ssq_b + jnp.sum(hn_b * hn_b, axis=-1, keepdims=True)
    inv_t = lax.rsqrt(ssq_t * (1.0 / D_MODEL) + EPS)
    inv_b = lax.rsqrt(ssq_b * (1.0 / D_MODEL) + EPS)
    for jj in range(nj):
        cs = slice(jj * tn, (jj + 1) * tn)
        if final:
            n_ref[top, cs] = keep_ref[top, cs] * inv_t * ng_ref[:, cs]
            n_ref[bot, cs] = keep_ref[bot, cs] * inv_b * ng_ref[:, cs]
        else:
            sh_top, sh_bot = rows(nmod_ref, cs)
            sc_top, sc_bot = rows(nmod_ref, slice(D_MODEL + jj * tn, D_MODEL + (jj + 1) * tn))
            n_ref[top, cs] = (keep_ref[top, cs] * inv_t * (ng_ref[:, cs] * (1.0 + sc_top)) + sh_top).astype(n_ref.dtype)
            n_ref[bot, cs] = (keep_ref[bot, cs] * inv_b * (ng_ref[:, cs] * (1.0 + sc_bot)) + sh_bot).astype(n_ref.dtype)


def _outproj_tn(kdim):
    return 512


def _wtile_kernel(w_ref, *rest):
    if len(rest) == 2:
        g_ref, o_ref = rest
        o_ref[0, 0] = (w_ref[0] * g_ref[0]).astype(o_ref.dtype)
    else:
        o_ref, = rest
        o_ref[0, 0] = w_ref[0].astype(o_ref.dtype)


def _outproj_weights(w, row_gain=None):
    nl, kdim, _ = w.shape
    tn = _outproj_tn(kdim)
    in_specs = [pl.BlockSpec((1, kdim, tn), lambda l, j: (l, 0, j))]
    args = [w]
    if row_gain is not None:
        in_specs.append(pl.BlockSpec((1, kdim, 1), lambda l, j: (l, 0, 0)))
        args.append(row_gain.reshape(nl, kdim, 1))
    return pl.pallas_call(
        _wtile_kernel,
        out_shape=jax.ShapeDtypeStruct((nl, D_MODEL // tn, kdim, tn), BF16),
        grid=(nl, D_MODEL // tn),
        in_specs=in_specs,
        out_specs=pl.BlockSpec((1, 1, kdim, tn), lambda l, j: (l, j, 0, 0)),
        compiler_params=_cparams(("parallel", "parallel")),
        name="wtile",
    )(*args)


def _outproj(a, w, layer, h, mods, mod_layer, next_g, next_mod, *, final, rms_a=False):
    m, kdim = a.shape
    tn = _outproj_tn(kdim)
    nj = D_MODEL // tn
    gate_off = 2 * D_MODEL // tn
    resident = kdim <= 2 * D_MODEL
    if resident:
        w_spec = pl.BlockSpec((1, nj, kdim, tn), lambda i, j: (layer, 0, 0, 0), pipeline_mode=pl.Buffered(1))
    else:
        w_spec = pl.BlockSpec((1, 1, kdim, tn), lambda i, j: (layer, j, 0, 0))
    if kdim <= D_MODEL and not rms_a:
        per = LTOT // OUT_TM
        row = pl.BlockSpec((OUT_TM, D_MODEL), lambda i: (i, 0))
        if final:
            out_shape = jax.ShapeDtypeStruct((BATCH, SEQ, D_MODEL), F32)
            out_specs = pl.BlockSpec((1, OUT_TM, D_MODEL), lambda i: (i // per, i % per, 0))
            scratch = [pltpu.VMEM((OUT_TM, D_MODEL), F32)]
        else:
            out_shape = (jax.ShapeDtypeStruct((m, D_MODEL), F32), jax.ShapeDtypeStruct((m, D_MODEL), BF16))
            out_specs = (row, row)
            scratch = []
        return pl.pallas_call(
            functools.partial(_outproj_rows_kernel, nj=nj, tn=tn, final=final),
            out_shape=out_shape,
            grid=(m // OUT_TM,),
            in_specs=[pl.BlockSpec((OUT_TM, kdim), lambda i: (i, 0)),
                      pl.BlockSpec((1, nj, kdim, tn), lambda i: (layer, 0, 0, 0), pipeline_mode=pl.Buffered(1)),
                      row,
                      pl.BlockSpec((1, MOD_ROWS, D_MODEL), lambda i: (mod_layer, 0, 2)),
                      pl.BlockSpec((1, D_MODEL), lambda i: (0, 0)),
                      pl.BlockSpec((MOD_ROWS, 3 * D_MODEL), lambda i: (0, 0))],
            out_specs=out_specs,
            scratch_shapes=scratch,
            compiler_params=_cparams(("parallel",)),
            name="outproj_rows",
        )(a, w, h, mods, next_g.reshape(1, D_MODEL), next_mod)
    tile = pl.BlockSpec((OUT_TM, tn), lambda i, j: (i, j))
    full = pl.BlockSpec((OUT_TM, D_MODEL), lambda i, j: (i, 0))
    const = lambda i, j: (0, 0)
    if final:
        per = LTOT // OUT_TM
        out_shape = jax.ShapeDtypeStruct((BATCH, SEQ, D_MODEL), F32)
        out_specs = pl.BlockSpec((1, OUT_TM, D_MODEL), lambda i, j: (i // per, i % per, 0))
    else:
        out_shape = (jax.ShapeDtypeStruct((m, D_MODEL), F32), jax.ShapeDtypeStruct((m, D_MODEL), BF16))
        out_specs = (tile, full)
    return pl.pallas_call(
        functools.partial(_outproj_kernel, nj=nj, tn=tn, final=final, resident=resident, rms_a=rms_a),
        out_shape=out_shape,
        grid=(m // OUT_TM, nj),
        in_specs=[pl.BlockSpec((OUT_TM, kdim), lambda i, j: (i, 0)),
                  w_spec,
                  tile,
                  pl.BlockSpec((1, MOD_ROWS, tn), lambda i, j: (mod_layer, 0, gate_off + j)),
                  pl.BlockSpec((1, D_MODEL), const),
                  pl.BlockSpec((MOD_ROWS, 3 * D_MODEL), const)],
        out_specs=out_specs,
        scratch_shapes=[pltpu.VMEM((nj, OUT_TM, tn), F32),
                        pltpu.VMEM((OUT_TM, 1), F32),
                        pltpu.VMEM((OUT_TM, 1), F32)],
        compiler_params=_cparams(("parallel", "arbitrary")),
        name="outproj",
    )(a, w, h, mods, next_g.reshape(1, D_MODEL), next_mod)


def _na_tile_patterns():
    pats = []
    for t in (0, 1, NA_TILES - 1):
        ws = min(max(NA_QROWS * t - NA_WIN_ROWS // 2, 0), NA_ROWS - NA_KROWS)
        pat = np.full((NA_QROWS, NA_KROWS), NA_NDR, dtype=np.int64)
        for a in range(NA_QROWS):
            r = NA_QROWS * t + a
            rs = min(max(r - NA_WIN_ROWS // 2, 0), NA_ROWS - NA_WIN_ROWS)
            for jj in range(NA_KROWS):
                kabs = ws + jj
                if rs <= kabs < rs + NA_WIN_ROWS:
                    pat[a, jj] = kabs - r + NA_WIN_ROWS - 1
        pats.append(pat)
    return pats


def _na_build_bias(rpb_ref, slab_ref, bias_ref):
    w2 = 2 * GRID_W
    qc = lax.broadcasted_iota(jnp.int32, (GRID_W, w2), 0)
    lane = lax.broadcasted_iota(jnp.int32, (GRID_W, w2), 1)
    left = lane < GRID_W
    kc = jnp.where(left, lane, lane - GRID_W)
    c_start = jnp.clip(qc - NA_WIN_COLS // 2, 0, GRID_W - NA_WIN_COLS)
    col_ok = jnp.logical_and(kc >= c_start, kc < c_start + NA_WIN_COLS)
    shift = w2 - (NA_WIN_COLS - 1)
    for dr in range(NA_NDR):
        row = jnp.broadcast_to(rpb_ref[0, dr:dr + 1, :] * LOG2E, (GRID_W, w2))
        lo = pltpu.roll(row, shift, 1, stride=1, stride_axis=0)
        hi = pltpu.roll(row, (shift + GRID_W) % w2, 1, stride=1, stride_axis=0)
        slab_ref[dr] = jnp.where(col_ok, jnp.where(left, lo, hi), NEG)
    slab_ref[NA_NDR] = jnp.full((GRID_W, w2), NEG, F32)
    for p, pat in enumerate(_na_tile_patterns()):
        for a in range(NA_QROWS):
            for jp in range(NA_KROWS // 2):
                blk = jnp.where(left, slab_ref[int(pat[a, 2 * jp])], slab_ref[int(pat[a, 2 * jp + 1])])
                bias_ref[p, a * GRID_W:(a + 1) * GRID_W, jp * w2:(jp + 1) * w2] = blk


def _na_kernel(q_ref, k_ref, v_ref, z_ref, rpb_ref, o_ref, bias_ref, slab_ref, *, need_ctx):
    @pl.when(pl.program_id(1) == 0)
    def _():
        _na_build_bias(rpb_ref, slab_ref, bias_ref)

    kc = k_ref[0, SEQ:LTOT, :]
    vc = v_ref[0, SEQ:LTOT, :]

    def finish(o, l, r0):
        z = z_ref[0, pl.ds(r0, NA_TQ), :].astype(F32)
        o_ref[0, pl.ds(r0, NA_TQ), :] = (o * (1.0 / l) * _silu(z)).astype(o_ref.dtype)

    if need_ctx:
        s = lax.dot_general(q_ref[0, SEQ:LTOT, :], kc, _NT, preferred_element_type=F32)
        p = jnp.exp2(s - jnp.max(s, axis=-1, keepdims=True))
        finish(jnp.dot(p.astype(BF16), vc, preferred_element_type=F32), jnp.sum(p, axis=-1, keepdims=True), SEQ)
    else:
        o_ref[0, SEQ:LTOT, :] = jnp.zeros((CTX_LEN, NA_HEAD_DIM), o_ref.dtype)

    def key_start(t):
        ws = jnp.clip(NA_QROWS * t - NA_WIN_ROWS // 2, 0, NA_ROWS - NA_KROWS)
        return pl.multiple_of(ws * GRID_W, GRID_W)

    def scores(t):
        q = q_ref[0, pl.ds(pl.multiple_of(t * NA_TQ, NA_TQ), NA_TQ), :]
        pat = jnp.where(t == 0, 0, jnp.where(t == NA_TILES - 1, 2, 1))
        s1 = lax.dot_general(q, k_ref[0, pl.ds(key_start(t), NA_TK), :], _NT,
                             preferred_element_type=F32) + bias_ref[pat]
        return s1, lax.dot_general(q, kc, _NT, preferred_element_type=F32)

    def tile(t, carry, has_next=True):
        s1, s2 = carry
        nxt = scores(t + 1) if has_next else None
        m = jnp.maximum(jnp.max(s1, axis=-1, keepdims=True), jnp.max(s2, axis=-1, keepdims=True))
        p1 = jnp.exp2(s1 - m)
        p2 = jnp.exp2(s2 - m)
        l = jnp.sum(p1, axis=-1, keepdims=True) + jnp.sum(p2, axis=-1, keepdims=True)
        vw = v_ref[0, pl.ds(key_start(t), NA_TK), :]
        o = (jnp.dot(p1.astype(BF16), vw, preferred_element_type=F32)
             + jnp.dot(p2.astype(BF16), vc, preferred_element_type=F32))
        finish(o, l, pl.multiple_of(t * NA_TQ, NA_TQ))
        return nxt

    last = lax.fori_loop(0, NA_TILES - 1, tile, scores(0), unroll=2)
    tile(NA_TILES - 1, last, has_next=False)


def _na_attention(p, rpb, need_ctx):
    hd = NA_HEAD_DIM
    w2 = 2 * GRID_W
    rpb_pad = jnp.pad(rpb, ((0, 0), (0, NA_NDR + 1 - rpb.shape[1]), (0, w2 - rpb.shape[2])))
    blk = lambda off: pl.BlockSpec((1, LTOT, hd), lambda h, b: (b, 0, off + h))
    return pl.pallas_call(
        functools.partial(_na_kernel, need_ctx=need_ctx),
        out_shape=jax.ShapeDtypeStruct((BATCH, LTOT, D_MODEL), BF16),
        grid=(NA_HEADS, BATCH),
        in_specs=[blk(0), blk(NA_HEADS), blk(2 * NA_HEADS), blk(3 * NA_HEADS),
                  pl.BlockSpec((1, NA_NDR + 1, w2), lambda h, b: (h, 0, 0))],
        out_specs=blk(0),
        scratch_shapes=[pltpu.VMEM((3, NA_TQ, NA_TK), F32),
                        pltpu.VMEM((NA_NDR + 1, GRID_W, w2), F32)],
        compiler_params=_cparams(("parallel", "arbitrary")),
        name="na_attention",
    )(p, p, p, p, rpb_pad)


def _conv_kernel(u_ref, w_ref, b_ref, o_ref):
    u = u_ref[0].astype(F32)
    row = lax.broadcasted_iota(jnp.int32, u.shape, 0)
    seg_first = jnp.logical_or(row == 0, row == SEQ)
    seg_last = jnp.logical_or(row == SEQ - 1, row == LTOT - 1)
    up = jnp.where(seg_first, 0.0, pltpu.roll(u, 1, 0))
    un = jnp.where(seg_last, 0.0, pltpu.roll(u, LTOT - 1, 0))
    w = w_ref[...]
    y = w[0:1] * up + w[1:2] * u + w[2:3] * un + b_ref[...]
    o_ref[0] = _silu(y).astype(o_ref.dtype)


def _ssd_conv(p, conv_w, conv_b):
    tc = 512
    off = SSD_D_INNER // tc
    return pl.pallas_call(
        _conv_kernel,
        out_shape=jax.ShapeDtypeStruct((BATCH, LTOT, SSD_CONV_CH), BF16),
        grid=(BATCH, SSD_CONV_CH // tc),
        in_specs=[pl.BlockSpec((1, LTOT, tc), lambda b, j: (b, 0, off + j)),
                  pl.BlockSpec((3, tc), lambda b, j: (0, j)),
                  pl.BlockSpec((1, tc), lambda b, j: (0, j))],
        out_specs=pl.BlockSpec((1, LTOT, tc), lambda b, j: (b, 0, j)),
        compiler_params=_cparams(("parallel", "parallel")),
        name="ssd_conv",
    )(p, conv_w, conv_b.reshape(1, SSD_CONV_CH))


def _bf16_pieces(v):
    hi = v.astype(BF16)
    r1 = v - hi.astype(F32)
    mid = r1.astype(BF16)
    lo = (r1 - mid.astype(F32)).astype(BF16)
    return hi, mid, lo


def _split3(v):
    hi, mid, lo = _bf16_pieces(v)
    lane = lax.broadcasted_iota(jnp.int32, v.shape, 1)
    return jnp.where(lane < SSD_DL, hi, jnp.where(lane < 2 * SSD_DL, mid, lo))


def _ssd_kernel(x_ref, b_ref, c_ref, z_ref, dtc_ref, dtr_ref, pc_ref, pr_ref, dskip_ref,
                y_ref, yacc_ref, s_ref):
    q = SSD_CHUNK
    hp = SSD_HPG
    li = lax.broadcasted_iota(jnp.int32, (q, q), 0)
    si = lax.broadcasted_iota(jnp.int32, (q, q), 1)
    lower = li >= si
    upper = li <= si
    tri_lo = lower.astype(BF16)
    tri_up = upper.astype(BF16)
    tri_lanes = (jnp.concatenate([tri_lo] * 3, axis=1), jnp.concatenate([tri_up] * 3, axis=1))
    tri_rows = (jnp.concatenate([tri_up] * 3, axis=0), jnp.concatenate([tri_lo] * 3, axis=0))
    left = si < SSD_HEAD_DIM

    def expand_matrix(d):
        r = lax.broadcasted_iota(jnp.int32, (3 * SSD_DL, SSD_GW), 0) % SSD_DL
        c = lax.broadcasted_iota(jnp.int32, (3 * SSD_DL, SSD_GW), 1) // SSD_HEAD_DIM
        return (r == d * hp + c).astype(BF16)

    expand = (expand_matrix(0), expand_matrix(1))

    bias_c = pc_ref[0, 0:1, :]
    a_c = -jnp.exp(pc_ref[0, 1:2, :]) * LOG2E
    bias_r = pr_ref[0, :, 0:1]
    a_r = -jnp.exp(pr_ref[0, :, 1:2]) * LOG2E

    s_ref[...] = jnp.zeros_like(s_ref)

    def prep(c, d):
        r0 = pl.multiple_of(c * q, q)
        dtc = _softplus(dtc_ref[0, 0, pl.ds(r0, q), :] + bias_c)
        acum_c = jnp.dot(tri_lanes[d], jnp.concatenate(_bf16_pieces(dtc * a_c), axis=0),
                         preferred_element_type=F32)
        dtr = _softplus(dtr_ref[0, 0, c] + bias_r)
        acum_r = jnp.dot(jnp.concatenate(_bf16_pieces(dtr * a_r), axis=1), tri_rows[d],
                         preferred_element_type=F32)
        tot_c = acum_c[q - 1:q, :] if d == 0 else acum_c[0:1, :]
        ea_c = jnp.exp2(acum_c)
        dw_c = dtc * jnp.exp2(tot_c - acum_c)
        ldt = jnp.log2(dtr)
        return _split3(ea_c), _split3(dw_c), acum_c, acum_r - ldt, ldt

    def chunk(c, d, prepared):
        ea_s, dw_s, acum_c, acum_r, ldt = prepared
        ea_x = jnp.dot(ea_s, expand[d], preferred_element_type=F32)
        dw_x = jnp.dot(dw_s, expand[d], preferred_element_type=F32)
        r0 = pl.multiple_of(c * q, q)
        mask = lower if d == 0 else upper
        xb = x_ref[0, pl.ds(r0, q), :]
        bm = b_ref[0, pl.ds(r0, q), :]
        cm = c_ref[0, pl.ds(r0, q), :]
        etot_x = ea_x[q - 1:q, :] if d == 0 else ea_x[0:1, :]
        cb = jnp.where(mask, lax.dot_general(cm, bm, _NT, preferred_element_type=F32), 0.0)
        bt = bm.astype(F32).T.astype(BF16)
        sprev = s_ref[d]
        yoff = jnp.dot(cm, sprev.astype(BF16), preferred_element_type=F32)
        s_ref[d] = sprev * etot_x + jnp.dot(bt, xb * dw_x.astype(BF16), preferred_element_type=F32)
        zero = jnp.zeros((q, q), BF16)
        for p in range(hp // 2):
            cs = slice(p * q, (p + 1) * q)
            m_pair = []
            for ln in (d * hp + 2 * p, d * hp + 2 * p + 1):
                e = jnp.minimum(acum_c[:, ln:ln + 1] - acum_r[ln:ln + 1, :], ldt[ln:ln + 1, :])
                m_pair.append((cb * jnp.exp2(e)).astype(BF16))
            x_p = xb[:, cs]
            rhs = jnp.concatenate([jnp.where(left, x_p, zero), jnp.where(left, zero, x_p)], axis=0)
            yd = jnp.dot(jnp.concatenate(m_pair, axis=1), rhs, preferred_element_type=F32)
            yacc_ref[d, pl.ds(r0, q), cs] = yd + ea_x[:, cs] * yoff[:, cs]

    lat_chunks = SSD_NCHUNK - SSD_CCHUNK

    def fwd_chunk(k):
        return jnp.where(k < SSD_CCHUNK, lat_chunks + k, k - SSD_CCHUNK)

    def bwd_chunk(k):
        return SSD_NCHUNK - 1 - k

    def step(k, carry):
        kn = jnp.minimum(k + 1, SSD_NCHUNK - 1)
        nxt = (prep(fwd_chunk(kn), 0), prep(bwd_chunk(kn), 1))
        chunk(fwd_chunk(k), 0, carry[0])
        chunk(bwd_chunk(k), 1, carry[1])
        return nxt

    lax.fori_loop(0, SSD_NCHUNK, step, (prep(fwd_chunk(0), 0), prep(bwd_chunk(0), 1)))

    y = yacc_ref[0] + yacc_ref[1] + x_ref[0].astype(F32) * dskip_ref[0]
    y_ref[0] = (y * _silu(z_ref[0].astype(F32))).astype(y_ref.dtype)


def _ssd_scan(xbc, p, dt_raw, dt_bias, a_log, d_skip):
    g, hp, dl = SSD_GROUPS, SSD_HPG, SSD_DL
    dt = dt_raw.reshape(BATCH, LTOT, 2, g, hp).transpose(0, 3, 1, 2, 4).reshape(BATCH, g, LTOT, dl)
    dt_col = jnp.tile(dt, (1, 1, 1, 3))
    dt_row = dt.reshape(BATCH, g, SSD_NCHUNK, SSD_CHUNK, dl).transpose(0, 1, 2, 4, 3)
    par = jnp.stack([dt_bias, a_log]).reshape(2, 2, g, hp).transpose(2, 0, 1, 3).reshape(g, 2, dl)
    par_col = jnp.pad(jnp.tile(par, (1, 1, 3)), ((0, 0), (0, 6), (0, 0)))
    par_row = jnp.pad(par.transpose(0, 2, 1), ((0, 0), (0, 0), (0, 126)))
    dskip = jnp.repeat(d_skip, SSD_HEAD_DIM).reshape(g, 1, SSD_GW)
    xoff = SSD_D_INNER // SSD_STATE
    return pl.pallas_call(
        _ssd_kernel,
        out_shape=jax.ShapeDtypeStruct((BATCH, LTOT, SSD_D_INNER), BF16),
        grid=(BATCH, g),
        in_specs=[pl.BlockSpec((1, LTOT, SSD_GW), lambda b, j: (b, 0, j)),
                  pl.BlockSpec((1, LTOT, SSD_STATE), lambda b, j: (b, 0, xoff + j)),
                  pl.BlockSpec((1, LTOT, SSD_STATE), lambda b, j: (b, 0, xoff + g + j)),
                  pl.BlockSpec((1, LTOT, SSD_GW), lambda b, j: (b, 0, j)),
                  pl.BlockSpec((1, 1, LTOT, 3 * dl), lambda b, j: (b, j, 0, 0)),
                  pl.BlockSpec((1, 1, SSD_NCHUNK, dl, SSD_CHUNK), lambda b, j: (b, j, 0, 0, 0)),
                  pl.BlockSpec((1, 8, 3 * dl), lambda b, j: (j, 0, 0)),
                  pl.BlockSpec((1, dl, 128), lambda b, j: (j, 0, 0)),
                  pl.BlockSpec((1, 1, SSD_GW), lambda b, j: (j, 0, 0))],
        out_specs=pl.BlockSpec((1, LTOT, SSD_GW), lambda b, j: (b, 0, j)),
        scratch_shapes=[pltpu.VMEM((2, LTOT, SSD_GW), F32),
                        pltpu.VMEM((2, SSD_STATE, SSD_GW), F32)],
        compiler_params=_cparams(("parallel", "parallel")),
        name="ssd_scan",
    )(xbc, xbc, xbc, p, dt_col, dt_row, par_col, par_row, dskip)


def _sg_kernel(z_ref, u_ref, v_ref, g_ref, b_ref, ws_ref, bs_ref, o_ref):
    v = v_ref[...].astype(F32)
    mu = jnp.mean(v, axis=-1, keepdims=True)
    vc = v - mu
    var = jnp.mean(vc * vc, axis=-1, keepdims=True)
    vn = (vc * lax.rsqrt(var + EPS) * g_ref[...] + b_ref[...]).astype(BF16)
    for r in range(v.shape[0] // SG_CHUNK):
        rs = slice(r * SG_CHUNK, (r + 1) * SG_CHUNK)
        for g in range(SG_GROUPS):
            cs = slice(g * SG_GW, (g + 1) * SG_GW)
            sv = jnp.dot(ws_ref[g], vn[rs, cs], preferred_element_type=F32) + bs_ref[:, g:g + 1]
            o_ref[rs, cs] = (u_ref[rs, cs].astype(F32) * sv * _silu(z_ref[rs, cs].astype(F32))).astype(o_ref.dtype)


def _sg_gate(p, ln_g, ln_b, w_s, b_s_t):
    rows = p.shape[0]
    tr = 2 * SG_CHUNK
    blk = lambda j: pl.BlockSpec((tr, SG_HALF), lambda i: (i, j))
    return pl.pallas_call(
        _sg_kernel,
        out_shape=jax.ShapeDtypeStruct((rows, SG_HALF), BF16),
        grid=(rows // tr,),
        in_specs=[blk(0), blk(1), blk(2),
                  pl.BlockSpec((1, SG_HALF), lambda i: (0, 0)),
                  pl.BlockSpec((1, SG_HALF), lambda i: (0, 0)),
                  pl.BlockSpec((SG_GROUPS, SG_CHUNK, SG_CHUNK), lambda i: (0, 0, 0)),
                  pl.BlockSpec((SG_CHUNK, SG_GROUPS), lambda i: (0, 0))],
        out_specs=blk(0),
        compiler_params=_cparams(("parallel",)),
        name="sg_gate",
    )(p, p, p, ln_g.reshape(1, SG_HALF), ln_b.reshape(1, SG_HALF), w_s, b_s_t)


def kernel(x, c, ctx, c_ctx, norm_g, ada_w, ada_b, na_w_in, na_rpb, na_w_out,
           ssd_w_in, ssd_conv_w, ssd_conv_b, ssd_dt_bias, ssd_a_log, ssd_d_skip,
           ssd_norm_g, ssd_w_out, sg_w_in, sg_ln_g, sg_ln_b, sg_w_s, sg_b_s, sg_w_out,
           final_norm_g):
    d = D_MODEL
    c_rows = jnp.concatenate([c, c_ctx[None], jnp.zeros((MOD_ROWS - BATCH - 1, d), F32)], axis=0)
    mods = _ada_all(c_rows, ada_w, ada_b)

    h, n = _norm_mod(ctx, x, norm_g[0], mods[0])

    na_w_out_b = _outproj_weights(na_w_out)
    ssd_w_out_b = _outproj_weights(ssd_w_out, ssd_norm_g)
    sg_w_out_b = _outproj_weights(sg_w_out)

    for i in range(DEPTH):
        kind, j = i % N_MIXERS, i // N_MIXERS
        need_ctx = i < DEPTH - 1
        if kind == 0:
            p = _proj(n, na_w_in, j, n=4 * d, out_dtype=BF16, scale_cols=d, scale=NA_HEAD_DIM ** -0.5 * LOG2E)
            y = _na_attention(p.reshape(BATCH, LTOT, 4 * d), na_rpb[j], need_ctx).reshape(ROWS, d)
            w_out = na_w_out_b
        elif kind == 1:
            p = _proj(n, ssd_w_in, j, n=SSD_MAIN, out_dtype=BF16).reshape(BATCH, LTOT, SSD_MAIN)
            dt_raw = _proj(n, ssd_w_in, j, n=2 * SSD_HEADS, out_dtype=F32, tn=2 * SSD_HEADS, col0=SSD_MAIN)
            xbc = _ssd_conv(p, ssd_conv_w[j], ssd_conv_b[j])
            y = _ssd_scan(xbc, p, dt_raw, ssd_dt_bias[j], ssd_a_log[j], ssd_d_skip[j]).reshape(ROWS, SSD_D_INNER)
            w_out = ssd_w_out_b
        else:
            p = _proj(n, sg_w_in, j, n=3 * SG_HALF, out_dtype=BF16, gelu_from_col=SG_HALF)
            y = _sg_gate(p, sg_ln_g[j], sg_ln_b[j], sg_w_s[j].astype(BF16), jnp.transpose(sg_b_s[j]))
            w_out = sg_w_out_b

        rms_a = kind == 1
        if need_ctx:
            h, n = _outproj(y, w_out, j, h, mods, i, norm_g[i + 1], mods[i + 1], final=False, rms_a=rms_a)
        else:
            out = _outproj(y, w_out, j, h, mods, i, final_norm_g, mods[i], final=True, rms_a=rms_a)

    return out
```

```python
import functools
import math

import numpy as np
import jax
import jax.numpy as jnp
from jax import lax
from jax.experimental import pallas as pl
from jax.experimental.pallas import tpu as pltpu

D_MODEL = 2048
BATCH = 4
SEQ = 2048
DEPTH = 4
GRID_W = 64
CTX_LEN = 256
N_MIXERS = 3
EPS = 1e-6
LTOT = SEQ + CTX_LEN
ROWS = BATCH * LTOT

NA_HEADS = 16
NA_HEAD_DIM = D_MODEL // NA_HEADS
NA_WIN_ROWS = 8
NA_WIN_COLS = 16
NA_ROWS = SEQ // GRID_W
NA_QROWS = 4
NA_KROWS = NA_QROWS + NA_WIN_ROWS
NA_TQ = NA_QROWS * GRID_W
NA_TK = NA_KROWS * GRID_W
NA_TILES = NA_ROWS // NA_QROWS
NA_NDR = 2 * NA_WIN_ROWS - 1

SSD_D_INNER = 2 * D_MODEL
SSD_HEAD_DIM = 64
SSD_HEADS = SSD_D_INNER // SSD_HEAD_DIM
SSD_GROUPS = 8
SSD_STATE = 128
SSD_CHUNK = 128
SSD_GN = SSD_GROUPS * SSD_STATE
SSD_CONV_CH = SSD_D_INNER + 2 * SSD_GN
SSD_MAIN = SSD_D_INNER + SSD_CONV_CH
SSD_HPG = SSD_HEADS // SSD_GROUPS
SSD_GW = SSD_HPG * SSD_HEAD_DIM
SSD_NCHUNK = LTOT // SSD_CHUNK
SSD_CCHUNK = CTX_LEN // SSD_CHUNK
SSD_DL = 2 * SSD_HPG

SG_HALF = 3 * D_MODEL
SG_GROUPS = 16
SG_CHUNK = 128
SG_GW = SG_HALF // SG_GROUPS

NEG = -1e30
VMEM_LIMIT = 56 * 1024 * 1024
MOD_ROWS = 8

PROJ_TM, PROJ_TN = 1536, 1024
OUT_TM = 768
LOG2E = math.log2(math.e)

F32 = jnp.float32
BF16 = jnp.bfloat16
_NT = (((1,), (1,)), ((), ()))


def _cparams(sem):
    return pltpu.CompilerParams(dimension_semantics=sem, vmem_limit_bytes=VMEM_LIMIT)


def _silu(x):
    return x * (1.0 / (1.0 + jnp.exp2(x * -LOG2E)))


def _gelu_tanh(x):
    c = math.sqrt(2.0 / math.pi)
    return 0.5 * x * (1.0 + jnp.tanh(c * (x + 0.044715 * (x * x * x))))


def _softplus(x):
    return jnp.maximum(x, 0.0) + jnp.log(1.0 + jnp.exp(-jnp.abs(x)))


def _ada_kernel(c_ref, w_ref, b_ref, o_ref):
    a = _silu(c_ref[...]).astype(BF16)
    o_ref[0] = jnp.dot(a, w_ref[0].astype(BF16), preferred_element_type=F32) + b_ref[0]


def _ada_all(c_rows, ada_w, ada_b):
    tn = 1024
    return pl.pallas_call(
        _ada_kernel,
        out_shape=jax.ShapeDtypeStruct((DEPTH, MOD_ROWS, 3 * D_MODEL), F32),
        grid=(DEPTH, 3 * D_MODEL // tn),
        in_specs=[pl.BlockSpec((MOD_ROWS, D_MODEL), lambda l, j: (0, 0)),
                  pl.BlockSpec((1, D_MODEL, tn), lambda l, j: (l, 0, j)),
                  pl.BlockSpec((1, 1, tn), lambda l, j: (l, 0, j))],
        out_specs=pl.BlockSpec((1, MOD_ROWS, tn), lambda l, j: (l, 0, j)),
        compiler_params=_cparams(("arbitrary", "arbitrary")),
        name="ada_mod",
    )(c_rows, ada_w, ada_b.reshape(DEPTH, 1, 3 * D_MODEL))


def _mod_rows(mod_ref, row):
    m = mod_ref[pl.ds(row, 1), :]
    return m[:, :D_MODEL], m[:, D_MODEL:2 * D_MODEL], m[:, 2 * D_MODEL:]


def _norm_mod_kernel(ctx_ref, x_ref, g_ref, mod_ref, h_ref, o_ref):
    b = pl.program_id(0)
    r = pl.program_id(1)
    is_ctx = r == LTOT // CTX_LEN - 1
    shift, scale, _ = _mod_rows(mod_ref, jnp.where(is_ctx, BATCH, b))
    x = jnp.where(is_ctx, ctx_ref[0], x_ref[0])
    h_ref[...] = x
    y = x * lax.rsqrt(jnp.mean(x * x, axis=-1, keepdims=True) + EPS) * g_ref[...]
    o_ref[...] = (y * (1.0 + scale) + shift).astype(o_ref.dtype)


def _norm_mod(ctx, x, g, mod):
    per = LTOT // CTX_LEN
    rows = pl.BlockSpec((CTX_LEN, D_MODEL), lambda b, r: (b * per + r, 0))
    return pl.pallas_call(
        _norm_mod_kernel,
        out_shape=(jax.ShapeDtypeStruct((ROWS, D_MODEL), F32), jax.ShapeDtypeStruct((ROWS, D_MODEL), BF16)),
        grid=(BATCH, per),
        in_specs=[pl.BlockSpec((1, CTX_LEN, D_MODEL), lambda b, r: (b, 0, 0)),
                  pl.BlockSpec((1, CTX_LEN, D_MODEL), lambda b, r: (b, jnp.minimum(r, per - 2), 0)),
                  pl.BlockSpec((1, D_MODEL), lambda b, r: (0, 0)),
                  pl.BlockSpec((MOD_ROWS, 3 * D_MODEL), lambda b, r: (0, 0))],
        out_specs=(rows, rows),
        compiler_params=_cparams(("parallel", "parallel")),
        name="norm_mod",
    )(ctx, x, g.reshape(1, D_MODEL), mod)


def _proj_kernel(a_ref, w_ref, o_ref, wb_ref, *, gelu_from, scale_tiles, scale):
    j = pl.program_id(0)
    i = pl.program_id(1)

    @pl.when(i == 0)
    def _():
        w = w_ref[0]
        if scale_tiles:
            w = w * jnp.where(j < scale_tiles, scale, 1.0)
        wb_ref[...] = w.astype(BF16)

    acc = jnp.dot(a_ref[...], wb_ref[...], preferred_element_type=F32)
    if gelu_from is None:
        o_ref[...] = acc.astype(o_ref.dtype)
    else:
        @pl.when(j >= gelu_from)
        def _():
            o_ref[...] = _gelu_tanh(acc).astype(o_ref.dtype)

        @pl.when(j < gelu_from)
        def _():
            o_ref[...] = acc.astype(o_ref.dtype)


def _proj(a, w, layer, *, n, out_dtype, tn=PROJ_TN, col0=0, gelu_from_col=None, scale_cols=0, scale=1.0):
    m, kdim = a.shape
    off = col0 // tn
    return pl.pallas_call(
        functools.partial(_proj_kernel,
                          gelu_from=None if gelu_from_col is None else gelu_from_col // tn,
                          scale_tiles=scale_cols // tn, scale=scale),
        out_shape=jax.ShapeDtypeStruct((m, n), out_dtype),
        grid=(n // tn, m // PROJ_TM),
        in_specs=[pl.BlockSpec((PROJ_TM, kdim), lambda j, i: (i, 0)),
                  pl.BlockSpec((1, kdim, tn), lambda j, i: (layer, 0, off + j))],
        out_specs=pl.BlockSpec((PROJ_TM, tn), lambda j, i: (i, j)),
        scratch_shapes=[pltpu.VMEM((kdim, tn), BF16)],
        compiler_params=_cparams(("parallel", "arbitrary")),
        name="proj",
    )(a, w)


def _outproj_kernel(a_ref, w_ref, h_ref, gate_ref, ng_ref, nmod_ref, *rest, nj, tn, final, resident, rms_a):
    if final:
        n_ref, hs_ref, ss_ref, ra_ref = rest
        hn_ref = None
    else:
        hn_ref, n_ref, hs_ref, ss_ref, ra_ref = rest
    i = pl.program_id(0)
    j = pl.program_id(1)

    if rms_a:
        @pl.when(j == 0)
        def _():
            kdim = a_ref.shape[1]
            ssq = jnp.zeros((OUT_TM, 1), F32)
            for c0 in range(0, kdim, 512):
                af = a_ref[:, c0:c0 + 512].astype(F32)
                ssq = ssq + jnp.sum(af * af, axis=-1, keepdims=True)
            ra_ref[...] = lax.rsqrt(ssq * (1.0 / kdim) + EPS)
    per = LTOT // OUT_TM
    b = i // per
    last = i % per == per - 1
    top = slice(0, OUT_TM - CTX_LEN)
    bot = slice(OUT_TM - CTX_LEN, OUT_TM)
    if final:
        n_ref = n_ref.at[0]

    def rows(ref, cols):
        lat = ref[pl.ds(b, 1), cols]
        return lat, jnp.where(last, ref[BATCH:BATCH + 1, cols], lat)

    w = w_ref[0, j] if resident else w_ref[0, 0]
    acc = jnp.dot(a_ref[...], w, preferred_element_type=F32)
    if rms_a:
        acc = acc * ra_ref[...]
    g_top, g_bot = rows(gate_ref.at[0], slice(None))
    hn_t = h_ref[top, :] + g_top * acc[top, :]
    hn_b = h_ref[bot, :] + g_bot * acc[bot, :]
    hs_ref[j, top, :] = hn_t
    hs_ref[j, bot, :] = hn_b
    if not final:
        hn_ref[top, :] = hn_t
        hn_ref[bot, :] = hn_b
    sq_t = jnp.sum(hn_t * hn_t, axis=-1, keepdims=True)
    sq_b = jnp.sum(hn_b * hn_b, axis=-1, keepdims=True)

    @pl.when(j == 0)
    def _():
        ss_ref[top, :] = sq_t
        ss_ref[bot, :] = sq_b

    @pl.when(j > 0)
    def _():
        ss_ref[top, :] += sq_t
        ss_ref[bot, :] += sq_b

    @pl.when(j == nj - 1)
    def _():
        inv = lax.rsqrt(ss_ref[...] * (1.0 / D_MODEL) + EPS)
        for jj in range(nj):
            cs = slice(jj * tn, (jj + 1) * tn)
            y = hs_ref[jj] * inv
            if final:
                n_ref[:, cs] = y * ng_ref[:, cs]
            else:
                sh_top, sh_bot = rows(nmod_ref, cs)
                sc_top, sc_bot = rows(nmod_ref, slice(D_MODEL + jj * tn, D_MODEL + (jj + 1) * tn))
                g_top = ng_ref[:, cs] * (1.0 + sc_top)
                g_bot = ng_ref[:, cs] * (1.0 + sc_bot)
                n_ref[top, cs] = (y[top, :] * g_top + sh_top).astype(n_ref.dtype)
                n_ref[bot, cs] = (y[bot, :] * g_bot + sh_bot).astype(n_ref.dtype)


def _outproj_rows_kernel(a_ref, w_ref, h_ref, gate_ref, ng_ref, nmod_ref, *rest, nj, tn, final):
    if final:
        n_ref, keep_ref = rest
        n_ref = n_ref.at[0]
    else:
        keep_ref, n_ref = rest
    i = pl.program_id(0)
    per = LTOT // OUT_TM
    b = i // per
    last = i % per == per - 1
    top = slice(0, OUT_TM - CTX_LEN)
    bot = slice(OUT_TM - CTX_LEN, OUT_TM)

    def rows(ref, cols):
        lat = ref[pl.ds(b, 1), cols]
        return lat, jnp.where(last, ref[BATCH:BATCH + 1, cols], lat)

    a = a_ref[...]
    ssq_t = jnp.zeros((OUT_TM - CTX_LEN, 1), F32)
    ssq_b = jnp.zeros((CTX_LEN, 1), F32)
    for jj in range(nj):
        cs = slice(jj * tn, (jj + 1) * tn)
        acc = jnp.dot(a, w_ref[0, jj], preferred_element_type=F32)
        g_top, g_bot = rows(gate_ref.at[0], cs)
        hn_t = h_ref[top, cs] + g_top * acc[top, :]
        hn_b = h_ref[bot, cs] + g_bot * acc[bot, :]
        keep_ref[top, cs] = hn_t
        keep_ref[bot, cs] = hn_b
        ssq_t = ssq_t + jnp.sum(hn_t * hn_t, axis=-1, keepdims=True)
        ssq_b = ssq_b + jnp.sum(hn_b * hn_b, axis=-1, keepdims=True)
    inv_t = lax.rsqrt(ssq_t * (1.0 / D_MODEL) + EPS)
    inv_b = lax.rsqrt(ssq_b * (1.0 / D_MODEL) + EPS)
    for jj in range(nj):
        cs = slice(jj * tn, (jj + 1) * tn)
        if final:
            n_ref[top, cs] = keep_ref[top, cs] * inv_t * ng_ref[:, cs]
            n_ref[bot, cs] = keep_ref[bot, cs] * inv_b * ng_ref[:, cs]
        else:
            sh_top, sh_bot = rows(nmod_ref, cs)
            sc_top, sc_bot = rows(nmod_ref, slice(D_MODEL + jj * tn, D_MODEL + (jj + 1) * tn))
            n_ref[top, cs] = (keep_ref[top, cs] * inv_t * (ng_ref[:, cs] * (1.0 + sc_top)) + sh_top).astype(n_ref.dtype)
            n_ref[bot, cs] = (keep_ref[bot, cs] * inv_b * (ng_ref[:, cs] * (1.0 + sc_bot)) + sh_bot).astype(n_ref.dtype)


def _outproj_tn(kdim):
    return 512


def _wtile_kernel(w_ref, *rest):
    if len(rest) == 2:
        g_ref, o_ref = rest
        o_ref[0, 0] = (w_ref[0] * g_ref[0]).astype(o_ref.dtype)
    else:
        o_ref, = rest
        o_ref[0, 0] = w_ref[0].astype(o_ref.dtype)


def _outproj_weights(w, row_gain=None):
    nl, kdim, _ = w.shape
    tn = _outproj_tn(kdim)
    in_specs = [pl.BlockSpec((1, kdim, tn), lambda l, j: (l, 0, j))]
    args = [w]
    if row_gain is not None:
        in_specs.append(pl.BlockSpec((1, kdim, 1), lambda l, j: (l, 0, 0)))
        args.append(row_gain.reshape(nl, kdim, 1))
    return pl.pallas_call(
        _wtile_kernel,
        out_shape=jax.ShapeDtypeStruct((nl, D_MODEL // tn, kdim, tn), BF16),
        grid=(nl, D_MODEL // tn),
        in_specs=in_specs,
        out_specs=pl.BlockSpec((1, 1, kdim, tn), lambda l, j: (l, j, 0, 0)),
        compiler_params=_cparams(("parallel", "parallel")),
        name="wtile",
    )(*args)


def _outproj(a, w, layer, h, mods, mod_layer, next_g, next_mod, *, final, rms_a=False):
    m, kdim = a.shape
    tn = _outproj_tn(kdim)
    nj = D_MODEL // tn
    gate_off = 2 * D_MODEL // tn
    resident = kdim <= 2 * D_MODEL
    if resident:
        w_spec = pl.BlockSpec((1, nj, kdim, tn), lambda i, j: (layer, 0, 0, 0), pipeline_mode=pl.Buffered(1))
    else:
        w_spec = pl.BlockSpec((1, 1, kdim, tn), lambda i, j: (layer, j, 0, 0))
    if kdim <= D_MODEL and not rms_a:
        per = LTOT // OUT_TM
        row = pl.BlockSpec((OUT_TM, D_MODEL), lambda i: (i, 0))
        if final:
            out_shape = jax.ShapeDtypeStruct((BATCH, SEQ, D_MODEL), F32)
            out_specs = pl.BlockSpec((1, OUT_TM, D_MODEL), lambda i: (i // per, i % per, 0))
            scratch = [pltpu.VMEM((OUT_TM, D_MODEL), F32)]
        else:
            out_shape = (jax.ShapeDtypeStruct((m, D_MODEL), F32), jax.ShapeDtypeStruct((m, D_MODEL), BF16))
            out_specs = (row, row)
            scratch = []
        return pl.pallas_call(
            functools.partial(_outproj_rows_kernel, nj=nj, tn=tn, final=final),
            out_shape=out_shape,
            grid=(m // OUT_TM,),
            in_specs=[pl.BlockSpec((OUT_TM, kdim), lambda i: (i, 0)),
                      pl.BlockSpec((1, nj, kdim, tn), lambda i: (layer, 0, 0, 0), pipeline_mode=pl.Buffered(1)),
                      row,
                      pl.BlockSpec((1, MOD_ROWS, D_MODEL), lambda i: (mod_layer, 0, 2)),
                      pl.BlockSpec((1, D_MODEL), lambda i: (0, 0)),
                      pl.BlockSpec((MOD_ROWS, 3 * D_MODEL), lambda i: (0, 0))],
            out_specs=out_specs,
            scratch_shapes=scratch,
            compiler_params=_cparams(("parallel",)),
            name="outproj_rows",
        )(a, w, h, mods, next_g.reshape(1, D_MODEL), next_mod)
    tile = pl.BlockSpec((OUT_TM, tn), lambda i, j: (i, j))
    full = pl.BlockSpec((OUT_TM, D_MODEL), lambda i, j: (i, 0))
    const = lambda i, j: (0, 0)
    if final:
        per = LTOT // OUT_TM
        out_shape = jax.ShapeDtypeStruct((BATCH, SEQ, D_MODEL), F32)
        out_specs = pl.BlockSpec((1, OUT_TM, D_MODEL), lambda i, j: (i // per, i % per, 0))
    else:
        out_shape = (jax.ShapeDtypeStruct((m, D_MODEL), F32), jax.ShapeDtypeStruct((m, D_MODEL), BF16))
        out_specs = (tile, full)
    return pl.pallas_call(
        functools.partial(_outproj_kernel, nj=nj, tn=tn, final=final, resident=resident, rms_a=rms_a),
        out_shape=out_shape,
        grid=(m // OUT_TM, nj),
        in_specs=[pl.BlockSpec((OUT_TM, kdim), lambda i, j: (i, 0)),
                  w_spec,
                  tile,
                  pl.BlockSpec((1, MOD_ROWS, tn), lambda i, j: (mod_layer, 0, gate_off + j)),
                  pl.BlockSpec((1, D_MODEL), const),
                  pl.BlockSpec((MOD_ROWS, 3 * D_MODEL), const)],
        out_specs=out_specs,
        scratch_shapes=[pltpu.VMEM((nj, OUT_TM, tn), F32),
                        pltpu.VMEM((OUT_TM, 1), F32),
                        pltpu.VMEM((OUT_TM, 1), F32)],
        compiler_params=_cparams(("parallel", "arbitrary")),
        name="outproj",
    )(a, w, h, mods, next_g.reshape(1, D_MODEL), next_mod)


def _na_tile_patterns():
    pats = []
    for t in (0, 1, NA_TILES - 1):
        ws = min(max(NA_QROWS * t - NA_WIN_ROWS // 2, 0), NA_ROWS - NA_KROWS)
        pat = np.full((NA_QROWS, NA_KROWS), NA_NDR, dtype=np.int64)
        for a in range(NA_QROWS):
            r = NA_QROWS * t + a
            rs = min(max(r - NA_WIN_ROWS // 2, 0), NA_ROWS - NA_WIN_ROWS)
            for jj in range(NA_KROWS):
                kabs = ws + jj
                if rs <= kabs < rs + NA_WIN_ROWS:
                    pat[a, jj] = kabs - r + NA_WIN_ROWS - 1
        pats.append(pat)
    return pats


def _na_build_bias(rpb_ref, slab_ref, bias_ref):
    w2 = 2 * GRID_W
    qc = lax.broadcasted_iota(jnp.int32, (GRID_W, w2), 0)
    lane = lax.broadcasted_iota(jnp.int32, (GRID_W, w2), 1)
    left = lane < GRID_W
    kc = jnp.where(left, lane, lane - GRID_W)
    c_start = jnp.clip(qc - NA_WIN_COLS // 2, 0, GRID_W - NA_WIN_COLS)
    col_ok = jnp.logical_and(kc >= c_start, kc < c_start + NA_WIN_COLS)
    shift = w2 - (NA_WIN_COLS - 1)
    for dr in range(NA_NDR):
        row = jnp.broadcast_to(rpb_ref[0, dr:dr + 1, :] * LOG2E, (GRID_W, w2))
        lo = pltpu.roll(row, shift, 1, stride=1, stride_axis=0)
        hi = pltpu.roll(row, (shift + GRID_W) % w2, 1, stride=1, stride_axis=0)
        slab_ref[dr] = jnp.where(col_ok, jnp.where(left, lo, hi), NEG)
    slab_ref[NA_NDR] = jnp.full((GRID_W, w2), NEG, F32)
    for p, pat in enumerate(_na_tile_patterns()):
        for a in range(NA_QROWS):
            for jp in range(NA_KROWS // 2):
                blk = jnp.where(left, slab_ref[int(pat[a, 2 * jp])], slab_ref[int(pat[a, 2 * jp + 1])])
                bias_ref[p, a * GRID_W:(a + 1) * GRID_W, jp * w2:(jp + 1) * w2] = blk


def _na_kernel(q_ref, k_ref, v_ref, z_ref, rpb_ref, o_ref, bias_ref, slab_ref, *, need_ctx):
    @pl.when(pl.program_id(1) == 0)
    def _():
        _na_build_bias(rpb_ref, slab_ref, bias_ref)

    kc = k_ref[0, SEQ:LTOT, :]
    vc = v_ref[0, SEQ:LTOT, :]

    def finish(o, l, r0):
        z = z_ref[0, pl.ds(r0, NA_TQ), :].astype(F32)
        o_ref[0, pl.ds(r0, NA_TQ), :] = (o * (1.0 / l) * _silu(z)).astype(o_ref.dtype)

    if need_ctx:
        s = lax.dot_general(q_ref[0, SEQ:LTOT, :], kc, _NT, preferred_element_type=F32)
        p = jnp.exp2(s - jnp.max(s, axis=-1, keepdims=True))
        finish(jnp.dot(p.astype(BF16), vc, preferred_element_type=F32), jnp.sum(p, axis=-1, keepdims=True), SEQ)
    else:
        o_ref[0, SEQ:LTOT, :] = jnp.zeros((CTX_LEN, NA_HEAD_DIM), o_ref.dtype)

    def key_start(t):
        ws = jnp.clip(NA_QROWS * t - NA_WIN_ROWS // 2, 0, NA_ROWS - NA_KROWS)
        return pl.multiple_of(ws * GRID_W, GRID_W)

    def scores(t):
        q = q_ref[0, pl.ds(pl.multiple_of(t * NA_TQ, NA_TQ), NA_TQ), :]
        pat = jnp.where(t == 0, 0, jnp.where(t == NA_TILES - 1, 2, 1))
        s1 = lax.dot_general(q, k_ref[0, pl.ds(key_start(t), NA_TK), :], _NT,
                             preferred_element_type=F32) + bias_ref[pat]
        return s1, lax.dot_general(q, kc, _NT, preferred_element_type=F32)

    def tile(t, carry, has_next=True):
        s1, s2 = carry
        nxt = scores(t + 1) if has_next else None
        m = jnp.maximum(jnp.max(s1, axis=-1, keepdims=True), jnp.max(s2, axis=-1, keepdims=True))
        p1 = jnp.exp2(s1 - m)
        p2 = jnp.exp2(s2 - m)
        l = jnp.sum(p1, axis=-1, keepdims=True) + jnp.sum(p2, axis=-1, keepdims=True)
        vw = v_ref[0, pl.ds(key_start(t), NA_TK), :]
        o = (jnp.dot(p1.astype(BF16), vw, preferred_element_type=F32)
             + jnp.dot(p2.astype(BF16), vc, preferred_element_type=F32))
        finish(o, l, pl.multiple_of(t * NA_TQ, NA_TQ))
        return nxt

    last = lax.fori_loop(0, NA_TILES - 1, tile, scores(0), unroll=True)
    tile(NA_TILES - 1, last, has_next=False)


def _na_attention(p, rpb, need_ctx):
    hd = NA_HEAD_DIM
    w2 = 2 * GRID_W
    rpb_pad = jnp.pad(rpb, ((0, 0), (0, NA_NDR + 1 - rpb.shape[1]), (0, w2 - rpb.shape[2])))
    blk = lambda off: pl.BlockSpec((1, LTOT, hd), lambda h, b: (b, 0, off + h))
    return pl.pallas_call(
        functools.partial(_na_kernel, need_ctx=need_ctx),
        out_shape=jax.ShapeDtypeStruct((BATCH, LTOT, D_MODEL), BF16),
        grid=(NA_HEADS, BATCH),
        in_specs=[blk(0), blk(NA_HEADS), blk(2 * NA_HEADS), blk(3 * NA_HEADS),
                  pl.BlockSpec((1, NA_NDR + 1, w2), lambda h, b: (h, 0, 0))],
        out_specs=blk(0),
        scratch_shapes=[pltpu.VMEM((3, NA_TQ, NA_TK), F32),
                        pltpu.VMEM((NA_NDR + 1, GRID_W, w2), F32)],
        compiler_params=_cparams(("parallel", "arbitrary")),
        name="na_attention",
    )(p, p, p, p, rpb_pad)


def _conv_kernel(u_ref, w_ref, b_ref, o_ref):
    u = u_ref[0].astype(F32)
    row = lax.broadcasted_iota(jnp.int32, u.shape, 0)
    seg_first = jnp.logical_or(row == 0, row == SEQ)
    seg_last = jnp.logical_or(row == SEQ - 1, row == LTOT - 1)
    up = jnp.where(seg_first, 0.0, pltpu.roll(u, 1, 0))
    un = jnp.where(seg_last, 0.0, pltpu.roll(u, LTOT - 1, 0))
    w = w_ref[...]
    y = w[0:1] * up + w[1:2] * u + w[2:3] * un + b_ref[...]
    o_ref[0] = _silu(y).astype(o_ref.dtype)


def _ssd_conv(p, conv_w, conv_b):
    tc = 512
    off = SSD_D_INNER // tc
    return pl.pallas_call(
        _conv_kernel,
        out_shape=jax.ShapeDtypeStruct((BATCH, LTOT, SSD_CONV_CH), BF16),
        grid=(BATCH, SSD_CONV_CH // tc),
        in_specs=[pl.BlockSpec((1, LTOT, tc), lambda b, j: (b, 0, off + j)),
                  pl.BlockSpec((3, tc), lambda b, j: (0, j)),
                  pl.BlockSpec((1, tc), lambda b, j: (0, j))],
        out_specs=pl.BlockSpec((1, LTOT, tc), lambda b, j: (b, 0, j)),
        compiler_params=_cparams(("parallel", "parallel")),
        name="ssd_conv",
    )(p, conv_w, conv_b.reshape(1, SSD_CONV_CH))


def _bf16_pieces(v):
    hi = v.astype(BF16)
    r1 = v - hi.astype(F32)
    mid = r1.astype(BF16)
    lo = (r1 - mid.astype(F32)).astype(BF16)
    return hi, mid, lo


def _split3(v):
    hi, mid, lo = _bf16_pieces(v)
    lane = lax.broadcasted_iota(jnp.int32, v.shape, 1)
    return jnp.where(lane < SSD_DL, hi, jnp.where(lane < 2 * SSD_DL, mid, lo))


def _ssd_kernel(x_ref, b_ref, c_ref, z_ref, dtc_ref, dtr_ref, pc_ref, pr_ref, dskip_ref,
                y_ref, yacc_ref, s_ref):
    q = SSD_CHUNK
    hp = SSD_HPG
    li = lax.broadcasted_iota(jnp.int32, (q, q), 0)
    si = lax.broadcasted_iota(jnp.int32, (q, q), 1)
    lower = li >= si
    upper = li <= si
    tri_lo = lower.astype(BF16)
    tri_up = upper.astype(BF16)
    tri_lanes = (jnp.concatenate([tri_lo] * 3, axis=1), jnp.concatenate([tri_up] * 3, axis=1))
    tri_rows = (jnp.concatenate([tri_up] * 3, axis=0), jnp.concatenate([tri_lo] * 3, axis=0))
    left = si < SSD_HEAD_DIM

    def expand_matrix(d):
        r = lax.broadcasted_iota(jnp.int32, (3 * SSD_DL, SSD_GW), 0) % SSD_DL
        c = lax.broadcasted_iota(jnp.int32, (3 * SSD_DL, SSD_GW), 1) // SSD_HEAD_DIM
        return (r == d * hp + c).astype(BF16)

    expand = (expand_matrix(0), expand_matrix(1))

    bias_c = pc_ref[0, 0:1, :]
    a_c = -jnp.exp(pc_ref[0, 1:2, :]) * LOG2E
    bias_r = pr_ref[0, :, 0:1]
    a_r = -jnp.exp(pr_ref[0, :, 1:2]) * LOG2E

    s_ref[...] = jnp.zeros_like(s_ref)

    def prep(c, d):
        r0 = pl.multiple_of(c * q, q)
        dtc = _softplus(dtc_ref[0, 0, pl.ds(r0, q), :] + bias_c)
        acum_c = jnp.dot(tri_lanes[d], jnp.concatenate(_bf16_pieces(dtc * a_c), axis=0),
                         preferred_element_type=F32)
        dtr = _softplus(dtr_ref[0, 0, c] + bias_r)
        acum_r = jnp.dot(jnp.concatenate(_bf16_pieces(dtr * a_r), axis=1), tri_rows[d],
                         preferred_element_type=F32)
        tot_c = acum_c[q - 1:q, :] if d == 0 else acum_c[0:1, :]
        ea_c = jnp.exp2(acum_c)
        dw_c = dtc * jnp.exp2(tot_c - acum_c)
        ldt = jnp.log2(dtr)
        return _split3(ea_c), _split3(dw_c), acum_c, acum_r - ldt, ldt

    def chunk(c, d, prepared):
        ea_s, dw_s, acum_c, acum_r, ldt = prepared
        ea_x = jnp.dot(ea_s, expand[d], preferred_element_type=F32)
        dw_x = jnp.dot(dw_s, expand[d], preferred_element_type=F32)
        r0 = pl.multiple_of(c * q, q)
        mask = lower if d == 0 else upper
        xb = x_ref[0, pl.ds(r0, q), :]
        bm = b_ref[0, pl.ds(r0, q), :]
        cm = c_ref[0, pl.ds(r0, q), :]
        etot_x = ea_x[q - 1:q, :] if d == 0 else ea_x[0:1, :]
        cb = jnp.where(mask, lax.dot_general(cm, bm, _NT, preferred_element_type=F32), 0.0)
        bt = bm.astype(F32).T.astype(BF16)
        sprev = s_ref[d]
        yoff = jnp.dot(cm, sprev.astype(BF16), preferred_element_type=F32)
        s_ref[d] = sprev * etot_x + jnp.dot(bt, xb * dw_x.astype(BF16), preferred_element_type=F32)
        zero = jnp.zeros((q, q), BF16)
        for p in range(hp // 2):
            cs = slice(p * q, (p + 1) * q)
            m_pair = []
            for ln in (d * hp + 2 * p, d * hp + 2 * p + 1):
                e = jnp.minimum(acum_c[:, ln:ln + 1] - acum_r[ln:ln + 1, :], ldt[ln:ln + 1, :])
                m_pair.append((cb * jnp.exp2(e)).astype(BF16))
            x_p = xb[:, cs]
            rhs = jnp.concatenate([jnp.where(left, x_p, zero), jnp.where(left, zero, x_p)], axis=0)
            yd = jnp.dot(jnp.concatenate(m_pair, axis=1), rhs, preferred_element_type=F32)
            yacc_ref[d, pl.ds(r0, q), cs] = yd + ea_x[:, cs] * yoff[:, cs]

    lat_chunks = SSD_NCHUNK - SSD_CCHUNK

    def fwd_chunk(k):
        return jnp.where(k < SSD_CCHUNK, lat_chunks + k, k - SSD_CCHUNK)

    def bwd_chunk(k):
        return SSD_NCHUNK - 1 - k

    def step(k, carry):
        kn = jnp.minimum(k + 1, SSD_NCHUNK - 1)
        nxt = (prep(fwd_chunk(kn), 0), prep(bwd_chunk(kn), 1))
        chunk(fwd_chunk(k), 0, carry[0])
        chunk(bwd_chunk(k), 1, carry[1])
        return nxt

    lax.fori_loop(0, SSD_NCHUNK, step, (prep(fwd_chunk(0), 0), prep(bwd_chunk(0), 1)))

    y = yacc_ref[0] + yacc_ref[1] + x_ref[0].astype(F32) * dskip_ref[0]
    y_ref[0] = (y * _silu(z_ref[0].astype(F32))).astype(y_ref.dtype)


def _ssd_scan(xbc, p, dt_raw, dt_bias, a_log, d_skip):
    g, hp, dl = SSD_GROUPS, SSD_HPG, SSD_DL
    dt = dt_raw.reshape(BATCH, LTOT, 2, g, hp).transpose(0, 3, 1, 2, 4).reshape(BATCH, g, LTOT, dl)
    dt_col = jnp.tile(dt, (1, 1, 1, 3))
    dt_row = dt.reshape(BATCH, g, SSD_NCHUNK, SSD_CHUNK, dl).transpose(0, 1, 2, 4, 3)
    par = jnp.stack([dt_bias, a_log]).reshape(2, 2, g, hp).transpose(2, 0, 1, 3).reshape(g, 2, dl)
    par_col = jnp.pad(jnp.tile(par, (1, 1, 3)), ((0, 0), (0, 6), (0, 0)))
    par_row = jnp.pad(par.transpose(0, 2, 1), ((0, 0), (0, 0), (0, 126)))
    dskip = jnp.repeat(d_skip, SSD_HEAD_DIM).reshape(g, 1, SSD_GW)
    xoff = SSD_D_INNER // SSD_STATE
    return pl.pallas_call(
        _ssd_kernel,
        out_shape=jax.ShapeDtypeStruct((BATCH, LTOT, SSD_D_INNER), BF16),
        grid=(BATCH, g),
        in_specs=[pl.BlockSpec((1, LTOT, SSD_GW), lambda b, j: (b, 0, j)),
                  pl.BlockSpec((1, LTOT, SSD_STATE), lambda b, j: (b, 0, xoff + j)),
                  pl.BlockSpec((1, LTOT, SSD_STATE), lambda b, j: (b, 0, xoff + g + j)),
                  pl.BlockSpec((1, LTOT, SSD_GW), lambda b, j: (b, 0, j)),
                  pl.BlockSpec((1, 1, LTOT, 3 * dl), lambda b, j: (b, j, 0, 0)),
                  pl.BlockSpec((1, 1, SSD_NCHUNK, dl, SSD_CHUNK), lambda b, j: (b, j, 0, 0, 0)),
                  pl.BlockSpec((1, 8, 3 * dl), lambda b, j: (j, 0, 0)),
                  pl.BlockSpec((1, dl, 128), lambda b, j: (j, 0, 0)),
                  pl.BlockSpec((1, 1, SSD_GW), lambda b, j: (j, 0, 0))],
        out_specs=pl.BlockSpec((1, LTOT, SSD_GW), lambda b, j: (b, 0, j)),
        scratch_shapes=[pltpu.VMEM((2, LTOT, SSD_GW), F32),
                        pltpu.VMEM((2, SSD_STATE, SSD_GW), F32)],
        compiler_params=_cparams(("parallel", "parallel")),
        name="ssd_scan",
    )(xbc, xbc, xbc, p, dt_col, dt_row, par_col, par_row, dskip)


def _sg_kernel(z_ref, u_ref, v_ref, g_ref, b_ref, ws_ref, bs_ref, o_ref):
    v = v_ref[...].astype(F32)
    mu = jnp.mean(v, axis=-1, keepdims=True)
    vc = v - mu
    var = jnp.mean(vc * vc, axis=-1, keepdims=True)
    vn = (vc * lax.rsqrt(var + EPS) * g_ref[...] + b_ref[...]).astype(BF16)
    for r in range(v.shape[0] // SG_CHUNK):
        rs = slice(r * SG_CHUNK, (r + 1) * SG_CHUNK)
        for g in range(SG_GROUPS):
            cs = slice(g * SG_GW, (g + 1) * SG_GW)
            sv = jnp.dot(ws_ref[g], vn[rs, cs], preferred_element_type=F32) + bs_ref[:, g:g + 1]
            o_ref[rs, cs] = (u_ref[rs, cs].astype(F32) * sv * _silu(z_ref[rs, cs].astype(F32))).astype(o_ref.dtype)


def _sg_gate(p, ln_g, ln_b, w_s, b_s_t):
    rows = p.shape[0]
    tr = 2 * SG_CHUNK
    blk = lambda j: pl.BlockSpec((tr, SG_HALF), lambda i: (i, j))
    return pl.pallas_call(
        _sg_kernel,
        out_shape=jax.ShapeDtypeStruct((rows, SG_HALF), BF16),
        grid=(rows // tr,),
        in_specs=[blk(0), blk(1), blk(2),
                  pl.BlockSpec((1, SG_HALF), lambda i: (0, 0)),
                  pl.BlockSpec((1, SG_HALF), lambda i: (0, 0)),
                  pl.BlockSpec((SG_GROUPS, SG_CHUNK, SG_CHUNK), lambda i: (0, 0, 0)),
                  pl.BlockSpec((SG_CHUNK, SG_GROUPS), lambda i: (0, 0))],
        out_specs=blk(0),
        compiler_params=_cparams(("parallel",)),
        name="sg_gate",
    )(p, p, p, ln_g.reshape(1, SG_HALF), ln_b.reshape(1, SG_HALF), w_s, b_s_t)


def kernel(x, c, ctx, c_ctx, norm_g, ada_w, ada_b, na_w_in, na_rpb, na_w_out,
           ssd_w_in, ssd_conv_w, ssd_conv_b, ssd_dt_bias, ssd_a_log, ssd_d_skip,
           ssd_norm_g, ssd_w_out, sg_w_in, sg_ln_g, sg_ln_b, sg_w_s, sg_b_s, sg_w_out,
           final_norm_g):
    d = D_MODEL
    c_rows = jnp.concatenate([c, c_ctx[None], jnp.zeros((MOD_ROWS - BATCH - 1, d), F32)], axis=0)
    mods = _ada_all(c_rows, ada_w, ada_b)

    h, n = _norm_mod(ctx, x, norm_g[0], mods[0])

    na_w_out_b = _outproj_weights(na_w_out)
    ssd_w_out_b = _outproj_weights(ssd_w_out, ssd_norm_g)
    sg_w_out_b = _outproj_weights(sg_w_out)

    for i in range(DEPTH):
        kind, j = i % N_MIXERS, i // N_MIXERS
        need_ctx = i < DEPTH - 1
        if kind == 0:
            p = _proj(n, na_w_in, j, n=4 * d, out_dtype=BF16, scale_cols=d, scale=NA_HEAD_DIM ** -0.5 * LOG2E)
            y = _na_attention(p.reshape(BATCH, LTOT, 4 * d), na_rpb[j], need_ctx).reshape(ROWS, d)
            w_out = na_w_out_b
        elif kind == 1:
            p = _proj(n, ssd_w_in, j, n=SSD_MAIN, out_dtype=BF16).reshape(BATCH, LTOT, SSD_MAIN)
            dt_raw = _proj(n, ssd_w_in, j, n=2 * SSD_HEADS, out_dtype=F32, tn=2 * SSD_HEADS, col0=SSD_MAIN)
            xbc = _ssd_conv(p, ssd_conv_w[j], ssd_conv_b[j])
            y = _ssd_scan(xbc, p, dt_raw, ssd_dt_bias[j], ssd_a_log[j], ssd_d_skip[j]).reshape(ROWS, SSD_D_INNER)
            w_out = ssd_w_out_b
        else:
            p = _proj(n, sg_w_in, j, n=3 * SG_HALF, out_dtype=BF16, gelu_from_col=SG_HALF)
            y = _sg_gate(p, sg_ln_g[j], sg_ln_b[j], sg_w_s[j].astype(BF16), jnp.transpose(sg_b_s[j]))
            w_out = sg_w_out_b

        rms_a = kind == 1
        if need_ctx:
            h, n = _outproj(y, w_out, j, h, mods, i, norm_g[i + 1], mods[i + 1], final=False, rms_a=rms_a)
        else:
            out = _outproj(y, w_out, j, h, mods, i, final_norm_g, mods[i], final=True, rms_a=rms_a)

    return out
```
